```python
import math, functools
import jax, jax.numpy as jnp
from jax import lax
import numpy as np

D_MODEL = 2048
BATCH = 4
SEQ = 2048
DEPTH = 1
DEC_BATCH = 32
DEC_SEQ = 32
PAST_LEN = 2048

CHUNK = 64
GDN_HEADS = 8
GDN_DK = 128
GDN_DV = 128
CONV_W = 4
SWA_HEADS = 16
SWA_KV_HEADS = 2
SWA_GROUP = SWA_HEADS // SWA_KV_HEADS
SWA_DH = 64
WINDOW = 128
WIN_PAST = WINDOW
REL_BUCKETS = 32
REL_MAX_DIST = 128
MEM_LEN = 256
MEM_HEADS = 4
MEM_DH = 128
N_EXPERTS = 32
TOP_K = 4
D_FF = D_MODEL
SWIGLU_LIMIT = 7.0
SWIGLU_ALPHA = 1.702
EPS = 1e-6

QK_A = GDN_HEADS * GDN_DK
V_A = GDN_HEADS * GDN_DV
CONV_CH = 2 * QK_A + V_A
Q_B = SWA_HEADS * SWA_DH
KV_B = SWA_KV_HEADS * SWA_DH
MIX_WIDTH = V_A + Q_B
IN_COLS = CONV_CH + V_A + 2 * GDN_HEADS + Q_B + 2 * KV_B
MEM_W = MEM_HEADS * MEM_DH

kernel_name = 'hybrid_gdn_swa_moe_stream_step'


def rmsnorm(x, g):
    xf = x.astype(jnp.float32)
    y = xf * lax.rsqrt(jnp.mean(xf * xf, axis=-1, keepdims=True) + EPS)
    return (y * g.astype(jnp.float32)).astype(x.dtype)


def l2norm(x):
    xf = x.astype(jnp.float32)
    return xf * lax.rsqrt(jnp.sum(xf * xf, axis=-1, keepdims=True) + EPS)


def gated_delta_chunked(q, k, v, g, beta, s0, chunk):
    B, T, H, DK = q.shape
    n = T // chunk

    def blocks(t):
        t = t.astype(jnp.float32).reshape((B, n, chunk, H) + t.shape[3:])
        return jnp.moveaxis(t, (1, 3), (0, 2))

    q = blocks(q) * DK ** -0.5
    k = blocks(k)
    v = blocks(v)
    beta = blocks(beta)
    gc = jnp.cumsum(blocks(g), axis=-1)
    pos = jnp.arange(chunk)
    incl = pos[:, None] >= pos[None, :]
    strict = pos[:, None] > pos[None, :]
    decay = jnp.exp(jnp.where(incl, gc[..., :, None] - gc[..., None, :], -jnp.inf))
    kb = k * beta[..., None]
    lower = jnp.where(strict, jnp.einsum('nbhcd,nbhsd->nbhcs', kb, k) * decay, 0.0)
    eye = jnp.eye(chunk, dtype=jnp.float32)
    tinv = lax.linalg.triangular_solve(eye + lower, jnp.broadcast_to(eye, lower.shape),
                                       left_side=True, lower=True)
    u = jnp.einsum('nbhcs,nbhse->nbhce', tinv, v * beta[..., None])
    w = jnp.einsum('nbhcs,nbhsd->nbhcd', tinv, kb * jnp.exp(gc)[..., None])
    a_qk = jnp.where(incl, jnp.einsum('nbhcd,nbhsd->nbhcs', q, k) * decay, 0.0)
    q_dec = q * jnp.exp(gc)[..., None]
    k_dec = k * jnp.exp(gc[..., -1:] - gc)[..., None]
    g_last = jnp.exp(gc[..., -1])

    def step(s, xs):
        u_n, w_n, q_n, k_n, a_n, gl_n = xs
        v_new = u_n - jnp.einsum('bhcd,bhde->bhce', w_n, s)
        o_n = jnp.einsum('bhcd,bhde->bhce', q_n, s) + jnp.einsum('bhcs,bhse->bhce', a_n, v_new)
        s = s * gl_n[..., None, None] + jnp.einsum('bhcd,bhce->bhde', k_n, v_new)
        return s, o_n

    s, o = lax.scan(step, s0.astype(jnp.float32), (u, w, q_dec, k_dec, a_qk, g_last))
    o = jnp.moveaxis(o, (0, 2), (1, 3)).reshape(B, T, H, v.shape[-1])
    return o, s.astype(s0.dtype)


def gdn_mixer(qkv, z, a, b, conv_hist, s0, conv_w, a_log, dt_bias, gdn_norm, chunk):
    B, T, _ = qkv.shape
    ext = jnp.concatenate([conv_hist.astype(qkv.dtype), qkv], axis=1)
    conv = ext[:, 0:T] * conv_w[0]
    for j in range(1, CONV_W):
        conv = conv + ext[:, j:j + T] * conv_w[j]
    c = jax.nn.silu(conv)
    q = l2norm(c[..., :QK_A].reshape(B, T, GDN_HEADS, GDN_DK))
    k = l2norm(c[..., QK_A:2 * QK_A].reshape(B, T, GDN_HEADS, GDN_DK))
    v = c[..., 2 * QK_A:].reshape(B, T, GDN_HEADS, GDN_DV)
    g = -jnp.exp(a_log.astype(jnp.float32)) * jax.nn.softplus(
        a.astype(jnp.float32) + dt_bias.astype(jnp.float32))
    beta = jax.nn.sigmoid(b.astype(jnp.float32))
    o, s_new = gated_delta_chunked(q, k, v, g, beta, s0, chunk)
    o = rmsnorm(o, gdn_norm) * jax.nn.silu(z.astype(jnp.float32)).reshape(B, T, GDN_HEADS, GDN_DV)
    return o.reshape(B, T, V_A).astype(qkv.dtype), ext[:, -(CONV_W - 1):], s_new


def rel_bucket(rel):
    nb = REL_BUCKETS // 2
    max_exact = nb // 2
    n = jnp.abs(rel)
    large = max_exact + (jnp.log(jnp.maximum(n, 1).astype(jnp.float32) / max_exact)
                         / math.log(REL_MAX_DIST / max_exact) * (nb - max_exact)).astype(jnp.int32)
    large = jnp.minimum(large, nb - 1)
    return jnp.where(rel > 0, nb, 0) + jnp.where(n < max_exact, n, large)


def rel_bias(table, nq, nk):
    rel = jnp.arange(nk)[None, :] - WIN_PAST - jnp.arange(nq)[:, None]
    bias = table[rel_bucket(rel)]
    return jnp.transpose(bias, (2, 0, 1)).reshape(SWA_KV_HEADS, SWA_GROUP, nq, nk).astype(jnp.float32)


def sink_attention(q, k, v, bias, mask, sinks):
    s = jnp.einsum('bnqhgd,bnkhd->bnhgqk', q, k).astype(jnp.float32) * SWA_DH ** -0.5 + bias
    s = jnp.where(mask[None, :, None, None, None, :], s, -jnp.inf)
    sk = sinks.astype(jnp.float32)[None, None, :, :, None]
    m = jnp.maximum(s.max(axis=-1), sk)
    p = jnp.exp(s - m[..., None])
    den = p.sum(axis=-1) + jnp.exp(sk - m)
    p = (p / den[..., None]).astype(v.dtype)
    return jnp.einsum('bnhgqk,bnkhd->bnqhgd', p, v)


def swa_prompt(q, k, v, sinks, table):
    B, S, _ = q.shape
    nc = S // CHUNK
    npast = WIN_PAST // CHUNK
    q = q.reshape(B, nc, CHUNK, SWA_KV_HEADS, SWA_GROUP, SWA_DH)
    k = k.reshape(B, S, SWA_KV_HEADS, SWA_DH)
    v = v.reshape(B, S, SWA_KV_HEADS, SWA_DH)

    def band(t):
        tp = jnp.pad(t, ((0, 0), (WIN_PAST, 0), (0, 0), (0, 0)))
        tp = tp.reshape(B, nc + npast, CHUNK, SWA_KV_HEADS, SWA_DH)
        return jnp.concatenate([tp[:, j:j + nc] for j in range(npast + 1)], axis=2)

    kpos = (jnp.arange(nc)[:, None] * CHUNK - WIN_PAST
            + jnp.arange(WIN_PAST + CHUNK)[None, :])
    o = sink_attention(q, band(k), band(v), rel_bias(table, CHUNK, WIN_PAST + CHUNK), kpos >= 0,
                       sinks.reshape(SWA_KV_HEADS, SWA_GROUP))
    return o.reshape(B, S, Q_B), k[:, -WIN_PAST:], v[:, -WIN_PAST:]


def swa_sample(q, k, v, sinks, table, k_past, v_past):
    B, T, _ = q.shape
    q = q.reshape(B, 1, T, SWA_KV_HEADS, SWA_GROUP, SWA_DH)
    k = jnp.concatenate([k_past.astype(k.dtype), k.reshape(B, T, SWA_KV_HEADS, SWA_DH)], axis=1)
    v = jnp.concatenate([v_past.astype(v.dtype), v.reshape(B, T, SWA_KV_HEADS, SWA_DH)], axis=1)
    mask = jnp.ones((1, WIN_PAST + T), dtype=bool)
    o = sink_attention(q, k[:, None], v[:, None], rel_bias(table, T, WIN_PAST + T), mask,
                       sinks.reshape(SWA_KV_HEADS, SWA_GROUP))
    return o.reshape(B, T, Q_B), k[:, -WIN_PAST:], v[:, -WIN_PAST:]


def memory_kv(mem, g, w_mk, w_mv):
    B = mem.shape[0]
    m = rmsnorm(mem, g)
    return ((m @ w_mk).reshape(B, -1, MEM_HEADS, MEM_DH),
            (m @ w_mv).reshape(B, -1, MEM_HEADS, MEM_DH))


def memory_cross_attention(h, mk, mv, w_mq, w_mo):
    B, T, _ = h.shape
    q = (h @ w_mq).reshape(B, T, MEM_HEADS, MEM_DH)
    s = jnp.einsum('bqhd,bkhd->bhqk', q, mk.astype(q.dtype)).astype(jnp.float32) * MEM_DH ** -0.5
    p = jax.nn.softmax(s, axis=-1).astype(h.dtype)
    o = jnp.einsum('bhqk,bkhd->bqhd', p, mv.astype(h.dtype))
    return o.reshape(B, T, MEM_W) @ w_mo


def moe_ffn(h, w_router, b_router, w_gate, b_gate, w_up, b_up, w_down, b_down):
    B, T, D = h.shape
    x = h.reshape(B * T, D)
    logits = (x @ w_router + b_router).astype(jnp.float32)
    top_v, top_i = lax.top_k(logits, TOP_K)
    wts = jax.nn.softmax(top_v, axis=-1)
    gates = jnp.einsum('nk,nke->ne', wts, jax.nn.one_hot(top_i, N_EXPERTS, dtype=jnp.float32))
    y = jnp.zeros((B * T, D), jnp.float32)
    for e in range(N_EXPERTS):
        gt = jnp.minimum(x @ w_gate[e] + b_gate[e], SWIGLU_LIMIT)
        up = jnp.clip(x @ w_up[e] + b_up[e], -SWIGLU_LIMIT, SWIGLU_LIMIT)
        act = gt * jax.nn.sigmoid(SWIGLU_ALPHA * gt) * (up + 1.0)
        y = y + gates[:, e:e + 1] * (act @ w_down[e] + b_down[e]).astype(jnp.float32)
    return y.reshape(B, T, D).astype(h.dtype)


def trunk_layer(x, p, rel_table, conv_hist, s0, mem_k, mem_v, swa_fn, gdn_chunk):
    h = rmsnorm(x, p['norm_mix'])
    proj = h @ p['w_in']
    bounds = np.cumsum([CONV_CH, V_A, GDN_HEADS, GDN_HEADS, Q_B, KV_B]).tolist()
    qkv_a, z_a, a_a, b_a, q_b, k_b, v_b = jnp.split(proj, bounds, axis=-1)
    o_a, conv_new, s_new = gdn_mixer(qkv_a, z_a, a_a, b_a, conv_hist, s0, p['conv_w'],
                                     p['gdn_a_log'], p['gdn_dt_bias'], p['gdn_norm'], gdn_chunk)
    o_b, k_win, v_win = swa_fn(q_b, k_b, v_b, p['swa_sinks'], rel_table)
    x = x + jnp.concatenate([o_a, o_b.astype(o_a.dtype)], axis=-1) @ p['w_out']
    x = x + memory_cross_attention(rmsnorm(x, p['norm_cross']), mem_k, mem_v, p['w_mq'], p['w_mo'])
    x = x + moe_ffn(rmsnorm(x, p['norm_ffn']), p['w_router'], p['b_router'], p['w_gate'], p['b_gate'],
                    p['w_up'], p['b_up'], p['w_down'], p['b_down'])
    return x, conv_new, s_new, k_win, v_win


def setup_inputs(seed: int = 0) -> dict:
    key = jax.random.key(seed)
    ks = iter(jax.random.split(key, 48))
    L = DEPTH

    def nrm(shape, scale):
        return scale * jax.random.normal(next(ks), shape, jnp.float32)

    def gain(shape):
        return 1.0 + 0.02 * jax.random.normal(next(ks), shape, jnp.float32)

    a_log = jnp.log(jax.random.uniform(next(ks), (L, GDN_HEADS), jnp.float32, 1.0, 16.0))
    dt = jnp.exp(jax.random.uniform(next(ks), (L, GDN_HEADS), jnp.float32,
                                    math.log(1e-3), math.log(1e-1)))
    dt_bias = dt + jnp.log(-jnp.expm1(-dt))
    return {
        'x_prompt': nrm((BATCH, SEQ, D_MODEL), 1.0),
        'x_sample': nrm((DEC_BATCH, DEC_SEQ, D_MODEL), 1.0),
        'cache_conv': nrm((L, DEC_BATCH, CONV_W - 1, CONV_CH), 1.0),
        'state_gdn': nrm((L, DEC_BATCH, GDN_HEADS, GDN_DK, GDN_DV), GDN_DK ** -0.5),
        'cache_swa_k': nrm((L, DEC_BATCH, WIN_PAST, SWA_KV_HEADS, SWA_DH), 1.0),
        'cache_swa_v': nrm((L, DEC_BATCH, WIN_PAST, SWA_KV_HEADS, SWA_DH), 1.0),
        'cache_mem_k': nrm((L, DEC_BATCH, MEM_LEN, MEM_HEADS, MEM_DH), 1.0),
        'cache_mem_v': nrm((L, DEC_BATCH, MEM_LEN, MEM_HEADS, MEM_DH), 1.0),
        'mem_prompt': nrm((BATCH, MEM_LEN, D_MODEL), 1.0),
        'norm_mix': gain((L, D_MODEL)),
        'w_in': nrm((L, D_MODEL, IN_COLS), D_MODEL ** -0.5),
        'conv_w': nrm((L, CONV_W, CONV_CH), CONV_W ** -0.5),
        'gdn_a_log': a_log,
        'gdn_dt_bias': dt_bias,
        'gdn_norm': gain((L, GDN_DV)),
        'swa_sinks': nrm((L, SWA_HEADS), 0.5),
        'rel_bias_table': nrm((REL_BUCKETS, SWA_HEADS), 0.2),
        'w_out': nrm((L, MIX_WIDTH, D_MODEL), MIX_WIDTH ** -0.5),
        'norm_cross': gain((L, D_MODEL)),
        'norm_mem': gain((L, D_MODEL)),
        'w_mq': nrm((L, D_MODEL, MEM_W), D_MODEL ** -0.5),
        'w_mk': nrm((L, D_MODEL, MEM_W), D_MODEL ** -0.5),
        'w_mv': nrm((L, D_MODEL, MEM_W), D_MODEL ** -0.5),
        'w_mo': nrm((L, MEM_W, D_MODEL), MEM_W ** -0.5),
        'norm_ffn': gain((L, D_MODEL)),
        'w_router': nrm((L, D_MODEL, N_EXPERTS), D_MODEL ** -0.5),
        'b_router': nrm((L, N_EXPERTS), 0.01),
        'w_gate': nrm((L, N_EXPERTS, D_MODEL, D_FF), D_MODEL ** -0.5),
        'b_gate': nrm((L, N_EXPERTS, D_FF), 0.02),
        'w_up': nrm((L, N_EXPERTS, D_MODEL, D_FF), D_MODEL ** -0.5),
        'b_up': nrm((L, N_EXPERTS, D_FF), 0.02),
        'w_down': nrm((L, N_EXPERTS, D_FF, D_MODEL), D_FF ** -0.5),
        'b_down': nrm((L, N_EXPERTS, D_MODEL), 0.02),
        'final_norm': gain((D_MODEL,)),
    }


def reference(x_prompt, x_sample, cache_conv, state_gdn, cache_swa_k, cache_swa_v, cache_mem_k,
              cache_mem_v, mem_prompt, norm_mix, w_in, conv_w, gdn_a_log, gdn_dt_bias, gdn_norm,
              swa_sinks, rel_bias_table, w_out, norm_cross, norm_mem, w_mq, w_mk, w_mv, w_mo,
              norm_ffn, w_router, b_router, w_gate, b_gate, w_up, b_up, w_down, b_down, final_norm):
    xp, xs = x_prompt, x_sample
    bp = xp.shape[0]
    p_conv, p_state, p_k, p_v, p_mk, p_mv = [], [], [], [], [], []
    s_conv, s_state, s_k, s_v = [], [], [], []
    for l in range(DEPTH):
        p = dict(norm_mix=norm_mix[l], w_in=w_in[l], conv_w=conv_w[l], gdn_a_log=gdn_a_log[l],
                 gdn_dt_bias=gdn_dt_bias[l], gdn_norm=gdn_norm[l], swa_sinks=swa_sinks[l],
                 w_out=w_out[l], norm_cross=norm_cross[l], w_mq=w_mq[l], w_mo=w_mo[l],
                 norm_ffn=norm_ffn[l], w_router=w_router[l], b_router=b_router[l],
                 w_gate=w_gate[l], b_gate=b_gate[l], w_up=w_up[l], b_up=b_up[l],
                 w_down=w_down[l], b_down=b_down[l])
        mk_p, mv_p = memory_kv(mem_prompt, norm_mem[l], w_mk[l], w_mv[l])
        zero_hist = jnp.zeros((bp, CONV_W - 1, CONV_CH), xp.dtype)
        zero_state = jnp.zeros((bp, GDN_HEADS, GDN_DK, GDN_DV), state_gdn.dtype)
        xp, c_p, st_p, k_p, v_p = trunk_layer(xp, p, rel_bias_table, zero_hist, zero_state,
                                              mk_p, mv_p, swa_prompt, CHUNK)
        swa_fn = functools.partial(swa_sample, k_past=cache_swa_k[l], v_past=cache_swa_v[l])
        xs, c_s, st_s, k_s, v_s = trunk_layer(xs, p, rel_bias_table, cache_conv[l], state_gdn[l],
                                              cache_mem_k[l], cache_mem_v[l], swa_fn, xs.shape[1])
        p_conv.append(c_p)
        p_state.append(st_p)
        p_k.append(k_p)
        p_v.append(v_p)
        p_mk.append(mk_p)
        p_mv.append(mv_p)
        s_conv.append(c_s)
        s_state.append(st_s)
        s_k.append(k_s)
        s_v.append(v_s)
    return (rmsnorm(xp, final_norm), rmsnorm(xs, final_norm),
            jnp.stack(p_conv), jnp.stack(p_state), jnp.stack(p_k), jnp.stack(p_v),
            jnp.stack(p_mk), jnp.stack(p_mv),
            jnp.stack(s_conv), jnp.stack(s_state), jnp.stack(s_k), jnp.stack(s_v))
```

```python
import functools
import math

import numpy as np
import jax
import jax.numpy as jnp
from jax import lax
from jax.experimental import pallas as pl
from jax.experimental.pallas import tpu as pltpu

F32 = jnp.float32
BF16 = jnp.bfloat16
HIGHEST = lax.Precision.HIGHEST

D_MODEL = 2048
CHUNK = 64
GDN_HEADS = 8
GDN_D = 128
CONV_W = 4
SWA_HEADS = 16
SWA_KV_HEADS = 2
SWA_GROUP = SWA_HEADS // SWA_KV_HEADS
SWA_DH = 64
WIN_PAST = 128
REL_BUCKETS = 32
REL_MAX_DIST = 128
MEM_LEN = 256
MEM_HEADS = 4
MEM_DH = 128
N_EXPERTS = 32
TOP_K = 4
D_FF = D_MODEL
SWIGLU_LIMIT = 7.0
SWIGLU_ALPHA = 1.702
EPS = 1e-6

QK_A = GDN_HEADS * GDN_D
V_A = GDN_HEADS * GDN_D
CONV_CH = 2 * QK_A + V_A
Q_B = SWA_HEADS * SWA_DH
KV_B = SWA_KV_HEADS * SWA_DH
MEM_W = MEM_HEADS * MEM_DH

LANES = 128
SUBLANES = 8
VMEM_LIMIT_BYTES = 56 * 1024 * 1024

PROJ_COLS = 5632
PROJ_TN = 512
COL_Z = CONV_CH // V_A
COL_QB = (CONV_CH + V_A) // Q_B
COL_KB = (CONV_CH + V_A + Q_B) // LANES
COL_VB = COL_KB + 1
COL_AB = COL_KB + 2

MOE_SUB = 256
MOE_ROWS = 1024
MOE_TF = 512
MOE_NF = D_FF // MOE_TF
HALF = D_MODEL // 2
NORM_ROWS = 256
XROW_TILES = HALF // LANES
YROW_TILES = D_MODEL // LANES


def _cparams(sem):
    return pltpu.CompilerParams(dimension_semantics=sem, vmem_limit_bytes=VMEM_LIMIT_BYTES)


def _rms(x, gain):
    return x * lax.rsqrt(jnp.mean(x * x, axis=-1, keepdims=True) + EPS) * gain


def _norm_matmul_kernel(x_ref, g_ref, w_ref, o_ref, h_ref):
    @pl.when(pl.program_id(1) == 0)
    def _():
        def body(r, carry):
            rows = pl.ds(pl.multiple_of(r * NORM_ROWS, NORM_ROWS), NORM_ROWS)
            h_ref[rows, :] = _rms(x_ref[rows, :], g_ref[...]).astype(BF16)
            return carry

        lax.fori_loop(0, x_ref.shape[0] // NORM_ROWS, body, 0)

    o_ref[...] = jnp.dot(h_ref[...], w_ref[...], preferred_element_type=F32)


def _norm_matmul(x, gain, w, tm, tn):
    m, k = x.shape
    n = w.shape[1]
    return pl.pallas_call(
        _norm_matmul_kernel,
        out_shape=jax.ShapeDtypeStruct((m, n), F32),
        grid=(m // tm, n // tn),
        in_specs=[pl.BlockSpec((tm, k), lambda i, j: (i, 0)),
                  pl.BlockSpec((1, k), lambda i, j: (0, 0)),
                  pl.BlockSpec((k, tn), lambda i, j: (0, j))],
        out_specs=pl.BlockSpec((tm, tn), lambda i, j: (i, j)),
        scratch_shapes=[pltpu.VMEM((tm, k), BF16)],
        compiler_params=_cparams(("arbitrary", "arbitrary")),
        name="norm_matmul",
    )(x, gain, w)


def _tri_inverse(low, c):
    row = lax.broadcasted_iota(jnp.int32, (c, c), 0)
    col = lax.broadcasted_iota(jnp.int32, (c, c), 1)
    p = jnp.where(row == col, 1.0, 0.0).astype(F32) - low
    m = low
    span = 1
    while 2 * span < c:
        mb = m.astype(BF16)
        m = jnp.dot(mb, mb, preferred_element_type=F32)
        p = p + jnp.dot(p.astype(BF16), m.astype(BF16), preferred_element_type=F32)
        span *= 2
    return p


def _gdn_kernel(qkv_ref, z_ref, ab_ref, hist_ref, s0_ref, convw_ref, gp_ref, norm_ref,
                o_ref, convnew_ref, s_ref, ext_ref, *, c):
    step = pl.program_id(1)

    @pl.when(step == 0)
    def _():
        ext_ref[0:SUBLANES, :] = hist_ref[...]
        s_ref[...] = s0_ref[...]

    ext_ref[SUBLANES:SUBLANES + c, :] = qkv_ref[...]
    base = SUBLANES - (CONV_W - 1)
    conv = ext_ref[base:base + c, :] * convw_ref[0:1, :]
    for j in range(1, CONV_W):
        conv = conv + ext_ref[base + j:base + j + c, :] * convw_ref[j:j + 1, :]
    tail = ext_ref[c:c + SUBLANES, :]
    convnew_ref[...] = tail
    ext_ref[0:SUBLANES, :] = tail

    act = conv * jax.nn.sigmoid(conv)

    ab = ab_ref[...]
    gp = gp_ref[...]
    g = -jnp.exp(gp[0:1, :]) * jax.nn.softplus(ab + gp[1:2, :])
    beta_all = jax.nn.sigmoid(ab)

    row = lax.broadcasted_iota(jnp.int32, (c, c), 0)
    col = lax.broadcasted_iota(jnp.int32, (c, c), 1)
    incl = row >= col
    strict = row > col
    tril = jnp.where(incl, 1.0, 0.0).astype(F32)
    gc_all = jnp.dot(tril, g, precision=HIGHEST, preferred_element_type=F32)
    gc_pad = jnp.concatenate([gc_all, jnp.zeros((LANES - c, LANES), F32)], axis=0) if c < LANES else gc_all
    gc_t = gc_pad.T

    scale = GDN_D ** -0.5
    for h in range(GDN_HEADS):
        lo = h * GDN_D
        qh = act[:, lo:lo + GDN_D]
        kh = act[:, QK_A + lo:QK_A + lo + GDN_D]
        vh = act[:, 2 * QK_A + lo:2 * QK_A + lo + GDN_D]
        q = qh * lax.rsqrt(jnp.sum(qh * qh, axis=-1, keepdims=True) + EPS) * scale
        k = kh * lax.rsqrt(jnp.sum(kh * kh, axis=-1, keepdims=True) + EPS)
        beta = beta_all[:, GDN_HEADS + h:GDN_HEADS + h + 1]
        gc = gc_all[:, h:h + 1]
        gc_row = gc_t[h:h + 1, 0:c]
        gc_last = gc_all[c - 1:c, h:h + 1]
        decay = jnp.exp(jnp.where(incl, gc - gc_row, -jnp.inf))
        egc = jnp.exp(gc)
        kb = k * beta
        kbf = k.astype(BF16)
        qk = lax.dot_general(jnp.concatenate([q, kb], axis=0).astype(BF16), kbf,
                             (((1,), (1,)), ((), ())), preferred_element_type=F32)
        a_qk = jnp.where(incl, qk[0:c] * decay, 0.0)
        low = jnp.where(strict, qk[c:2 * c] * decay, 0.0)
        tinv = _tri_inverse(low, c)
        rhs = jnp.concatenate([vh * beta, kb * egc], axis=1).astype(BF16)
        uw = jnp.dot(tinv.astype(BF16), rhs, preferred_element_type=F32)
        u = uw[:, 0:GDN_D]
        w = uw[:, GDN_D:2 * GDN_D]
        q_dec = q * egc
        k_dec = k * jnp.exp(gc_last - gc)
        s = s_ref[h]
        sb = s.astype(BF16)
        ws_qs = jnp.dot(jnp.concatenate([w, q_dec], axis=0).astype(BF16), sb,
                        preferred_element_type=F32)
        v_new = u - ws_qs[0:c]
        v_new_b = v_new.astype(BF16)
        o = ws_qs[c:2 * c] + jnp.dot(a_qk.astype(BF16), v_new_b, preferred_element_type=F32)
        s_ref[h] = s * jnp.exp(gc_last) + lax.dot_general(
            k_dec.astype(BF16), v_new_b, (((0,), (0,)), ((), ())), preferred_element_type=F32)
        zh = z_ref[:, lo:lo + GDN_D]
        o_ref[:, lo:lo + GDN_D] = _rms(o, norm_ref[...]) * (zh * jax.nn.sigmoid(zh))


def _gdn(proj3, hist8, s0, conv_w, gate_par, gdn_norm, c):
    b, t, _ = proj3.shape
    nchunks = t // c
    kern = functools.partial(_gdn_kernel, c=c)
    return pl.pallas_call(
        kern,
        out_shape=(jax.ShapeDtypeStruct((b, t, V_A), F32),
                   jax.ShapeDtypeStruct((b, SUBLANES, CONV_CH), F32),
                   jax.ShapeDtypeStruct((b, GDN_HEADS, GDN_D, GDN_D), F32)),
        grid=(b, nchunks),
        in_specs=[pl.BlockSpec((None, c, CONV_CH), lambda i, j: (i, j, 0)),
                  pl.BlockSpec((None, c, V_A), lambda i, j: (i, j, COL_Z)),
                  pl.BlockSpec((None, c, LANES), lambda i, j: (i, j, COL_AB)),
                  pl.BlockSpec((None, SUBLANES, CONV_CH), lambda i, j: (i, 0, 0)),
                  pl.BlockSpec((None, GDN_HEADS, GDN_D, GDN_D), lambda i, j: (i, 0, 0, 0)),
                  pl.BlockSpec((CONV_W, CONV_CH), lambda i, j: (0, 0)),
                  pl.BlockSpec((SUBLANES, LANES), lambda i, j: (0, 0)),
                  pl.BlockSpec((1, GDN_D), lambda i, j: (0, 0))],
        out_specs=(pl.BlockSpec((None, c, V_A), lambda i, j: (i, j, 0)),
                   pl.BlockSpec((None, SUBLANES, CONV_CH), lambda i, j: (i, 0, 0)),
                   pl.BlockSpec((None, GDN_HEADS, GDN_D, GDN_D), lambda i, j: (i, 0, 0, 0))),
        scratch_shapes=[pltpu.VMEM((SUBLANES + c, CONV_CH), F32)],
        compiler_params=_cparams(("arbitrary", "arbitrary")),
        name="gdn_mixer",
    )(proj3, proj3, proj3, hist8, s0, conv_w, gate_par, gdn_norm)


def _swa_kernel(*refs, nq, piece_rows, masked):
    npieces = len(piece_rows)
    q_ref = refs[0]
    k_refs = refs[1:1 + npieces]
    v_refs = refs[1 + npieces:1 + 2 * npieces]
    bias_ref, sink_ref, o_ref = refs[1 + 2 * npieces:]
    nk = sum(piece_rows)
    kcat = jnp.concatenate([r[...] for r in k_refs], axis=0).astype(BF16)
    vcat = jnp.concatenate([r[...] for r in v_refs], axis=0).astype(BF16)
    q = q_ref[...]
    if masked:
        kpos = lax.broadcasted_iota(jnp.int32, (SWA_GROUP * nq, nk), 1) + (pl.program_id(1) * nq - WIN_PAST)
        visible = kpos >= 0
    for kv in range(SWA_KV_HEADS):
        qs = jnp.concatenate(
            [q[:, (kv * SWA_GROUP + g) * SWA_DH:(kv * SWA_GROUP + g + 1) * SWA_DH] for g in range(SWA_GROUP)],
            axis=0).astype(BF16)
        kh = kcat[:, kv * SWA_DH:(kv + 1) * SWA_DH]
        vh = vcat[:, kv * SWA_DH:(kv + 1) * SWA_DH]
        s = lax.dot_general(qs, kh, (((1,), (1,)), ((), ())), preferred_element_type=F32)
        s = s * SWA_DH ** -0.5 + bias_ref[kv]
        if masked:
            s = jnp.where(visible, s, -jnp.inf)
        sk = sink_ref[kv]
        m = jnp.maximum(jnp.max(s, axis=-1, keepdims=True), sk)
        p = jnp.exp(s - m)
        den = jnp.sum(p, axis=-1, keepdims=True) + jnp.exp(sk - m)
        p = (p / den).astype(BF16)
        o = jnp.dot(p, vh, preferred_element_type=F32)
        for g in range(SWA_GROUP):
            hcol = (kv * SWA_GROUP + g) * SWA_DH
            o_ref[:, hcol:hcol + SWA_DH] = o[g * nq:(g + 1) * nq, :]


def _rel_bucket(nq, nk):
    rel = jnp.arange(nk)[None, :] - WIN_PAST - jnp.arange(nq)[:, None]
    nb = REL_BUCKETS // 2
    max_exact = nb // 2
    n = jnp.abs(rel)
    large = max_exact + (jnp.log(jnp.maximum(n, 1).astype(F32) / max_exact)
                         / math.log(REL_MAX_DIST / max_exact) * (nb - max_exact)).astype(jnp.int32)
    large = jnp.minimum(large, nb - 1)
    return jnp.where(rel > 0, nb, 0) + jnp.where(n < max_exact, n, large)


def _swa_tables(table, sinks, nq, nk):
    bias = table[_rel_bucket(nq, nk)]
    bias = jnp.transpose(bias, (2, 0, 1)).astype(F32)
    bias = bias.reshape(SWA_KV_HEADS, SWA_GROUP * nq, nk)
    sk = jnp.broadcast_to(sinks.astype(F32).reshape(SWA_KV_HEADS, SWA_GROUP, 1, 1),
                          (SWA_KV_HEADS, SWA_GROUP, nq, 1)).reshape(SWA_KV_HEADS, SWA_GROUP * nq, 1)
    return bias, sk


def _swa_prompt(proj3, table, sinks):
    b, t, _ = proj3.shape
    nq = CHUNK
    nc = t // nq
    npast = WIN_PAST // nq
    piece_rows = (nq,) * (npast + 1)
    bias, sk = _swa_tables(table, sinks, nq, WIN_PAST + nq)
    kern = functools.partial(_swa_kernel, nq=nq, piece_rows=piece_rows, masked=True)

    def kv_spec(colblk, back):
        return pl.BlockSpec((None, nq, LANES), lambda i, j: (i, jnp.maximum(j - back, 0), colblk))

    in_specs = [pl.BlockSpec((None, nq, Q_B), lambda i, j: (i, j, COL_QB))]
    in_specs += [kv_spec(COL_KB, npast - p) for p in range(npast + 1)]
    in_specs += [kv_spec(COL_VB, npast - p) for p in range(npast + 1)]
    in_specs += [pl.BlockSpec(bias.shape, lambda i, j: (0, 0, 0)),
                 pl.BlockSpec(sk.shape, lambda i, j: (0, 0, 0))]
    return pl.pallas_call(
        kern,
        out_shape=jax.ShapeDtypeStruct((b, t, Q_B), F32),
        grid=(b, nc),
        in_specs=in_specs,
        out_specs=pl.BlockSpec((None, nq, Q_B), lambda i, j: (i, j, 0)),
        compiler_params=_cparams(("arbitrary", "arbitrary")),
        name="swa_prompt",
    )(*([proj3] * (1 + 2 * (npast + 1))), bias, sk)


def _swa_sample(proj3, k_past, v_past, table, sinks):
    b, t, _ = proj3.shape
    bias, sk = _swa_tables(table, sinks, t, WIN_PAST + t)
    kern = functools.partial(_swa_kernel, nq=t, piece_rows=(WIN_PAST, t), masked=False)
    in_specs = [pl.BlockSpec((None, t, Q_B), lambda i, j: (i, 0, COL_QB)),
                pl.BlockSpec((None, WIN_PAST, LANES), lambda i, j: (i, 0, 0)),
                pl.BlockSpec((None, t, LANES), lambda i, j: (i, 0, COL_KB)),
                pl.BlockSpec((None, WIN_PAST, LANES), lambda i, j: (i, 0, 0)),
                pl.BlockSpec((None, t, LANES), lambda i, j: (i, 0, COL_VB)),
                pl.BlockSpec(bias.shape, lambda i, j: (0, 0, 0)),
                pl.BlockSpec(sk.shape, lambda i, j: (0, 0, 0))]
    return pl.pallas_call(
        kern,
        out_shape=jax.ShapeDtypeStruct((b, t, Q_B), F32),
        grid=(b, 1),
        in_specs=in_specs,
        out_specs=pl.BlockSpec((None, t, Q_B), lambda i, j: (i, 0, 0)),
        compiler_params=_cparams(("arbitrary", "arbitrary")),
        name="swa_sample",
    )(proj3, k_past, proj3, v_past, proj3, bias, sk)


def _outproj_kernel(x_ref, oa_ref, ob_ref, wa_ref, wb_ref, g_ref, wq_ref, x1_ref, q_ref):
    x1 = (x_ref[...]
          + jnp.dot(oa_ref[...].astype(BF16), wa_ref[...], preferred_element_type=F32)
          + jnp.dot(ob_ref[...].astype(BF16), wb_ref[...], preferred_element_type=F32))
    x1_ref[...] = x1
    hc = _rms(x1, g_ref[...]).astype(BF16)
    q_ref[...] = jnp.dot(hc, wq_ref[...], preferred_element_type=F32)


def _outproj(x, o_a, o_b, w_out, norm_cross, w_mq, tm):
    m = x.shape[0]
    return pl.pallas_call(
        _outproj_kernel,
        out_shape=(jax.ShapeDtypeStruct((m, D_MODEL), F32), jax.ShapeDtypeStruct((m, MEM_W), F32)),
        grid=(m // tm,),
        in_specs=[pl.BlockSpec((tm, D_MODEL), lambda i: (i, 0)),
                  pl.BlockSpec((tm, V_A), lambda i: (i, 0)),
                  pl.BlockSpec((tm, Q_B), lambda i: (i, 0)),
                  pl.BlockSpec((V_A, D_MODEL), lambda i: (0, 0)),
                  pl.BlockSpec((Q_B, D_MODEL), lambda i: (1, 0)),
                  pl.BlockSpec((1, D_MODEL), lambda i: (0, 0)),
                  pl.BlockSpec((D_MODEL, MEM_W), lambda i: (0, 0))],
        out_specs=(pl.BlockSpec((tm, D_MODEL), lambda i: (i, 0)),
                   pl.BlockSpec((tm, MEM_W), lambda i: (i, 0))),
        compiler_params=_cparams(("arbitrary",)),
        name="outproj",
    )(x, o_a, o_b, w_out, w_out, norm_cross, w_mq)


def _cross_router_kernel(x1_ref, q_ref, mk_ref, mv_ref, wo_ref, g_ref, wr_ref, br_ref,
                         x2_ref, hp_ref, route_ref):
    q = q_ref[...]
    mk = mk_ref[...].astype(BF16)
    mv = mv_ref[...].astype(BF16)
    outs = []
    for h in range(MEM_HEADS):
        sl = slice(h * MEM_DH, (h + 1) * MEM_DH)
        s = lax.dot_general(q[:, sl].astype(BF16), mk[:, sl], (((1,), (1,)), ((), ())),
                            preferred_element_type=F32) * MEM_DH ** -0.5
        m = jnp.max(s, axis=-1, keepdims=True)
        p = jnp.exp(s - m)
        p = (p / jnp.sum(p, axis=-1, keepdims=True)).astype(BF16)
        outs.append(jnp.dot(p, mv[:, sl], preferred_element_type=F32))
    att = jnp.concatenate(outs, axis=1).astype(BF16)
    x2 = x1_ref[...] + jnp.dot(att, wo_ref[...], preferred_element_type=F32)
    x2_ref[...] = x2
    hf = _rms(x2, g_ref[...])
    lo = pltpu.bitcast(hf[:, 0:HALF].astype(BF16).astype(F32), jnp.uint32)
    hi = pltpu.bitcast(hf[:, HALF:D_MODEL].astype(BF16).astype(F32), jnp.uint32)
    packed = (lo >> 16) | (hi & jnp.uint32(0xFFFF0000))
    for j in range(XROW_TILES):
        hp_ref[:, j, :] = packed[:, j * LANES:(j + 1) * LANES]
    logits = jnp.dot(hf, wr_ref[...], precision=HIGHEST, preferred_element_type=F32) + br_ref[...]
    lane = lax.broadcasted_iota(jnp.int32, logits.shape, 1)
    lanef = lane.astype(F32)
    l = jnp.where(lane < N_EXPERTS, logits, -jnp.inf)
    vals, idxs = [], []
    for _ in range(TOP_K):
        m = jnp.max(l, axis=-1, keepdims=True)
        idx = jnp.min(jnp.where(l == m, lanef, float(LANES)), axis=-1, keepdims=True)
        vals.append(m)
        idxs.append(idx)
        l = jnp.where(lanef == idx, -jnp.inf, l)
    es = [jnp.exp(v - vals[0]) for v in vals]
    den = es[0] + es[1] + es[2] + es[3]
    route = jnp.zeros(logits.shape, F32)
    for k in range(TOP_K):
        route = jnp.where(lane == k, idxs[k], route)
        route = jnp.where(lane == TOP_K + k, es[k] / den, route)
    route_ref[...] = route


def _cross_router(x1, qm, mk, mv, w_mo, norm_ffn, w_router, b_router, tm):
    b, t, _ = x1.shape
    nt = t // tm
    return pl.pallas_call(
        _cross_router_kernel,
        out_shape=(jax.ShapeDtypeStruct((b, t, D_MODEL), F32),
                   jax.ShapeDtypeStruct((b, t, XROW_TILES, LANES), jnp.uint32),
                   jax.ShapeDtypeStruct((b, t, LANES), F32)),
        grid=(b, nt),
        in_specs=[pl.BlockSpec((None, tm, D_MODEL), lambda i, j: (i, j, 0)),
                  pl.BlockSpec((None, tm, MEM_W), lambda i, j: (i, j, 0)),
                  pl.BlockSpec((None, MEM_LEN, MEM_W), lambda i, j: (i, 0, 0)),
                  pl.BlockSpec((None, MEM_LEN, MEM_W), lambda i, j: (i, 0, 0)),
                  pl.BlockSpec((MEM_W, D_MODEL), lambda i, j: (0, 0)),
                  pl.BlockSpec((1, D_MODEL), lambda i, j: (0, 0)),
                  pl.BlockSpec((D_MODEL, LANES), lambda i, j: (0, 0)),
                  pl.BlockSpec((1, LANES), lambda i, j: (0, 0))],
        out_specs=(pl.BlockSpec((None, tm, D_MODEL), lambda i, j: (i, j, 0)),
                   pl.BlockSpec((None, tm, XROW_TILES, LANES), lambda i, j: (i, j, 0, 0)),
                   pl.BlockSpec((None, tm, LANES), lambda i, j: (i, j, 0))),
        compiler_params=_cparams(("arbitrary", "arbitrary")),
        name="cross_router",
    )(x1, qm, mk, mv, w_mo, norm_ffn, w_router, b_router)


def _dispatch_kernel(pos_ref, tail_ref, hp_hbm, xs_hbm, zero_ref, sem, *, tokens):
    i = pl.program_id(0)

    @pl.when(i == 0)
    def _():
        zero_ref[...] = jnp.zeros(zero_ref.shape, zero_ref.dtype)
        def tail_copy(e):
            return pltpu.make_async_copy(zero_ref, xs_hbm.at[pl.ds(tail_ref[e], MOE_SUB)], sem.at[0])

        for e in range(N_EXPERTS):
            pl.when(tail_ref[e] >= 0)(lambda e=e: tail_copy(e).start())
        for e in range(N_EXPERTS):
            pl.when(tail_ref[e] >= 0)(lambda e=e: tail_copy(e).wait())

    def issue(t, carry):
        src = hp_hbm.at[i * tokens + t]
        for k in range(TOP_K):
            pltpu.make_async_copy(src, xs_hbm.at[pos_ref[t * TOP_K + k]], sem.at[1]).start()
        return carry

    lax.fori_loop(0, tokens, issue, 0)

    def drain(t, carry):
        for k in range(TOP_K):
            pltpu.make_async_copy(hp_hbm.at[0], xs_hbm.at[0], sem.at[1]).wait()
        return carry

    lax.fori_loop(0, tokens, drain, 0)


def _dispatch(hp, pos_flat, tail_start, rows, tokens):
    t = hp.shape[0]
    kern = functools.partial(_dispatch_kernel, tokens=tokens)
    return pl.pallas_call(
        kern,
        out_shape=jax.ShapeDtypeStruct((rows, XROW_TILES, LANES), jnp.uint32),
        grid=(t // tokens,),
        in_specs=[pl.BlockSpec((tokens * TOP_K,), lambda i: (i,), memory_space=pltpu.SMEM),
                  pl.BlockSpec((N_EXPERTS,), lambda i: (0,), memory_space=pltpu.SMEM),
                  pl.BlockSpec(memory_space=pl.ANY)],
        out_specs=pl.BlockSpec(memory_space=pl.ANY),
        scratch_shapes=[pltpu.VMEM((MOE_SUB, XROW_TILES, LANES), jnp.uint32),
                        pltpu.SemaphoreType.DMA((2,))],
        compiler_params=pltpu.CompilerParams(dimension_semantics=("arbitrary",),
                                             vmem_limit_bytes=VMEM_LIMIT_BYTES, has_side_effects=True),
        name="moe_dispatch",
    )(pos_flat, tail_start, hp)


def _moe_kernel(ie_ref, is_ref, ns_ref, xs_hbm, wg_ref, bg_ref, wu_ref, bu_ref, wd_ref, bd_ref,
                ys_hbm, xu_ref, xb_ref, acc_ref, ystage_ref, in_sem, out_sem):
    w = pl.program_id(0)
    f = pl.program_id(1)
    nsub = ns_ref[w]
    start = is_ref[w]

    def sub_rows(s):
        return pl.ds(pl.multiple_of(s * MOE_SUB, MOE_SUB), MOE_SUB)

    def hbm_rows(s):
        return pl.ds(start + s * MOE_SUB, MOE_SUB)

    @pl.when(jnp.logical_and(f == 0, nsub > 0))
    def _():
        def load(s):
            return pltpu.make_async_copy(xs_hbm.at[hbm_rows(s)], xu_ref.at[sub_rows(s)], in_sem.at[s])

        def issue(s, carry):
            load(s).start()
            return carry

        lax.fori_loop(0, nsub, issue, 0)

        def unpack(s, carry):
            load(s).wait()
            for j in range(XROW_TILES):
                u = xu_ref[sub_rows(s), j, :]
                cols = slice(j * LANES, (j + 1) * LANES)
                xb_ref[sub_rows(s), cols] = pltpu.bitcast(u << 16, F32).astype(BF16)
                hcols = slice(HALF + j * LANES, HALF + (j + 1) * LANES)
                xb_ref[sub_rows(s), hcols] = pltpu.bitcast(u & jnp.uint32(0xFFFF0000), F32).astype(BF16)
            acc_ref[sub_rows(s), :] = jnp.broadcast_to(bd_ref[...], (MOE_SUB, D_MODEL))
            return carry

        lax.fori_loop(0, nsub, unpack, 0)

    @pl.when(nsub > 0)
    def _():
        wg = wg_ref[...].astype(BF16)
        wu = wu_ref[...].astype(BF16)
        wd = wd_ref[...].astype(BF16)
        bg = bg_ref[...]
        bu = bu_ref[...]

        def body(s, carry):
            x = xb_ref[sub_rows(s), :]
            gt = jnp.minimum(jnp.dot(x, wg, preferred_element_type=F32) + bg, SWIGLU_LIMIT)
            up = jnp.clip(jnp.dot(x, wu, preferred_element_type=F32) + bu, -SWIGLU_LIMIT, SWIGLU_LIMIT)
            a = gt * jax.nn.sigmoid(SWIGLU_ALPHA * gt) * (up + 1.0)
            acc_ref[sub_rows(s), :] += jnp.dot(a.astype(BF16), wd, preferred_element_type=F32)
            return carry

        lax.fori_loop(0, nsub, body, 0)

    @pl.when(jnp.logical_and(f == MOE_NF - 1, nsub > 0))
    def _():
        def write(s, carry):
            for j in range(YROW_TILES):
                ystage_ref[:, j, :] = acc_ref[sub_rows(s), j * LANES:(j + 1) * LANES]
            store = pltpu.make_async_copy(ystage_ref, ys_hbm.at[hbm_rows(s)], out_sem.at[0])
            store.start()
            store.wait()
            return carry

        lax.fori_loop(0, nsub, write, 0)


def _moe(xs, item_expert, item_start, item_nsub, w_gate, b_gate, w_up, b_up, w_down, b_down):
    rows = xs.shape[0]
    nitems = item_expert.shape[0]

    def fcol(w, f, ie, st, ns):
        return jnp.where(ns[w] > 0, f, MOE_NF - 1)

    grid_spec = pltpu.PrefetchScalarGridSpec(
        num_scalar_prefetch=3,
        grid=(nitems, MOE_NF),
        in_specs=[pl.BlockSpec(memory_space=pl.ANY),
                  pl.BlockSpec((None, D_MODEL, MOE_TF), lambda w, f, ie, st, ns: (ie[w], 0, fcol(w, f, ie, st, ns))),
                  pl.BlockSpec((None, 1, MOE_TF), lambda w, f, ie, st, ns: (ie[w], 0, fcol(w, f, ie, st, ns))),
                  pl.BlockSpec((None, D_MODEL, MOE_TF), lambda w, f, ie, st, ns: (ie[w], 0, fcol(w, f, ie, st, ns))),
                  pl.BlockSpec((None, 1, MOE_TF), lambda w, f, ie, st, ns: (ie[w], 0, fcol(w, f, ie, st, ns))),
                  pl.BlockSpec((None, MOE_TF, D_MODEL), lambda w, f, ie, st, ns: (ie[w], fcol(w, f, ie, st, ns), 0)),
                  pl.BlockSpec((None, 1, D_MODEL), lambda w, f, ie, st, ns: (ie[w], 0, 0))],
        out_specs=pl.BlockSpec(memory_space=pl.ANY),
        scratch_shapes=[pltpu.VMEM((MOE_ROWS, XROW_TILES, LANES), jnp.uint32),
                        pltpu.VMEM((MOE_ROWS, D_MODEL), BF16),
                        pltpu.VMEM((MOE_ROWS, D_MODEL), F32),
                        pltpu.VMEM((MOE_SUB, YROW_TILES, LANES), F32),
                        pltpu.SemaphoreType.DMA((MOE_ROWS // MOE_SUB,)),
                        pltpu.SemaphoreType.DMA((1,))])
    return pl.pallas_call(
        _moe_kernel,
        out_shape=jax.ShapeDtypeStruct((rows, YROW_TILES, LANES), F32),
        grid_spec=grid_spec,
        compiler_params=pltpu.CompilerParams(dimension_semantics=("arbitrary", "arbitrary"),
                                             vmem_limit_bytes=VMEM_LIMIT_BYTES, has_side_effects=True),
        name="moe_experts",
    )(item_expert, item_start, item_nsub, xs, w_gate, b_gate, w_up, b_up, w_down, b_down)


def _combine_kernel(pos_ref, x2_ref, route_ref, g_ref, ys_hbm, o_ref, buf_ref, sem, *, tokens):
    def issue(t, carry):
        for k in range(TOP_K):
            pltpu.make_async_copy(ys_hbm.at[pos_ref[t * TOP_K + k]], buf_ref.at[k, t], sem.at[0]).start()
        return carry

    lax.fori_loop(0, tokens, issue, 0)

    def drain(t, carry):
        for k in range(TOP_K):
            pltpu.make_async_copy(ys_hbm.at[0], buf_ref.at[k, 0], sem.at[0]).wait()
        return carry

    lax.fori_loop(0, tokens, drain, 0)

    route = route_ref[...]
    gates = [route[:, TOP_K + k:TOP_K + k + 1] for k in range(TOP_K)]
    cols = []
    for j in range(YROW_TILES):
        y = x2_ref[:, j * LANES:(j + 1) * LANES]
        for k in range(TOP_K):
            y = y + gates[k] * buf_ref[k, :, j, :]
        cols.append(y)
    o_ref[...] = _rms(jnp.concatenate(cols, axis=1), g_ref[...])


def _combine(x2, route, final_norm, ys, pos_flat, tokens):
    t = x2.shape[0]
    kern = functools.partial(_combine_kernel, tokens=tokens)
    return pl.pallas_call(
        kern,
        out_shape=jax.ShapeDtypeStruct((t, D_MODEL), F32),
        grid=(t // tokens,),
        in_specs=[pl.BlockSpec((tokens * TOP_K,), lambda i: (i,), memory_space=pltpu.SMEM),
                  pl.BlockSpec((tokens, D_MODEL), lambda i: (i, 0)),
                  pl.BlockSpec((tokens, LANES), lambda i: (i, 0)),
                  pl.BlockSpec((1, D_MODEL), lambda i: (0, 0)),
                  pl.BlockSpec(memory_space=pl.ANY)],
        out_specs=pl.BlockSpec((tokens, D_MODEL), lambda i: (i, 0)),
        scratch_shapes=[pltpu.VMEM((TOP_K, tokens, YROW_TILES, LANES), F32),
                        pltpu.SemaphoreType.DMA((1,))],
        compiler_params=_cparams(("arbitrary",)),
        name="moe_combine",
    )(pos_flat, x2, route, final_norm, ys)


def _routing_tables(top_i, nitems):
    t = top_i.shape[0]
    sel = jnp.sum((top_i[:, :, None] == jnp.arange(N_EXPERTS, dtype=jnp.int32)[None, None, :]).astype(jnp.int32),
                  axis=1)
    cnt = jnp.sum(sel, axis=0)
    rank = jnp.cumsum(sel, axis=0) - sel
    cnt_pad = ((cnt + MOE_SUB - 1) // MOE_SUB) * MOE_SUB
    off = jnp.cumsum(cnt_pad) - cnt_pad
    pos = jnp.take_along_axis(off[None, :] + rank, top_i, axis=1)
    tail_start = jnp.where(cnt > 0, off + cnt_pad - MOE_SUB, -1)
    items_per = (cnt + MOE_ROWS - 1) // MOE_ROWS
    item_end = jnp.cumsum(items_per)
    total = item_end[-1]
    widx = jnp.arange(nitems, dtype=jnp.int32)
    e_of = jnp.minimum(jnp.searchsorted(item_end, widx, side="right"), N_EXPERTS - 1).astype(jnp.int32)
    j_of = widx - (item_end - items_per)[e_of]
    valid = widx < total
    e_last = e_of[jnp.maximum(total - 1, 0)]
    item_expert = jnp.where(valid, e_of, e_last).astype(jnp.int32)
    item_start = jnp.where(valid, off[e_of] + j_of * MOE_ROWS, 0).astype(jnp.int32)
    rows_left = cnt_pad[e_of] - j_of * MOE_ROWS
    item_nsub = jnp.where(valid, jnp.minimum(rows_left, MOE_ROWS) // MOE_SUB, 0).astype(jnp.int32)
    return pos.reshape(t * TOP_K).astype(jnp.int32), tail_start.astype(jnp.int32), item_expert, item_start, item_nsub


def _trunk(x, conv_hist, s0, swa_fn, mk, mv, gdn_chunk, p):
    b, t, _ = x.shape
    m = b * t
    xf = x.reshape(m, D_MODEL)
    proj = _norm_matmul(xf, p["norm_mix"], p["w_in"], min(m, 1024), PROJ_TN)
    proj3 = proj.reshape(b, t, PROJ_COLS)
    hist8 = jnp.pad(conv_hist, ((0, 0), (SUBLANES - (CONV_W - 1), 0), (0, 0)))
    o_a, conv8, s_new = _gdn(proj3, hist8, s0, p["conv_w"], p["gate_par"], p["gdn_norm"], gdn_chunk)
    o_b = swa_fn(proj3)
    x1, qm = _outproj(xf, o_a.reshape(m, V_A), o_b.reshape(m, Q_B), p["w_out"], p["norm_cross"], p["w_mq"], 256)
    tm = min(t, 256)
    x2, hp, route = _cross_router(x1.reshape(b, t, D_MODEL), qm.reshape(b, t, MEM_W), mk, mv,
                                  p["w_mo"], p["norm_ffn"], p["w_router"], p["b_router"], tm)
    return proj3, conv8[:, SUBLANES - (CONV_W - 1):], s_new, x2.reshape(m, D_MODEL), \
        hp.reshape(m, XROW_TILES, LANES), route.reshape(m, LANES)


def kernel(x_prompt, x_sample, cache_conv, state_gdn, cache_swa_k, cache_swa_v, cache_mem_k, cache_mem_v, mem_prompt, norm_mix, w_in, conv_w, gdn_a_log, gdn_dt_bias, gdn_norm, swa_sinks, rel_bias_table, w_out, norm_cross, norm_mem, w_mq, w_mk, w_mv, w_mo, norm_ffn, w_router, b_router, w_gate, b_gate, w_up, b_up, w_down, b_down, final_norm):
    depth = norm_mix.shape[0]
    assert depth == 1, "kernel is written for the single-layer trunk"
    bp, sp, _ = x_prompt.shape
    bs, ss, _ = x_sample.shape
    l = 0
    w = w_in[l]
    n_ab = 2 * GDN_HEADS
    w_perm = jnp.concatenate(
        [w[:, :CONV_CH + V_A], w[:, CONV_CH + V_A + n_ab:], w[:, CONV_CH + V_A:CONV_CH + V_A + n_ab],
         jnp.zeros((D_MODEL, PROJ_COLS - w.shape[1]), w.dtype)], axis=1).astype(BF16)
    gate_par = jnp.zeros((SUBLANES, LANES), F32)
    gate_par = gate_par.at[0, :GDN_HEADS].set(gdn_a_log[l]).at[1, :GDN_HEADS].set(gdn_dt_bias[l])
    p = dict(
        norm_mix=norm_mix[l].reshape(1, D_MODEL), w_in=w_perm, conv_w=conv_w[l], gate_par=gate_par,
        gdn_norm=gdn_norm[l].reshape(1, GDN_D), w_out=w_out[l].astype(BF16),
        norm_cross=norm_cross[l].reshape(1, D_MODEL), w_mq=w_mq[l].astype(BF16), w_mo=w_mo[l].astype(BF16),
        norm_ffn=norm_ffn[l].reshape(1, D_MODEL),
        w_router=jnp.pad(w_router[l], ((0, 0), (0, LANES - N_EXPERTS))),
        b_router=jnp.pad(b_router[l], (0, LANES - N_EXPERTS)).reshape(1, LANES))

    w_mkv = jnp.concatenate([w_mk[l], w_mv[l]], axis=1).astype(BF16)
    mkv = _norm_matmul(mem_prompt.reshape(bp * MEM_LEN, D_MODEL), norm_mem[l].reshape(1, D_MODEL), w_mkv,
                       min(bp * MEM_LEN, 1024), PROJ_TN)
    mk_p = mkv[:, :MEM_W].reshape(bp, MEM_LEN, MEM_W)
    mv_p = mkv[:, MEM_W:].reshape(bp, MEM_LEN, MEM_W)

    zero_hist = jnp.zeros((bp, CONV_W - 1, CONV_CH), F32)
    zero_state = jnp.zeros((bp, GDN_HEADS, GDN_D, GDN_D), F32)
    swa_p = functools.partial(_swa_prompt, table=rel_bias_table, sinks=swa_sinks[l])
    proj_p, conv_p, st_p, x2_p, hp_p, route_p = _trunk(x_prompt, zero_hist, zero_state, swa_p, mk_p, mv_p, CHUNK, p)

    k_past = cache_swa_k[l].reshape(bs, WIN_PAST, KV_B)
    v_past = cache_swa_v[l].reshape(bs, WIN_PAST, KV_B)
    swa_s = functools.partial(_swa_sample, k_past=k_past, v_past=v_past, table=rel_bias_table, sinks=swa_sinks[l])
    proj_s, conv_s, st_s, x2_s, hp_s, route_s = _trunk(
        x_sample, cache_conv[l], state_gdn[l], swa_s,
        cache_mem_k[l].reshape(bs, MEM_LEN, MEM_W), cache_mem_v[l].reshape(bs, MEM_LEN, MEM_W), ss, p)

    mp, ms = bp * sp, bs * ss
    x2 = jnp.concatenate([x2_p, x2_s], axis=0)
    hp = jnp.concatenate([hp_p, hp_s], axis=0)
    route = jnp.concatenate([route_p, route_s], axis=0)
    ntok = mp + ms
    top_i = route[:, :TOP_K].astype(jnp.int32)
    rows = ntok * TOP_K + N_EXPERTS * MOE_SUB
    nitems = (ntok * TOP_K) // MOE_ROWS + N_EXPERTS
    pos_flat, tail_start, item_expert, item_start, item_nsub = _routing_tables(top_i, nitems)
    xs = _dispatch(hp, pos_flat, tail_start, rows, 512)
    ys = _moe(xs, item_expert, item_start, item_nsub,
              w_gate[l], b_gate[l].reshape(N_EXPERTS, 1, D_FF), w_up[l], b_up[l].reshape(N_EXPERTS, 1, D_FF),
              w_down[l], b_down[l].reshape(N_EXPERTS, 1, D_MODEL))
    y = _combine(x2, route, final_norm.reshape(1, D_MODEL), ys, pos_flat, 256)
    y_p = y[:mp].reshape(bp, sp, D_MODEL)
    y_s = y[mp:].reshape(bs, ss, D_MODEL)

    def kv_window(proj3, col, past=None):
        new = proj3[:, :, col * LANES:(col + 1) * LANES]
        full = new if past is None else jnp.concatenate([past, new], axis=1)
        win = full[:, -WIN_PAST:]
        return win.reshape(win.shape[0], WIN_PAST, SWA_KV_HEADS, SWA_DH)[None]

    return (y_p, y_s,
            conv_p[None], st_p[None], kv_window(proj_p, COL_KB), kv_window(proj_p, COL_VB),
            mk_p.reshape(bp, MEM_LEN, MEM_HEADS, MEM_DH)[None], mv_p.reshape(bp, MEM_LEN, MEM_HEADS, MEM_DH)[None],
            conv_s[None], st_s[None], kv_window(proj_s, COL_KB, k_past), kv_window(proj_s, COL_VB, v_past))
```

```python
import functools
import math

import numpy as np
import jax
import jax.numpy as jnp
from jax import lax
from jax.experimental import pallas as pl
from jax.experimental.pallas import tpu as pltpu

F32 = jnp.float32
BF16 = jnp.bfloat16
HIGHEST = lax.Precision.HIGHEST

D_MODEL = 2048
CHUNK = 64
GDN_HEADS = 8
GDN_D = 128
CONV_W = 4
SWA_HEADS = 16
SWA_KV_HEADS = 2
SWA_GROUP = SWA_HEADS // SWA_KV_HEADS
SWA_DH = 64
WIN_PAST = 128
REL_BUCKETS = 32
REL_MAX_DIST = 128
MEM_LEN = 256
MEM_HEADS = 4
MEM_DH = 128
N_EXPERTS = 32
TOP_K = 4
D_FF = D_MODEL
SWIGLU_LIMIT = 7.0
SWIGLU_ALPHA = 1.702
EPS = 1e-6

QK_A = GDN_HEADS * GDN_D
V_A = GDN_HEADS * GDN_D
CONV_CH = 2 * QK_A + V_A
Q_B = SWA_HEADS * SWA_DH
KV_B = SWA_KV_HEADS * SWA_DH
MEM_W = MEM_HEADS * MEM_DH

LANES = 128
SUBLANES = 8
VMEM_LIMIT_BYTES = 58 * 1024 * 1024

PROJ_COLS = 5632
PROJ_TN = 512
COL_Z = CONV_CH // V_A
COL_QB = (CONV_CH + V_A) // Q_B
COL_KB = (CONV_CH + V_A + Q_B) // LANES
COL_VB = COL_KB + 1
COL_AB = COL_KB + 2

MOE_SUB = 256
MOE_ROWS = 1280
MOE_STAGE = 128
MOE_TF = 512
MOE_NF = D_FF // MOE_TF
HALF = D_MODEL // 2
NORM_ROWS = 256
SWA_CHUNKS_PER_STEP = 4
CROSS_ROWS = 256
XROW_TILES = HALF // LANES
YROW_TILES = D_MODEL // LANES


def _cparams(sem):
    return pltpu.CompilerParams(dimension_semantics=sem, vmem_limit_bytes=VMEM_LIMIT_BYTES)


def _rms(x, gain):
    return x * lax.rsqrt(jnp.mean(x * x, axis=-1, keepdims=True) + EPS) * gain


def _norm_matmul_kernel(x_ref, g_ref, w_ref, o_ref, h_ref):
    @pl.when(pl.program_id(1) == 0)
    def _():
        def body(r, carry):
            rows = pl.ds(pl.multiple_of(r * NORM_ROWS, NORM_ROWS), NORM_ROWS)
            h_ref[rows, :] = _rms(x_ref[rows, :], g_ref[...]).astype(BF16)
            return carry

        lax.fori_loop(0, x_ref.shape[0] // NORM_ROWS, body, 0)

    o_ref[...] = jnp.dot(h_ref[...], w_ref[...], preferred_element_type=F32)


def _norm_matmul(x, gain, w, tm, tn):
    m, k = x.shape
    n = w.shape[1]
    return pl.pallas_call(
        _norm_matmul_kernel,
        out_shape=jax.ShapeDtypeStruct((m, n), F32),
        grid=(m // tm, n // tn),
        in_specs=[pl.BlockSpec((tm, k), lambda i, j: (i, 0)),
                  pl.BlockSpec((1, k), lambda i, j: (0, 0)),
                  pl.BlockSpec((k, tn), lambda i, j: (0, j))],
        out_specs=pl.BlockSpec((tm, tn), lambda i, j: (i, j)),
        scratch_shapes=[pltpu.VMEM((tm, k), BF16)],
        compiler_params=_cparams(("arbitrary", "arbitrary")),
        name="norm_matmul",
    )(x, gain, w)


def _tri_inverse(lows, c):
    row = lax.broadcasted_iota(jnp.int32, (c, c), 0)
    col = lax.broadcasted_iota(jnp.int32, (c, c), 1)
    eye = jnp.where(row == col, 1.0, 0.0).astype(F32)
    ps = [eye - low for low in lows]
    ms = list(lows)
    span = 1
    while 2 * span < c:
        mbs = [m.astype(BF16) for m in ms]
        ms = [jnp.dot(mb, mb, preferred_element_type=F32) for mb in mbs]
        ps = [p + jnp.dot(p.astype(BF16), m.astype(BF16), preferred_element_type=F32) for p, m in zip(ps, ms)]
        span *= 2
    return ps


def _gdn_kernel(qkv_ref, z_ref, ab_ref, hist_ref, s0_ref, convw_ref, gp_ref, norm_ref,
                o_ref, convnew_ref, s_ref, ext_ref, *, c):
    step = pl.program_id(1)

    @pl.when(step == 0)
    def _():
        ext_ref[0:SUBLANES, :] = hist_ref[...]
        s_ref[...] = s0_ref[...]

    ext_ref[SUBLANES:SUBLANES + c, :] = qkv_ref[...]
    base = SUBLANES - (CONV_W - 1)
    conv = ext_ref[base:base + c, :] * convw_ref[0:1, :]
    for j in range(1, CONV_W):
        conv = conv + ext_ref[base + j:base + j + c, :] * convw_ref[j:j + 1, :]
    tail = ext_ref[c:c + SUBLANES, :]
    convnew_ref[...] = tail
    ext_ref[0:SUBLANES, :] = tail

    act = conv * jax.nn.sigmoid(conv)

    ab = ab_ref[...]
    gp = gp_ref[...]
    g = -jnp.exp(gp[0:1, :]) * jax.nn.softplus(ab + gp[1:2, :])
    beta_all = jax.nn.sigmoid(ab)

    row = lax.broadcasted_iota(jnp.int32, (c, c), 0)
    col = lax.broadcasted_iota(jnp.int32, (c, c), 1)
    incl = row >= col
    strict = row > col
    tril = jnp.where(incl, 1.0, 0.0).astype(F32)
    gc_all = jnp.dot(tril, g, precision=HIGHEST, preferred_element_type=F32)
    gc_pad = jnp.concatenate([gc_all, jnp.zeros((LANES - c, LANES), F32)], axis=0) if c < LANES else gc_all
    gc_t = gc_pad.T

    scale = GDN_D ** -0.5
    heads = range(GDN_HEADS)
    s_old = [s_ref[h] for h in heads]
    z_all = z_ref[...]
    q, k, v = [], [], []
    for h in heads:
        lo = h * GDN_D
        qh = act[:, lo:lo + GDN_D]
        kh = act[:, QK_A + lo:QK_A + lo + GDN_D]
        q.append(qh * lax.rsqrt(jnp.sum(qh * qh, axis=-1, keepdims=True) + EPS) * scale)
        k.append(kh * lax.rsqrt(jnp.sum(kh * kh, axis=-1, keepdims=True) + EPS))
        v.append(act[:, 2 * QK_A + lo:2 * QK_A + lo + GDN_D])
    beta = [beta_all[:, GDN_HEADS + h:GDN_HEADS + h + 1] for h in heads]
    gc = [gc_all[:, h:h + 1] for h in heads]
    gc_last = [gc_all[c - 1:c, h:h + 1] for h in heads]
    decay = [jnp.exp(jnp.where(incl, gc[h] - gc_t[h:h + 1, 0:c], -jnp.inf)) for h in heads]
    egc = [jnp.exp(gc[h]) for h in heads]
    kb = [k[h] * beta[h] for h in heads]
    qk = [lax.dot_general(jnp.concatenate([q[h], kb[h]], axis=0).astype(BF16), k[h].astype(BF16),
                          (((1,), (1,)), ((), ())), preferred_element_type=F32) for h in heads]
    a_qk = [jnp.where(incl, qk[h][0:c] * decay[h], 0.0) for h in heads]
    tinv = _tri_inverse([jnp.where(strict, qk[h][c:2 * c] * decay[h], 0.0) for h in heads], c)
    uw = [jnp.dot(tinv[h].astype(BF16),
                  jnp.concatenate([v[h] * beta[h], kb[h] * egc[h]], axis=1).astype(BF16),
                  preferred_element_type=F32) for h in heads]
    ws_qs = [jnp.dot(jnp.concatenate([uw[h][:, GDN_D:2 * GDN_D], q[h] * egc[h]], axis=0).astype(BF16),
                     s_old[h].astype(BF16), preferred_element_type=F32) for h in heads]
    v_new = [(uw[h][:, 0:GDN_D] - ws_qs[h][0:c]).astype(BF16) for h in heads]
    o = [ws_qs[h][c:2 * c] + jnp.dot(a_qk[h].astype(BF16), v_new[h], preferred_element_type=F32)
         for h in heads]
    s_new = [s_old[h] * jnp.exp(gc_last[h]) + lax.dot_general(
        (k[h] * jnp.exp(gc_last[h] - gc[h])).astype(BF16), v_new[h], (((0,), (0,)), ((), ())),
        preferred_element_type=F32) for h in heads]
    gate = z_all * jax.nn.sigmoid(z_all)
    o_ref[...] = jnp.concatenate([_rms(o[h], norm_ref[...]) for h in heads], axis=1) * gate
    for h in heads:
        s_ref[h] = s_new[h]


def _gdn(proj3, hist8, s0, conv_w, gate_par, gdn_norm, c):
    b, t, _ = proj3.shape
    nchunks = t // c
    kern = functools.partial(_gdn_kernel, c=c)
    return pl.pallas_call(
        kern,
        out_shape=(jax.ShapeDtypeStruct((b, t, V_A), F32),
                   jax.ShapeDtypeStruct((b, SUBLANES, CONV_CH), F32),
                   jax.ShapeDtypeStruct((b, GDN_HEADS, GDN_D, GDN_D), F32)),
        grid=(b, nchunks),
        in_specs=[pl.BlockSpec((None, c, CONV_CH), lambda i, j: (i, j, 0)),
                  pl.BlockSpec((None, c, V_A), lambda i, j: (i, j, COL_Z)),
                  pl.BlockSpec((None, c, LANES), lambda i, j: (i, j, COL_AB)),
                  pl.BlockSpec((None, SUBLANES, CONV_CH), lambda i, j: (i, 0, 0)),
                  pl.BlockSpec((None, GDN_HEADS, GDN_D, GDN_D), lambda i, j: (i, 0, 0, 0)),
                  pl.BlockSpec((CONV_W, CONV_CH), lambda i, j: (0, 0)),
                  pl.BlockSpec((SUBLANES, LANES), lambda i, j: (0, 0)),
                  pl.BlockSpec((1, GDN_D), lambda i, j: (0, 0))],
        out_specs=(pl.BlockSpec((None, c, V_A), lambda i, j: (i, j, 0)),
                   pl.BlockSpec((None, SUBLANES, CONV_CH), lambda i, j: (i, 0, 0)),
                   pl.BlockSpec((None, GDN_HEADS, GDN_D, GDN_D), lambda i, j: (i, 0, 0, 0))),
        scratch_shapes=[pltpu.VMEM((SUBLANES + c, CONV_CH), F32)],
        compiler_params=_cparams(("arbitrary", "arbitrary")),
        name="gdn_mixer",
    )(proj3, proj3, proj3, hist8, s0, conv_w, gate_par, gdn_norm)


def _swa_kernel(*refs, nq, nsub, piece_rows, masked):
    npieces = len(piece_rows)
    q_ref = refs[0]
    k_refs = refs[1:1 + npieces]
    v_refs = refs[1 + npieces:1 + 2 * npieces]
    bias_ref, sink_ref, o_ref = refs[1 + 2 * npieces:]
    nk = WIN_PAST + nq
    kcat = jnp.concatenate([r[...] for r in k_refs], axis=0).astype(BF16)
    vcat = jnp.concatenate([r[...] for r in v_refs], axis=0).astype(BF16)
    q = q_ref[...]
    kidx = lax.broadcasted_iota(jnp.int32, (SWA_GROUP * nq, nk), 1)
    out_rows = []
    for i in range(nsub):
        heads = []
        for kv in range(SWA_KV_HEADS):
            qs = jnp.concatenate(
                [q[i * nq:(i + 1) * nq, (kv * SWA_GROUP + g) * SWA_DH:(kv * SWA_GROUP + g + 1) * SWA_DH]
                 for g in range(SWA_GROUP)], axis=0).astype(BF16)
            kh = kcat[i * nq:i * nq + nk, kv * SWA_DH:(kv + 1) * SWA_DH]
            vh = vcat[i * nq:i * nq + nk, kv * SWA_DH:(kv + 1) * SWA_DH]
            s = lax.dot_general(qs, kh, (((1,), (1,)), ((), ())), preferred_element_type=F32)
            s = s * SWA_DH ** -0.5 + bias_ref[kv]
            if masked:
                first = (pl.program_id(1) * nsub + i) * nq - WIN_PAST
                s = jnp.where(kidx + first >= 0, s, -jnp.inf)
            sk = sink_ref[kv]
            m = jnp.maximum(jnp.max(s, axis=-1, keepdims=True), sk)
            p = jnp.exp(s - m)
            den = jnp.sum(p, axis=-1, keepdims=True) + jnp.exp(sk - m)
            p = (p / den).astype(BF16)
            o = jnp.dot(p, vh, preferred_element_type=F32)
            heads += [o[g * nq:(g + 1) * nq, :] for g in range(SWA_GROUP)]
        out_rows.append(jnp.concatenate(heads, axis=1))
    o_ref[...] = out_rows[0] if nsub == 1 else jnp.concatenate(out_rows, axis=0)


def _rel_bucket(nq, nk):
    rel = jnp.arange(nk)[None, :] - WIN_PAST - jnp.arange(nq)[:, None]
    nb = REL_BUCKETS // 2
    max_exact = nb // 2
    n = jnp.abs(rel)
    large = max_exact + (jnp.log(jnp.maximum(n, 1).astype(F32) / max_exact)
                         / math.log(REL_MAX_DIST / max_exact) * (nb - max_exact)).astype(jnp.int32)
    large = jnp.minimum(large, nb - 1)
    return jnp.where(rel > 0, nb, 0) + jnp.where(n < max_exact, n, large)


def _swa_tables(table, sinks, nq, nk):
    bias = table[_rel_bucket(nq, nk)]
    bias = jnp.transpose(bias, (2, 0, 1)).astype(F32)
    bias = bias.reshape(SWA_KV_HEADS, SWA_GROUP * nq, nk)
    sk = jnp.broadcast_to(sinks.astype(F32).reshape(SWA_KV_HEADS, SWA_GROUP, 1, 1),
                          (SWA_KV_HEADS, SWA_GROUP, nq, 1)).reshape(SWA_KV_HEADS, SWA_GROUP * nq, 1)
    return bias, sk


def _swa_prompt(proj3, table, sinks):
    b, t, _ = proj3.shape
    nq = CHUNK
    nsub = SWA_CHUNKS_PER_STEP
    rows = nsub * nq
    assert rows % WIN_PAST == 0 and t % rows == 0
    past_per_step = rows // WIN_PAST
    bias, sk = _swa_tables(table, sinks, nq, WIN_PAST + nq)
    kern = functools.partial(_swa_kernel, nq=nq, nsub=nsub, piece_rows=(WIN_PAST, rows), masked=True)

    def past_spec(colblk):
        return pl.BlockSpec((None, WIN_PAST, LANES),
                            lambda i, j: (i, jnp.maximum(j * past_per_step - 1, 0), colblk))

    def cur_spec(colblk):
        return pl.BlockSpec((None, rows, LANES), lambda i, j: (i, j, colblk))

    in_specs = [pl.BlockSpec((None, rows, Q_B), lambda i, j: (i, j, COL_QB)),
                past_spec(COL_KB), cur_spec(COL_KB), past_spec(COL_VB), cur_spec(COL_VB),
                pl.BlockSpec(bias.shape, lambda i, j: (0, 0, 0)),
                pl.BlockSpec(sk.shape, lambda i, j: (0, 0, 0))]
    return pl.pallas_call(
        kern,
        out_shape=jax.ShapeDtypeStruct((b, t, Q_B), F32),
        grid=(b, t // rows),
        in_specs=in_specs,
        out_specs=pl.BlockSpec((None, rows, Q_B), lambda i, j: (i, j, 0)),
        compiler_params=_cparams(("arbitrary", "arbitrary")),
        name="swa_prompt",
    )(proj3, proj3, proj3, proj3, proj3, bias, sk)


def _swa_sample(proj3, k_past, v_past, table, sinks):
    b, t, _ = proj3.shape
    bias, sk = _swa_tables(table, sinks, t, WIN_PAST + t)
    kern = functools.partial(_swa_kernel, nq=t, nsub=1, piece_rows=(WIN_PAST, t), masked=False)
    in_specs = [pl.BlockSpec((None, t, Q_B), lambda i, j: (i, 0, COL_QB)),
                pl.BlockSpec((None, WIN_PAST, LANES), lambda i, j: (i, 0, 0)),
                pl.BlockSpec((None, t, LANES), lambda i, j: (i, 0, COL_KB)),
                pl.BlockSpec((None, WIN_PAST, LANES), lambda i, j: (i, 0, 0)),
                pl.BlockSpec((None, t, LANES), lambda i, j: (i, 0, COL_VB)),
                pl.BlockSpec(bias.shape, lambda i, j: (0, 0, 0)),
                pl.BlockSpec(sk.shape, lambda i, j: (0, 0, 0))]
    return pl.pallas_call(
        kern,
        out_shape=jax.ShapeDtypeStruct((b, t, Q_B), F32),
        grid=(b, 1),
        in_specs=in_specs,
        out_specs=pl.BlockSpec((None, t, Q_B), lambda i, j: (i, 0, 0)),
        compiler_params=_cparams(("arbitrary", "arbitrary")),
        name="swa_sample",
    )(proj3, k_past, proj3, v_past, proj3, bias, sk)


def _outproj_kernel(x_ref, oa_ref, ob_ref, wa_ref, wb_ref, g_ref, wq_ref, x1_ref, q_ref):
    x1 = (x_ref[...]
          + jnp.dot(oa_ref[...].astype(BF16), wa_ref[...], preferred_element_type=F32)
          + jnp.dot(ob_ref[...].astype(BF16), wb_ref[...], preferred_element_type=F32))
    x1_ref[...] = x1
    hc = _rms(x1, g_ref[...]).astype(BF16)
    q_ref[...] = jnp.dot(hc, wq_ref[...], preferred_element_type=F32)


def _outproj(x, o_a, o_b, w_out, norm_cross, w_mq, tm):
    m = x.shape[0]
    return pl.pallas_call(
        _outproj_kernel,
        out_shape=(jax.ShapeDtypeStruct((m, D_MODEL), F32), jax.ShapeDtypeStruct((m, MEM_W), F32)),
        grid=(m // tm,),
        in_specs=[pl.BlockSpec((tm, D_MODEL), lambda i: (i, 0)),
                  pl.BlockSpec((tm, V_A), lambda i: (i, 0)),
                  pl.BlockSpec((tm, Q_B), lambda i: (i, 0)),
                  pl.BlockSpec((V_A, D_MODEL), lambda i: (0, 0)),
                  pl.BlockSpec((Q_B, D_MODEL), lambda i: (1, 0)),
                  pl.BlockSpec((1, D_MODEL), lambda i: (0, 0)),
                  pl.BlockSpec((D_MODEL, MEM_W), lambda i: (0, 0))],
        out_specs=(pl.BlockSpec((tm, D_MODEL), lambda i: (i, 0)),
                   pl.BlockSpec((tm, MEM_W), lambda i: (i, 0))),
        compiler_params=_cparams(("arbitrary",)),
        name="outproj",
    )(x, o_a, o_b, w_out, w_out, norm_cross, w_mq)


def _cross_router_kernel(x1_ref, q_ref, mk_ref, mv_ref, wo_ref, g_ref, wr_ref, br_ref,
                         x2_ref, hp_ref, route_ref):
    nb, tm = x1_ref.shape[0], x1_ref.shape[1]
    rows = []
    for bi in range(nb):
        q = q_ref[bi]
        mk = mk_ref[bi].astype(BF16)
        mv = mv_ref[bi].astype(BF16)
        outs = []
        for h in range(MEM_HEADS):
            sl = slice(h * MEM_DH, (h + 1) * MEM_DH)
            s = lax.dot_general(q[:, sl].astype(BF16), mk[:, sl], (((1,), (1,)), ((), ())),
                                preferred_element_type=F32) * MEM_DH ** -0.5
            m = jnp.max(s, axis=-1, keepdims=True)
            p = jnp.exp(s - m)
            p = (p / jnp.sum(p, axis=-1, keepdims=True)).astype(BF16)
            outs.append(jnp.dot(p, mv[:, sl], preferred_element_type=F32))
        rows.append(jnp.concatenate(outs, axis=1))
    att = (rows[0] if nb == 1 else jnp.concatenate(rows, axis=0)).astype(BF16)
    x2 = x1_ref[...].reshape(nb * tm, D_MODEL) + jnp.dot(att, wo_ref[...], preferred_element_type=F32)
    x2_ref[...] = x2.reshape(nb, tm, D_MODEL)
    hf = _rms(x2, g_ref[...])
    lo = pltpu.bitcast(hf[:, 0:HALF].astype(BF16).astype(F32), jnp.uint32)
    hi = pltpu.bitcast(hf[:, HALF:D_MODEL].astype(BF16).astype(F32), jnp.uint32)
    packed = (lo >> 16) | (hi & jnp.uint32(0xFFFF0000))
    for bi in range(nb):
        for j in range(XROW_TILES):
            hp_ref[bi, pl.ds(j, tm, stride=XROW_TILES), :] = packed[bi * tm:(bi + 1) * tm, j * LANES:(j + 1) * LANES]
    logits = jnp.dot(hf, wr_ref[...], precision=HIGHEST, preferred_element_type=F32) + br_ref[...]
    lane = lax.broadcasted_iota(jnp.int32, logits.shape, 1)
    lanef = lane.astype(F32)
    l = jnp.where(lane < N_EXPERTS, logits, -jnp.inf)
    vals, idxs = [], []
    for _ in range(TOP_K):
        m = jnp.max(l, axis=-1, keepdims=True)
        idx = jnp.min(jnp.where(l == m, lanef, float(LANES)), axis=-1, keepdims=True)
        vals.append(m)
        idxs.append(idx)
        l = jnp.where(lanef == idx, -jnp.inf, l)
    es = [jnp.exp(v - vals[0]) for v in vals]
    den = es[0] + es[1] + es[2] + es[3]
    route = jnp.zeros(logits.shape, F32)
    for k in range(TOP_K):
        route = jnp.where(lane == k, idxs[k], route)
        route = jnp.where(lane == TOP_K + k, es[k] / den, route)
    route_ref[...] = route.reshape(nb, tm, LANES)


def _cross_router(x1, qm, mk, mv, w_mo, norm_ffn, w_router, b_router, nb, tm):
    b, t, _ = x1.shape
    nt = t // tm
    return pl.pallas_call(
        _cross_router_kernel,
        out_shape=(jax.ShapeDtypeStruct((b, t, D_MODEL), F32),
                   jax.ShapeDtypeStruct((b, t * XROW_TILES, LANES), jnp.uint32),
                   jax.ShapeDtypeStruct((b, t, LANES), F32)),
        grid=(b // nb, nt),
        in_specs=[pl.BlockSpec((nb, tm, D_MODEL), lambda i, j: (i, j, 0)),
                  pl.BlockSpec((nb, tm, MEM_W), lambda i, j: (i, j, 0)),
                  pl.BlockSpec((nb, MEM_LEN, MEM_W), lambda i, j: (i, 0, 0)),
                  pl.BlockSpec((nb, MEM_LEN, MEM_W), lambda i, j: (i, 0, 0)),
                  pl.BlockSpec((MEM_W, D_MODEL), lambda i, j: (0, 0)),
                  pl.BlockSpec((1, D_MODEL), lambda i, j: (0, 0)),
                  pl.BlockSpec((D_MODEL, LANES), lambda i, j: (0, 0)),
                  pl.BlockSpec((1, LANES), lambda i, j: (0, 0))],
        out_specs=(pl.BlockSpec((nb, tm, D_MODEL), lambda i, j: (i, j, 0)),
                   pl.BlockSpec((nb, tm * XROW_TILES, LANES), lambda i, j: (i, j, 0)),
                   pl.BlockSpec((nb, tm, LANES), lambda i, j: (i, j, 0))),
        compiler_params=_cparams(("arbitrary", "arbitrary")),
        name="cross_router",
    )(x1, qm, mk, mv, w_mo, norm_ffn, w_router, b_router)


def _row_tile(idx, tiles):
    return pl.ds(pl.multiple_of(idx * tiles, tiles), tiles)


def _dispatch_kernel(pos_ref, tail_ref, hp_ref, xs_hbm, zero_ref, sem, *, tokens):
    @pl.when(pl.program_id(0) == 0)
    def _():
        zero_ref[...] = jnp.zeros(zero_ref.shape, zero_ref.dtype)

        def tail_copy(e):
            dst = xs_hbm.at[pl.ds(pl.multiple_of(tail_ref[e] * XROW_TILES, XROW_TILES), MOE_SUB * XROW_TILES)]
            return pltpu.make_async_copy(zero_ref, dst, sem.at[0])

        for e in range(N_EXPERTS):
            pl.when(tail_ref[e] >= 0)(lambda e=e: tail_copy(e).start())
        for e in range(N_EXPERTS):
            pl.when(tail_ref[e] >= 0)(lambda e=e: tail_copy(e).wait())

    def issue(t, carry):
        src = hp_ref.at[_row_tile(t, XROW_TILES)]
        for k in range(TOP_K):
            dst = xs_hbm.at[_row_tile(pos_ref[t * TOP_K + k], XROW_TILES)]
            pltpu.make_async_copy(src, dst, sem.at[1]).start()
        return carry

    lax.fori_loop(0, tokens, issue, 0, unroll=4)

    def drain(t, carry):
        for k in range(TOP_K):
            pltpu.make_async_copy(hp_ref.at[_row_tile(0, XROW_TILES)], xs_hbm.at[_row_tile(0, XROW_TILES)],
                                  sem.at[1]).wait()
        return carry

    lax.fori_loop(0, tokens, drain, 0)


def _dispatch(hp, pos_flat, tail_start, rows, tokens):
    t = hp.shape[0] // XROW_TILES
    kern = functools.partial(_dispatch_kernel, tokens=tokens)
    return pl.pallas_call(
        kern,
        out_shape=jax.ShapeDtypeStruct((rows * XROW_TILES, LANES), jnp.uint32),
        grid=(t // tokens,),
        in_specs=[pl.BlockSpec((tokens * TOP_K,), lambda i: (i,), memory_space=pltpu.SMEM),
                  pl.BlockSpec((N_EXPERTS,), lambda i: (0,), memory_space=pltpu.SMEM),
                  pl.BlockSpec((tokens * XROW_TILES, LANES), lambda i: (i, 0))],
        out_specs=pl.BlockSpec(memory_space=pl.ANY),
        scratch_shapes=[pltpu.VMEM((MOE_SUB * XROW_TILES, LANES), jnp.uint32),
                        pltpu.SemaphoreType.DMA((2,))],
        compiler_params=pltpu.CompilerParams(dimension_semantics=("arbitrary",),
                                             vmem_limit_bytes=VMEM_LIMIT_BYTES, has_side_effects=True),
        name="moe_dispatch",
    )(pos_flat, tail_start, hp)


def _moe_kernel(ie_ref, is_ref, ns_ref, xs_hbm, wg_ref, bg_ref, wu_ref, bu_ref, wd_ref, bd_ref,
                ys_hbm, xu_ref, xb_ref, acc_ref, ystage_ref, in_sem, out_sem):
    w = pl.program_id(0)
    f = pl.program_id(1)
    nsub = ns_ref[w]
    start = is_ref[w]

    def sub_rows(s):
        return pl.ds(pl.multiple_of(s * MOE_SUB, MOE_SUB), MOE_SUB)

    nstage = nsub * (MOE_SUB // MOE_STAGE)

    def stage_rows(s):
        return pl.ds(pl.multiple_of(s * MOE_STAGE, MOE_STAGE), MOE_STAGE)

    def hbm_rows(s, tiles):
        return pl.ds(pl.multiple_of((start + s * MOE_STAGE) * tiles, MOE_STAGE * tiles), MOE_STAGE * tiles)

    @pl.when(jnp.logical_and(f == 0, nsub > 0))
    def _():
        def load(s):
            return pltpu.make_async_copy(xs_hbm.at[hbm_rows(s, XROW_TILES)], xu_ref.at[s % 2], in_sem.at[s % 2])

        load(0).start()

        def unpack(s, carry):
            load(s).wait()
            pl.when(s + 1 < nstage)(lambda: load(s + 1).start())
            for j in range(XROW_TILES):
                u = xu_ref[s % 2, pl.ds(j, MOE_STAGE, stride=XROW_TILES), :]
                cols = slice(j * LANES, (j + 1) * LANES)
                xb_ref[stage_rows(s), cols] = pltpu.bitcast(u << 16, F32).astype(BF16)
                hcols = slice(HALF + j * LANES, HALF + (j + 1) * LANES)
                xb_ref[stage_rows(s), hcols] = pltpu.bitcast(u & jnp.uint32(0xFFFF0000), F32).astype(BF16)
            acc_ref[stage_rows(s), :] = jnp.broadcast_to(bd_ref[...], (MOE_STAGE, D_MODEL))
            return carry

        lax.fori_loop(0, nstage, unpack, 0)

    @pl.when(nsub > 0)
    def _():
        wg = wg_ref[...].astype(BF16)
        wu = wu_ref[...].astype(BF16)
        wd = wd_ref[...].astype(BF16)
        bg = bg_ref[...]
        bu = bu_ref[...]

        def body(s, carry):
            x = xb_ref[sub_rows(s), :]
            gt = jnp.minimum(jnp.dot(x, wg, preferred_element_type=F32) + bg, SWIGLU_LIMIT)
            up = jnp.clip(jnp.dot(x, wu, preferred_element_type=F32) + bu, -SWIGLU_LIMIT, SWIGLU_LIMIT)
            a = gt * jax.nn.sigmoid(SWIGLU_ALPHA * gt) * (up + 1.0)
            acc_ref[sub_rows(s), :] += jnp.dot(a.astype(BF16), wd, preferred_element_type=F32)
            return carry

        lax.fori_loop(0, nsub, body, 0)

    @pl.when(jnp.logical_and(f == MOE_NF - 1, nsub > 0))
    def _():
        def store(s):
            return pltpu.make_async_copy(ystage_ref.at[s % 2], ys_hbm.at[hbm_rows(s, YROW_TILES)], out_sem.at[s % 2])

        def write(s, carry):
            pl.when(s >= 2)(lambda: store(s - 2).wait())
            for j in range(YROW_TILES):
                ystage_ref[s % 2, pl.ds(j, MOE_STAGE, stride=YROW_TILES), :] = \
                    acc_ref[stage_rows(s), j * LANES:(j + 1) * LANES]
            store(s).start()
            return carry

        lax.fori_loop(0, nstage, write, 0)
        store(nstage - 2).wait()
        store(nstage - 1).wait()


def _moe(xs, item_expert, item_start, item_nsub, w_gate, b_gate, w_up, b_up, w_down, b_down):
    rows = xs.shape[0] // XROW_TILES
    nitems = item_expert.shape[0]

    def fcol(w, f, ie, st, ns):
        return jnp.where(ns[w] > 0, f, MOE_NF - 1)

    grid_spec = pltpu.PrefetchScalarGridSpec(
        num_scalar_prefetch=3,
        grid=(nitems, MOE_NF),
        in_specs=[pl.BlockSpec(memory_space=pl.ANY),
                  pl.BlockSpec((None, D_MODEL, MOE_TF), lambda w, f, ie, st, ns: (ie[w], 0, fcol(w, f, ie, st, ns))),
                  pl.BlockSpec((None, 1, MOE_TF), lambda w, f, ie, st, ns: (ie[w], 0, fcol(w, f, ie, st, ns))),
                  pl.BlockSpec((None, D_MODEL, MOE_TF), lambda w, f, ie, st, ns: (ie[w], 0, fcol(w, f, ie, st, ns))),
                  pl.BlockSpec((None, 1, MOE_TF), lambda w, f, ie, st, ns: (ie[w], 0, fcol(w, f, ie, st, ns))),
                  pl.BlockSpec((None, MOE_TF, D_MODEL), lambda w, f, ie, st, ns: (ie[w], fcol(w, f, ie, st, ns), 0)),
                  pl.BlockSpec((None, 1, D_MODEL), lambda w, f, ie, st, ns: (ie[w], 0, 0))],
        out_specs=pl.BlockSpec(memory_space=pl.ANY),
        scratch_shapes=[pltpu.VMEM((2, MOE_STAGE * XROW_TILES, LANES), jnp.uint32),
                        pltpu.VMEM((MOE_ROWS, D_MODEL), BF16),
                        pltpu.VMEM((MOE_ROWS, D_MODEL), F32),
                        pltpu.VMEM((2, MOE_STAGE * YROW_TILES, LANES), F32),
                        pltpu.SemaphoreType.DMA((2,)),
                        pltpu.SemaphoreType.DMA((2,))])
    return pl.pallas_call(
        _moe_kernel,
        out_shape=jax.ShapeDtypeStruct((rows * YROW_TILES, LANES), F32),
        grid_spec=grid_spec,
        compiler_params=pltpu.CompilerParams(dimension_semantics=("arbitrary", "arbitrary"),
                                             vmem_limit_bytes=VMEM_LIMIT_BYTES, has_side_effects=True),
        name="moe_experts",
    )(item_expert, item_start, item_nsub, xs, w_gate, b_gate, w_up, b_up, w_down, b_down)


def _combine_kernel(pos_ref, nxt_ref, x2_ref, route_ref, g_ref, ys_hbm, o_ref, buf_ref, sem, *, tokens):
    i = pl.program_id(0)
    slot = i % 2

    def gather(p_ref, dst_slot):
        def issue(t, carry):
            for k in range(TOP_K):
                pltpu.make_async_copy(ys_hbm.at[_row_tile(p_ref[t * TOP_K + k], YROW_TILES)],
                                      buf_ref.at[dst_slot, k, _row_tile(t, YROW_TILES)], sem.at[dst_slot]).start()
            return carry

        lax.fori_loop(0, tokens, issue, 0, unroll=4)

    pl.when(i == 0)(lambda: gather(pos_ref, 0))
    pl.when(i + 1 < pl.num_programs(0))(lambda: gather(nxt_ref, 1 - slot))

    def drain(t, carry):
        for k in range(TOP_K):
            pltpu.make_async_copy(ys_hbm.at[_row_tile(0, YROW_TILES)],
                                  buf_ref.at[slot, k, _row_tile(0, YROW_TILES)], sem.at[slot]).wait()
        return carry

    lax.fori_loop(0, tokens, drain, 0)

    route = route_ref[...]
    gates = [route[:, TOP_K + k:TOP_K + k + 1] for k in range(TOP_K)]
    cols = []
    for j in range(YROW_TILES):
        y = x2_ref[:, j * LANES:(j + 1) * LANES]
        for k in range(TOP_K):
            y = y + gates[k] * buf_ref[slot, k, pl.ds(j, tokens, stride=YROW_TILES), :]
        cols.append(y)
    o_ref[...] = _rms(jnp.concatenate(cols, axis=1), g_ref[...])


def _combine(x2, route, final_norm, ys, pos_flat, tokens):
    t = x2.shape[0]
    nsteps = t // tokens
    kern = functools.partial(_combine_kernel, tokens=tokens)
    return pl.pallas_call(
        kern,
        out_shape=jax.ShapeDtypeStruct((t, D_MODEL), F32),
        grid=(nsteps,),
        in_specs=[pl.BlockSpec((tokens * TOP_K,), lambda i: (i,), memory_space=pltpu.SMEM),
                  pl.BlockSpec((tokens * TOP_K,), lambda i: (jnp.minimum(i + 1, nsteps - 1),),
                               memory_space=pltpu.SMEM),
                  pl.BlockSpec((tokens, D_MODEL), lambda i: (i, 0)),
                  pl.BlockSpec((tokens, LANES), lambda i: (i, 0)),
                  pl.BlockSpec((1, D_MODEL), lambda i: (0, 0)),
                  pl.BlockSpec(memory_space=pl.ANY)],
        out_specs=pl.BlockSpec((tokens, D_MODEL), lambda i: (i, 0)),
        scratch_shapes=[pltpu.VMEM((2, TOP_K, tokens * YROW_TILES, LANES), F32),
                        pltpu.SemaphoreType.DMA((2,))],
        compiler_params=_cparams(("arbitrary",)),
        name="moe_combine",
    )(pos_flat, pos_flat, x2, route, final_norm, ys)


def _routing_tables(top_i, nitems):
    t = top_i.shape[0]
    sel = jnp.sum((top_i[:, :, None] == jnp.arange(N_EXPERTS, dtype=jnp.int32)[None, None, :]).astype(jnp.int32),
                  axis=1)
    cnt = jnp.sum(sel, axis=0)
    rank = jnp.cumsum(sel, axis=0) - sel
    cnt_pad = ((cnt + MOE_SUB - 1) // MOE_SUB) * MOE_SUB
    off = jnp.cumsum(cnt_pad) - cnt_pad
    pos = jnp.take_along_axis(off[None, :] + rank, top_i, axis=1)
    tail_start = jnp.where(cnt > 0, off + cnt_pad - MOE_SUB, -1)
    items_per = (cnt + MOE_ROWS - 1) // MOE_ROWS
    item_end = jnp.cumsum(items_per)
    total = item_end[-1]
    widx = jnp.arange(nitems, dtype=jnp.int32)
    e_of = jnp.minimum(jnp.searchsorted(item_end, widx, side="right"), N_EXPERTS - 1).astype(jnp.int32)
    j_of = widx - (item_end - items_per)[e_of]
    valid = widx < total
    e_last = e_of[jnp.maximum(total - 1, 0)]
    item_expert = jnp.where(valid, e_of, e_last).astype(jnp.int32)
    item_start = jnp.where(valid, off[e_of] + j_of * MOE_ROWS, 0).astype(jnp.int32)
    rows_left = cnt_pad[e_of] - j_of * MOE_ROWS
    item_nsub = jnp.where(valid, jnp.minimum(rows_left, MOE_ROWS) // MOE_SUB, 0).astype(jnp.int32)
    return pos.reshape(t * TOP_K).astype(jnp.int32), tail_start.astype(jnp.int32), item_expert, item_start, item_nsub


def _trunk(x, conv_hist, s0, swa_fn, mk, mv, gdn_chunk, p):
    b, t, _ = x.shape
    m = b * t
    xf = x.reshape(m, D_MODEL)
    proj = _norm_matmul(xf, p["norm_mix"], p["w_in"], min(m, 1024), PROJ_TN)
    proj3 = proj.reshape(b, t, PROJ_COLS)
    hist8 = jnp.pad(conv_hist, ((0, 0), (SUBLANES - (CONV_W - 1), 0), (0, 0)))
    o_a, conv8, s_new = _gdn(proj3, hist8, s0, p["conv_w"], p["gate_par"], p["gdn_norm"], gdn_chunk)
    o_b = swa_fn(proj3)
    x1, qm = _outproj(xf, o_a.reshape(m, V_A), o_b.reshape(m, Q_B), p["w_out"], p["norm_cross"], p["w_mq"], 256)
    tm = min(t, CROSS_ROWS)
    x2, hp, route = _cross_router(x1.reshape(b, t, D_MODEL), qm.reshape(b, t, MEM_W), mk, mv,
                                  p["w_mo"], p["norm_ffn"], p["w_router"], p["b_router"], CROSS_ROWS // tm, tm)
    return proj3, conv8[:, SUBLANES - (CONV_W - 1):], s_new, x2.reshape(m, D_MODEL), \
        hp.reshape(m * XROW_TILES, LANES), route.reshape(m, LANES)


def kernel(x_prompt, x_sample, cache_conv, state_gdn, cache_swa_k, cache_swa_v, cache_mem_k, cache_mem_v, mem_prompt, norm_mix, w_in, conv_w, gdn_a_log, gdn_dt_bias, gdn_norm, swa_sinks, rel_bias_table, w_out, norm_cross, norm_mem, w_mq, w_mk, w_mv, w_mo, norm_ffn, w_router, b_router, w_gate, b_gate, w_up, b_up, w_down, b_down, final_norm):
    depth = norm_mix.shape[0]
    assert depth == 1, "kernel is written for the single-layer trunk"
    bp, sp, _ = x_prompt.shape
    bs, ss, _ = x_sample.shape
    l = 0
    w = w_in[l]
    n_ab = 2 * GDN_HEADS
    w_perm = jnp.concatenate(
        [w[:, :CONV_CH + V_A], w[:, CONV_CH + V_A + n_ab:], w[:, CONV_CH + V_A:CONV_CH + V_A + n_ab],
         jnp.zeros((D_MODEL, PROJ_COLS - w.shape[1]), w.dtype)], axis=1).astype(BF16)
    gate_par = jnp.zeros((SUBLANES, LANES), F32)
    gate_par = gate_par.at[0, :GDN_HEADS].set(gdn_a_log[l]).at[1, :GDN_HEADS].set(gdn_dt_bias[l])
    p = dict(
        norm_mix=norm_mix[l].reshape(1, D_MODEL), w_in=w_perm, conv_w=conv_w[l], gate_par=gate_par,
        gdn_norm=gdn_norm[l].reshape(1, GDN_D), w_out=w_out[l].astype(BF16),
        norm_cross=norm_cross[l].reshape(1, D_MODEL), w_mq=w_mq[l].astype(BF16), w_mo=w_mo[l].astype(BF16),
        norm_ffn=norm_ffn[l].reshape(1, D_MODEL),
        w_router=jnp.pad(w_router[l], ((0, 0), (0, LANES - N_EXPERTS))),
        b_router=jnp.pad(b_router[l], (0, LANES - N_EXPERTS)).reshape(1, LANES))

    w_mkv = jnp.concatenate([w_mk[l], w_mv[l]], axis=1).astype(BF16)
    mkv = _norm_matmul(mem_prompt.reshape(bp * MEM_LEN, D_MODEL), norm_mem[l].reshape(1, D_MODEL), w_mkv,
                       min(bp * MEM_LEN, 1024), PROJ_TN)
    mk_p = mkv[:, :MEM_W].reshape(bp, MEM_LEN, MEM_W)
    mv_p = mkv[:, MEM_W:].reshape(bp, MEM_LEN, MEM_W)

    zero_hist = jnp.zeros((bp, CONV_W - 1, CONV_CH), F32)
    zero_state = jnp.zeros((bp, GDN_HEADS, GDN_D, GDN_D), F32)
    swa_p = functools.partial(_swa_prompt, table=rel_bias_table, sinks=swa_sinks[l])
    proj_p, conv_p, st_p, x2_p, hp_p, route_p = _trunk(x_prompt, zero_hist, zero_state, swa_p, mk_p, mv_p, CHUNK, p)

    k_past = cache_swa_k[l].reshape(bs, WIN_PAST, KV_B)
    v_past = cache_swa_v[l].reshape(bs, WIN_PAST, KV_B)
    swa_s = functools.partial(_swa_sample, k_past=k_past, v_past=v_past, table=rel_bias_table, sinks=swa_sinks[l])
    proj_s, conv_s, st_s, x2_s, hp_s, route_s = _trunk(
        x_sample, cache_conv[l], state_gdn[l], swa_s,
        cache_mem_k[l].reshape(bs, MEM_LEN, MEM_W), cache_mem_v[l].reshape(bs, MEM_LEN, MEM_W), ss, p)

    mp, ms = bp * sp, bs * ss
    x2 = jnp.concatenate([x2_p, x2_s], axis=0)
    hp = jnp.concatenate([hp_p, hp_s], axis=0)
    route = jnp.concatenate([route_p, route_s], axis=0)
    ntok = mp + ms
    top_i = route[:, :TOP_K].astype(jnp.int32)
    rows = ntok * TOP_K + N_EXPERTS * MOE_SUB
    nitems = (ntok * TOP_K) // MOE_ROWS + N_EXPERTS
    pos_flat, tail_start, item_expert, item_start, item_nsub = _routing_tables(top_i, nitems)
    xs = _dispatch(hp, pos_flat, tail_start, rows, 512)
    ys = _moe(xs, item_expert, item_start, item_nsub,
              w_gate[l], b_gate[l].reshape(N_EXPERTS, 1, D_FF), w_up[l], b_up[l].reshape(N_EXPERTS, 1, D_FF),
              w_down[l], b_down[l].reshape(N_EXPERTS, 1, D_MODEL))
    y = _combine(x2, route, final_norm.reshape(1, D_MODEL), ys, pos_flat, 256)
    y_p = y[:mp].reshape(bp, sp, D_MODEL)
    y_s = y[mp:].reshape(bs, ss, D_MODEL)

    def kv_window(proj3, col, past=None):
        new = proj3[:, :, col * LANES:(col + 1) * LANES]
        full = new if past is None else jnp.concatenate([past, new], axis=1)
        win = full[:, -WIN_PAST:]
        return win.reshape(win.shape[0], WIN_PAST, SWA_KV_HEADS, SWA_DH)[None]

    return (y_p, y_s,
            conv_p[None], st_p[None], kv_window(proj_p, COL_KB), kv_window(proj_p, COL_VB),
            mk_p.reshape(bp, MEM_LEN, MEM_HEADS, MEM_DH)[None], mv_p.reshape(bp, MEM_LEN, MEM_HEADS, MEM_DH)[None],
            conv_s[None], st_s[None], kv_window(proj_s, COL_KB, k_past), kv_window(proj_s, COL_VB, v_past))
```

```python
import functools
import math

import numpy as np
import jax
import jax.numpy as jnp
from jax import lax
from jax.experimental import pallas as pl
from jax.experimental.pallas import tpu as pltpu

F32 = jnp.float32
BF16 = jnp.bfloat16
HIGHEST = lax.Precision.HIGHEST

D_MODEL = 2048
CHUNK = 64
GDN_HEADS = 8
GDN_D = 128
CONV_W = 4
SWA_HEADS = 16
SWA_KV_HEADS = 2
SWA_GROUP = SWA_HEADS // SWA_KV_HEADS
SWA_DH = 64
WIN_PAST = 128
REL_BUCKETS = 32
REL_MAX_DIST = 128
MEM_LEN = 256
MEM_HEADS = 4
MEM_DH = 128
N_EXPERTS = 32
TOP_K = 4
D_FF = D_MODEL
SWIGLU_LIMIT = 7.0
SWIGLU_ALPHA = 1.702
EPS = 1e-6

QK_A = GDN_HEADS * GDN_D
V_A = GDN_HEADS * GDN_D
CONV_CH = 2 * QK_A + V_A
Q_B = SWA_HEADS * SWA_DH
KV_B = SWA_KV_HEADS * SWA_DH
MEM_W = MEM_HEADS * MEM_DH

LANES = 128
SUBLANES = 8
VMEM_LIMIT_BYTES = 58 * 1024 * 1024

PROJ_COLS = 5632
PROJ_TN = 512
COL_Z = CONV_CH // V_A
COL_QB = (CONV_CH + V_A) // Q_B
COL_KB = (CONV_CH + V_A + Q_B) // LANES
COL_VB = COL_KB + 1
COL_AB = COL_KB + 2

MOE_SUB = 256
MOE_ROWS = 1536
MOE_STAGE = 128
MOE_OUT_SLOTS = 4
MOE_TF = 256
MOE_NF = D_FF // MOE_TF
HALF = D_MODEL // 2
NORM_ROWS = 256
SWA_CHUNKS_PER_STEP = 4
CROSS_ROWS = 256
XROW_TILES = HALF // LANES
YROW_TILES = D_MODEL // LANES


def _cparams(sem):
    return pltpu.CompilerParams(dimension_semantics=sem, vmem_limit_bytes=VMEM_LIMIT_BYTES)


def _rms(x, gain):
    return x * lax.rsqrt(jnp.mean(x * x, axis=-1, keepdims=True) + EPS) * gain


def _norm_matmul_kernel(x_ref, g_ref, w_ref, o_ref, h_ref):
    @pl.when(pl.program_id(1) == 0)
    def _():
        def body(r, carry):
            rows = pl.ds(pl.multiple_of(r * NORM_ROWS, NORM_ROWS), NORM_ROWS)
            h_ref[rows, :] = _rms(x_ref[rows, :], g_ref[...]).astype(BF16)
            return carry

        lax.fori_loop(0, x_ref.shape[0] // NORM_ROWS, body, 0)

    o_ref[...] = jnp.dot(h_ref[...], w_ref[...], preferred_element_type=F32)


def _norm_matmul(x, gain, w, tm, tn):
    m, k = x.shape
    n = w.shape[1]
    return pl.pallas_call(
        _norm_matmul_kernel,
        out_shape=jax.ShapeDtypeStruct((m, n), F32),
        grid=(m // tm, n // tn),
        in_specs=[pl.BlockSpec((tm, k), lambda i, j: (i, 0)),
                  pl.BlockSpec((1, k), lambda i, j: (0, 0)),
                  pl.BlockSpec((k, tn), lambda i, j: (0, j))],
        out_specs=pl.BlockSpec((tm, tn), lambda i, j: (i, j)),
        scratch_shapes=[pltpu.VMEM((tm, k), BF16)],
        compiler_params=_cparams(("arbitrary", "arbitrary")),
        name="norm_matmul",
    )(x, gain, w)


def _tri_inverse(lows, c):
    row = lax.broadcasted_iota(jnp.int32, (c, c), 0)
    col = lax.broadcasted_iota(jnp.int32, (c, c), 1)
    eye = jnp.where(row == col, 1.0, 0.0).astype(F32)
    ps = [eye - low for low in lows]
    ms = list(lows)
    span = 1
    while 2 * span < c:
        mbs = [m.astype(BF16) for m in ms]
        ms = [jnp.dot(mb, mb, preferred_element_type=F32) for mb in mbs]
        ps = [p + jnp.dot(p.astype(BF16), m.astype(BF16), preferred_element_type=F32) for p, m in zip(ps, ms)]
        span *= 2
    return ps


def _gdn_kernel(qkv_ref, z_ref, ab_ref, hist_ref, s0_ref, convw_ref, gp_ref, norm_ref,
                o_ref, convnew_ref, s_ref, ext_ref, *, c):
    step = pl.program_id(1)

    @pl.when(step == 0)
    def _():
        ext_ref[0:SUBLANES, :] = hist_ref[...]
        s_ref[...] = s0_ref[...]

    ext_ref[SUBLANES:SUBLANES + c, :] = qkv_ref[...]
    base = SUBLANES - (CONV_W - 1)
    conv = ext_ref[base:base + c, :] * convw_ref[0:1, :]
    for j in range(1, CONV_W):
        conv = conv + ext_ref[base + j:base + j + c, :] * convw_ref[j:j + 1, :]
    tail = ext_ref[c:c + SUBLANES, :]
    convnew_ref[...] = tail
    ext_ref[0:SUBLANES, :] = tail

    act = conv * jax.nn.sigmoid(conv)

    ab = ab_ref[...]
    gp = gp_ref[...]
    g = -jnp.exp(gp[0:1, :]) * jax.nn.softplus(ab + gp[1:2, :])
    beta_all = jax.nn.sigmoid(ab)

    row = lax.broadcasted_iota(jnp.int32, (c, c), 0)
    col = lax.broadcasted_iota(jnp.int32, (c, c), 1)
    incl = row >= col
    strict = row > col
    tril = jnp.where(incl, 1.0, 0.0).astype(F32)
    gc_all = jnp.dot(tril, g, precision=HIGHEST, preferred_element_type=F32)
    gc_pad = jnp.concatenate([gc_all, jnp.zeros((LANES - c, LANES), F32)], axis=0) if c < LANES else gc_all
    gc_t = gc_pad.T

    scale = GDN_D ** -0.5
    heads = range(GDN_HEADS)
    s_old = [s_ref[h] for h in heads]
    z_all = z_ref[...]
    q, k, v = [], [], []
    for h in heads:
        lo = h * GDN_D
        qh = act[:, lo:lo + GDN_D]
        kh = act[:, QK_A + lo:QK_A + lo + GDN_D]
        q.append(qh * lax.rsqrt(jnp.sum(qh * qh, axis=-1, keepdims=True) + EPS) * scale)
        k.append(kh * lax.rsqrt(jnp.sum(kh * kh, axis=-1, keepdims=True) + EPS))
        v.append(act[:, 2 * QK_A + lo:2 * QK_A + lo + GDN_D])
    beta = [beta_all[:, GDN_HEADS + h:GDN_HEADS + h + 1] for h in heads]
    gc = [gc_all[:, h:h + 1] for h in heads]
    gc_last = [gc_all[c - 1:c, h:h + 1] for h in heads]
    decay = [jnp.exp(jnp.where(incl, gc[h] - gc_t[h:h + 1, 0:c], -jnp.inf)) for h in heads]
    egc = [jnp.exp(gc[h]) for h in heads]
    kb = [k[h] * beta[h] for h in heads]
    qk = [lax.dot_general(jnp.concatenate([q[h], kb[h]], axis=0).astype(BF16), k[h].astype(BF16),
                          (((1,), (1,)), ((), ())), preferred_element_type=F32) for h in heads]
    a_qk = [jnp.where(incl, qk[h][0:c] * decay[h], 0.0) for h in heads]
    tinv = _tri_inverse([jnp.where(strict, qk[h][c:2 * c] * decay[h], 0.0) for h in heads], c)
    uw = [jnp.dot(tinv[h].astype(BF16),
                  jnp.concatenate([v[h] * beta[h], kb[h] * egc[h]], axis=1).astype(BF16),
                  preferred_element_type=F32) for h in heads]
    ws_qs = [jnp.dot(jnp.concatenate([uw[h][:, GDN_D:2 * GDN_D], q[h] * egc[h]], axis=0).astype(BF16),
                     s_old[h].astype(BF16), preferred_element_type=F32) for h in heads]
    v_new = [(uw[h][:, 0:GDN_D] - ws_qs[h][0:c]).astype(BF16) for h in heads]
    o = [ws_qs[h][c:2 * c] + jnp.dot(a_qk[h].astype(BF16), v_new[h], preferred_element_type=F32)
         for h in heads]
    s_new = [s_old[h] * jnp.exp(gc_last[h]) + lax.dot_general(
        (k[h] * jnp.exp(gc_last[h] - gc[h])).astype(BF16), v_new[h], (((0,), (0,)), ((), ())),
        preferred_element_type=F32) for h in heads]
    gate = z_all * jax.nn.sigmoid(z_all)
    o_ref[...] = jnp.concatenate([_rms(o[h], norm_ref[...]) for h in heads], axis=1) * gate
    for h in heads:
        s_ref[h] = s_new[h]


def _gdn(proj3, hist8, s0, conv_w, gate_par, gdn_norm, c):
    b, t, _ = proj3.shape
    nchunks = t // c
    kern = functools.partial(_gdn_kernel, c=c)
    return pl.pallas_call(
        kern,
        out_shape=(jax.ShapeDtypeStruct((b, t, V_A), F32),
                   jax.ShapeDtypeStruct((b, SUBLANES, CONV_CH), F32),
                   jax.ShapeDtypeStruct((b, GDN_HEADS, GDN_D, GDN_D), F32)),
        grid=(b, nchunks),
        in_specs=[pl.BlockSpec((None, c, CONV_CH), lambda i, j: (i, j, 0)),
                  pl.BlockSpec((None, c, V_A), lambda i, j: (i, j, COL_Z)),
                  pl.BlockSpec((None, c, LANES), lambda i, j: (i, j, COL_AB)),
                  pl.BlockSpec((None, SUBLANES, CONV_CH), lambda i, j: (i, 0, 0)),
                  pl.BlockSpec((None, GDN_HEADS, GDN_D, GDN_D), lambda i, j: (i, 0, 0, 0)),
                  pl.BlockSpec((CONV_W, CONV_CH), lambda i, j: (0, 0)),
                  pl.BlockSpec((SUBLANES, LANES), lambda i, j: (0, 0)),
                  pl.BlockSpec((1, GDN_D), lambda i, j: (0, 0))],
        out_specs=(pl.BlockSpec((None, c, V_A), lambda i, j: (i, j, 0)),
                   pl.BlockSpec((None, SUBLANES, CONV_CH), lambda i, j: (i, 0, 0)),
                   pl.BlockSpec((None, GDN_HEADS, GDN_D, GDN_D), lambda i, j: (i, 0, 0, 0))),
        scratch_shapes=[pltpu.VMEM((SUBLANES + c, CONV_CH), F32)],
        compiler_params=_cparams(("arbitrary", "arbitrary")),
        name="gdn_mixer",
    )(proj3, proj3, proj3, hist8, s0, conv_w, gate_par, gdn_norm)


def _swa_kernel(*refs, nq, nsub, piece_rows, masked):
    npieces = len(piece_rows)
    q_ref = refs[0]
    k_refs = refs[1:1 + npieces]
    v_refs = refs[1 + npieces:1 + 2 * npieces]
    bias_ref, sink_ref, o_ref = refs[1 + 2 * npieces:]
    nk = WIN_PAST + nq
    kcat = jnp.concatenate([r[...] for r in k_refs], axis=0).astype(BF16)
    vcat = jnp.concatenate([r[...] for r in v_refs], axis=0).astype(BF16)
    q = q_ref[...]
    kidx = lax.broadcasted_iota(jnp.int32, (SWA_GROUP * nq, nk), 1)
    out_rows = []
    for i in range(nsub):
        heads = []
        for kv in range(SWA_KV_HEADS):
            qs = jnp.concatenate(
                [q[i * nq:(i + 1) * nq, (kv * SWA_GROUP + g) * SWA_DH:(kv * SWA_GROUP + g + 1) * SWA_DH]
                 for g in range(SWA_GROUP)], axis=0).astype(BF16)
            kh = kcat[i * nq:i * nq + nk, kv * SWA_DH:(kv + 1) * SWA_DH]
            vh = vcat[i * nq:i * nq + nk, kv * SWA_DH:(kv + 1) * SWA_DH]
            s = lax.dot_general(qs, kh, (((1,), (1,)), ((), ())), preferred_element_type=F32)
            s = s * SWA_DH ** -0.5 + bias_ref[kv]
            if masked:
                first = (pl.program_id(1) * nsub + i) * nq - WIN_PAST
                s = jnp.where(kidx + first >= 0, s, -jnp.inf)
            sk = sink_ref[kv]
            m = jnp.maximum(jnp.max(s, axis=-1, keepdims=True), sk)
            p = jnp.exp(s - m)
            den = jnp.sum(p, axis=-1, keepdims=True) + jnp.exp(sk - m)
            p = (p / den).astype(BF16)
            o = jnp.dot(p, vh, preferred_element_type=F32)
            heads += [o[g * nq:(g + 1) * nq, :] for g in range(SWA_GROUP)]
        out_rows.append(jnp.concatenate(heads, axis=1))
    o_ref[...] = out_rows[0] if nsub == 1 else jnp.concatenate(out_rows, axis=0)


def _rel_bucket(nq, nk):
    rel = jnp.arange(nk)[None, :] - WIN_PAST - jnp.arange(nq)[:, None]
    nb = REL_BUCKETS // 2
    max_exact = nb // 2
    n = jnp.abs(rel)
    large = max_exact + (jnp.log(jnp.maximum(n, 1).astype(F32) / max_exact)
                         / math.log(REL_MAX_DIST / max_exact) * (nb - max_exact)).astype(jnp.int32)
    large = jnp.minimum(large, nb - 1)
    return jnp.where(rel > 0, nb, 0) + jnp.where(n < max_exact, n, large)


def _swa_tables(table, sinks, nq, nk):
    onehot = (_rel_bucket(nq, nk)[:, :, None] == jnp.arange(REL_BUCKETS)[None, None, :]).astype(F32)
    bias = jnp.einsum("qkb,bh->qkh", onehot, table.astype(F32), precision=HIGHEST)
    bias = jnp.transpose(bias, (2, 0, 1)).astype(F32)
    bias = bias.reshape(SWA_KV_HEADS, SWA_GROUP * nq, nk)
    sk = jnp.broadcast_to(sinks.astype(F32).reshape(SWA_KV_HEADS, SWA_GROUP, 1, 1),
                          (SWA_KV_HEADS, SWA_GROUP, nq, 1)).reshape(SWA_KV_HEADS, SWA_GROUP * nq, 1)
    return bias, sk


def _swa_prompt(proj3, table, sinks):
    b, t, _ = proj3.shape
    nq = CHUNK
    nsub = SWA_CHUNKS_PER_STEP
    rows = nsub * nq
    assert rows % WIN_PAST == 0 and t % rows == 0
    past_per_step = rows // WIN_PAST
    bias, sk = _swa_tables(table, sinks, nq, WIN_PAST + nq)
    kern = functools.partial(_swa_kernel, nq=nq, nsub=nsub, piece_rows=(WIN_PAST, rows), masked=True)

    def past_spec(colblk):
        return pl.BlockSpec((None, WIN_PAST, LANES),
                            lambda i, j: (i, jnp.maximum(j * past_per_step - 1, 0), colblk))

    def cur_spec(colblk):
        return pl.BlockSpec((None, rows, LANES), lambda i, j: (i, j, colblk))

    in_specs = [pl.BlockSpec((None, rows, Q_B), lambda i, j: (i, j, COL_QB)),
                past_spec(COL_KB), cur_spec(COL_KB), past_spec(COL_VB), cur_spec(COL_VB),
                pl.BlockSpec(bias.shape, lambda i, j: (0, 0, 0)),
                pl.BlockSpec(sk.shape, lambda i, j: (0, 0, 0))]
    return pl.pallas_call(
        kern,
        out_shape=jax.ShapeDtypeStruct((b, t, Q_B), F32),
        grid=(b, t // rows),
        in_specs=in_specs,
        out_specs=pl.BlockSpec((None, rows, Q_B), lambda i, j: (i, j, 0)),
        compiler_params=_cparams(("arbitrary", "arbitrary")),
        name="swa_prompt",
    )(proj3, proj3, proj3, proj3, proj3, bias, sk)


def _swa_sample(proj3, k_past, v_past, table, sinks):
    b, t, _ = proj3.shape
    bias, sk = _swa_tables(table, sinks, t, WIN_PAST + t)
    kern = functools.partial(_swa_kernel, nq=t, nsub=1, piece_rows=(WIN_PAST, t), masked=False)
    in_specs = [pl.BlockSpec((None, t, Q_B), lambda i, j: (i, 0, COL_QB)),
                pl.BlockSpec((None, WIN_PAST, LANES), lambda i, j: (i, 0, 0)),
                pl.BlockSpec((None, t, LANES), lambda i, j: (i, 0, COL_KB)),
                pl.BlockSpec((None, WIN_PAST, LANES), lambda i, j: (i, 0, 0)),
                pl.BlockSpec((None, t, LANES), lambda i, j: (i, 0, COL_VB)),
                pl.BlockSpec(bias.shape, lambda i, j: (0, 0, 0)),
                pl.BlockSpec(sk.shape, lambda i, j: (0, 0, 0))]
    return pl.pallas_call(
        kern,
        out_shape=jax.ShapeDtypeStruct((b, t, Q_B), F32),
        grid=(b, 1),
        in_specs=in_specs,
        out_specs=pl.BlockSpec((None, t, Q_B), lambda i, j: (i, 0, 0)),
        compiler_params=_cparams(("arbitrary", "arbitrary")),
        name="swa_sample",
    )(proj3, k_past, proj3, v_past, proj3, bias, sk)


def _outproj_kernel(x_ref, oa_ref, ob_ref, wa_ref, wb_ref, g_ref, wq_ref, x1_ref, q_ref):
    x1 = (x_ref[...]
          + jnp.dot(oa_ref[...].astype(BF16), wa_ref[...], preferred_element_type=F32)
          + jnp.dot(ob_ref[...].astype(BF16), wb_ref[...], preferred_element_type=F32))
    x1_ref[...] = x1
    hc = _rms(x1, g_ref[...]).astype(BF16)
    q_ref[...] = jnp.dot(hc, wq_ref[...], preferred_element_type=F32)


def _outproj(x, o_a, o_b, w_out, norm_cross, w_mq, tm):
    m = x.shape[0]
    return pl.pallas_call(
        _outproj_kernel,
        out_shape=(jax.ShapeDtypeStruct((m, D_MODEL), F32), jax.ShapeDtypeStruct((m, MEM_W), F32)),
        grid=(m // tm,),
        in_specs=[pl.BlockSpec((tm, D_MODEL), lambda i: (i, 0)),
                  pl.BlockSpec((tm, V_A), lambda i: (i, 0)),
                  pl.BlockSpec((tm, Q_B), lambda i: (i, 0)),
                  pl.BlockSpec((V_A, D_MODEL), lambda i: (0, 0)),
                  pl.BlockSpec((Q_B, D_MODEL), lambda i: (1, 0)),
                  pl.BlockSpec((1, D_MODEL), lambda i: (0, 0)),
                  pl.BlockSpec((D_MODEL, MEM_W), lambda i: (0, 0))],
        out_specs=(pl.BlockSpec((tm, D_MODEL), lambda i: (i, 0)),
                   pl.BlockSpec((tm, MEM_W), lambda i: (i, 0))),
        compiler_params=_cparams(("arbitrary",)),
        name="outproj",
    )(x, o_a, o_b, w_out, w_out, norm_cross, w_mq)


def _cross_router_kernel(x1_ref, q_ref, mk_ref, mv_ref, wo_ref, g_ref, wr_ref, br_ref,
                         x2_ref, hp_ref, route_ref):
    nb, tm = x1_ref.shape[0], x1_ref.shape[1]
    rows = []
    for bi in range(nb):
        q = q_ref[bi]
        outs = []
        for h in range(MEM_HEADS):
            sl = slice(h * MEM_DH, (h + 1) * MEM_DH)
            mk = mk_ref[bi, pl.ds(h, MEM_LEN, stride=MEM_HEADS), :].astype(BF16)
            mv = mv_ref[bi, pl.ds(h, MEM_LEN, stride=MEM_HEADS), :].astype(BF16)
            s = lax.dot_general(q[:, sl].astype(BF16), mk, (((1,), (1,)), ((), ())),
                                preferred_element_type=F32) * MEM_DH ** -0.5
            m = jnp.max(s, axis=-1, keepdims=True)
            p = jnp.exp(s - m)
            p = (p / jnp.sum(p, axis=-1, keepdims=True)).astype(BF16)
            outs.append(jnp.dot(p, mv, preferred_element_type=F32))
        rows.append(jnp.concatenate(outs, axis=1))
    att = (rows[0] if nb == 1 else jnp.concatenate(rows, axis=0)).astype(BF16)
    x2 = x1_ref[...].reshape(nb * tm, D_MODEL) + jnp.dot(att, wo_ref[...], preferred_element_type=F32)
    x2_ref[...] = x2.reshape(nb, tm, D_MODEL)
    hf = _rms(x2, g_ref[...])
    lo = pltpu.bitcast(hf[:, 0:HALF].astype(BF16).astype(F32), jnp.uint32)
    hi = pltpu.bitcast(hf[:, HALF:D_MODEL].astype(BF16).astype(F32), jnp.uint32)
    packed = (lo >> 16) | (hi & jnp.uint32(0xFFFF0000))
    for bi in range(nb):
        for j in range(XROW_TILES):
            hp_ref[bi, pl.ds(j, tm, stride=XROW_TILES), :] = packed[bi * tm:(bi + 1) * tm, j * LANES:(j + 1) * LANES]
    logits = jnp.dot(hf, wr_ref[...], precision=HIGHEST, preferred_element_type=F32) + br_ref[...]
    lane = lax.broadcasted_iota(jnp.int32, logits.shape, 1)
    lanef = lane.astype(F32)
    l = jnp.where(lane < N_EXPERTS, logits, -jnp.inf)
    vals, idxs = [], []
    for _ in range(TOP_K):
        m = jnp.max(l, axis=-1, keepdims=True)
        idx = jnp.min(jnp.where(l == m, lanef, float(LANES)), axis=-1, keepdims=True)
        vals.append(m)
        idxs.append(idx)
        l = jnp.where(lanef == idx, -jnp.inf, l)
    es = [jnp.exp(v - vals[0]) for v in vals]
    den = es[0] + es[1] + es[2] + es[3]
    route = jnp.zeros(logits.shape, F32)
    for k in range(TOP_K):
        route = jnp.where(lane == k, idxs[k], route)
        route = jnp.where(lane == TOP_K + k, es[k] / den, route)
    route_ref[...] = route.reshape(nb, tm, LANES)


def _cross_router(x1, qm, mk, mv, w_mo, norm_ffn, w_router, b_router, nb, tm):
    b, t, _ = x1.shape
    nt = t // tm
    return pl.pallas_call(
        _cross_router_kernel,
        out_shape=(jax.ShapeDtypeStruct((b, t, D_MODEL), F32),
                   jax.ShapeDtypeStruct((b, t * XROW_TILES, LANES), jnp.uint32),
                   jax.ShapeDtypeStruct((b, t, LANES), F32)),
        grid=(b // nb, nt),
        in_specs=[pl.BlockSpec((nb, tm, D_MODEL), lambda i, j: (i, j, 0)),
                  pl.BlockSpec((nb, tm, MEM_W), lambda i, j: (i, j, 0)),
                  pl.BlockSpec((nb, MEM_LEN * MEM_HEADS, MEM_DH), lambda i, j: (i, 0, 0)),
                  pl.BlockSpec((nb, MEM_LEN * MEM_HEADS, MEM_DH), lambda i, j: (i, 0, 0)),
                  pl.BlockSpec((MEM_W, D_MODEL), lambda i, j: (0, 0)),
                  pl.BlockSpec((1, D_MODEL), lambda i, j: (0, 0)),
                  pl.BlockSpec((D_MODEL, LANES), lambda i, j: (0, 0)),
                  pl.BlockSpec((1, LANES), lambda i, j: (0, 0))],
        out_specs=(pl.BlockSpec((nb, tm, D_MODEL), lambda i, j: (i, j, 0)),
                   pl.BlockSpec((nb, tm * XROW_TILES, LANES), lambda i, j: (i, j, 0)),
                   pl.BlockSpec((nb, tm, LANES), lambda i, j: (i, j, 0))),
        compiler_params=_cparams(("arbitrary", "arbitrary")),
        name="cross_router",
    )(x1, qm, mk, mv, w_mo, norm_ffn, w_router, b_router)


def _row_tile(idx, tiles):
    return pl.ds(pl.multiple_of(idx * tiles, tiles), tiles)


def _dispatch_kernel(*refs, tokens, first_group):
    if first_group:
        pos_ref, tail_ref, hp_ref, xs_hbm, zero_ref, sem = refs
    else:
        pos_ref, tail_ref, hp_ref, _, xs_hbm, zero_ref, sem = refs

    def zero_tails():
        zero_ref[...] = jnp.zeros(zero_ref.shape, zero_ref.dtype)

        def tail_copy(e):
            dst = xs_hbm.at[pl.ds(pl.multiple_of(tail_ref[e] * XROW_TILES, XROW_TILES), MOE_SUB * XROW_TILES)]
            return pltpu.make_async_copy(zero_ref, dst, sem.at[0])

        for e in range(N_EXPERTS):
            pl.when(tail_ref[e] >= 0)(lambda e=e: tail_copy(e).start())
        for e in range(N_EXPERTS):
            pl.when(tail_ref[e] >= 0)(lambda e=e: tail_copy(e).wait())

    if first_group:
        pl.when(pl.program_id(0) == 0)(zero_tails)

    def issue(t, carry):
        src = hp_ref.at[_row_tile(t, XROW_TILES)]
        for k in range(TOP_K):
            dst = xs_hbm.at[_row_tile(pos_ref[t * TOP_K + k], XROW_TILES)]
            pltpu.make_async_copy(src, dst, sem.at[1]).start()
        return carry

    lax.fori_loop(0, tokens, issue, 0, unroll=4)

    def drain(t, carry):
        for k in range(TOP_K):
            pltpu.make_async_copy(hp_ref.at[_row_tile(0, XROW_TILES)], xs_hbm.at[_row_tile(0, XROW_TILES)],
                                  sem.at[1]).wait()
        return carry

    lax.fori_loop(0, tokens, drain, 0)


def _dispatch(hp, pos_flat, tail_start, rows, tokens, xs_prev=None):
    t = hp.shape[0] // XROW_TILES
    first_group = xs_prev is None
    kern = functools.partial(_dispatch_kernel, tokens=tokens, first_group=first_group)
    in_specs = [pl.BlockSpec((tokens * TOP_K,), lambda i: (i,), memory_space=pltpu.SMEM),
                pl.BlockSpec((N_EXPERTS,), lambda i: (0,), memory_space=pltpu.SMEM),
                pl.BlockSpec((tokens * XROW_TILES, LANES), lambda i: (i, 0))]
    args = [pos_flat, tail_start, hp]
    if not first_group:
        in_specs.append(pl.BlockSpec(memory_space=pl.ANY))
        args.append(xs_prev)
    return pl.pallas_call(
        kern,
        out_shape=jax.ShapeDtypeStruct((rows * XROW_TILES, LANES), jnp.uint32),
        grid=(t // tokens,),
        in_specs=in_specs,
        out_specs=pl.BlockSpec(memory_space=pl.ANY),
        scratch_shapes=[pltpu.VMEM((MOE_SUB * XROW_TILES, LANES), jnp.uint32),
                        pltpu.SemaphoreType.DMA((2,))],
        input_output_aliases={} if first_group else {3: 0},
        compiler_params=pltpu.CompilerParams(dimension_semantics=("arbitrary",),
                                             vmem_limit_bytes=VMEM_LIMIT_BYTES, has_side_effects=True),
        name="moe_dispatch",
    )(*args)


def _moe_kernel(ie_ref, is_ref, ns_ref, xs_hbm, wg_ref, bg_ref, wu_ref, bu_ref, wd_ref, bd_ref,
                ys_hbm, xu_ref, xb_ref, acc_ref, ystage_ref, in_sem, out_sem):
    w = pl.program_id(0)
    f = pl.program_id(1)
    nsub = ns_ref[w]
    start = is_ref[w]

    def sub_rows(s):
        return pl.ds(pl.multiple_of(s * MOE_SUB, MOE_SUB), MOE_SUB)

    nstage = nsub * (MOE_SUB // MOE_STAGE)

    def stage_rows(s):
        return pl.ds(pl.multiple_of(s * MOE_STAGE, MOE_STAGE), MOE_STAGE)

    def hbm_rows(first, s, tiles):
        return pl.ds(pl.multiple_of((first + s * MOE_STAGE) * tiles, MOE_STAGE * tiles), MOE_STAGE * tiles)

    def load(first, s):
        dst = xu_ref.at[pl.ds(pl.multiple_of(s * MOE_STAGE * XROW_TILES, MOE_STAGE * XROW_TILES),
                              MOE_STAGE * XROW_TILES)]
        return pltpu.make_async_copy(xs_hbm.at[hbm_rows(first, s, XROW_TILES)], dst, in_sem.at[0])

    def fetch_item(first, stages):
        def issue(s, carry):
            load(first, s).start()
            return carry

        lax.fori_loop(0, stages, issue, 0)

    pl.when(jnp.logical_and(jnp.logical_and(w == 0, f == 0), nsub > 0))(lambda: fetch_item(start, nstage))

    @pl.when(jnp.logical_and(f == 0, nsub > 0))
    def _():
        def landed(s, carry):
            load(start, s).wait()
            return carry

        lax.fori_loop(0, nstage, landed, 0)

        def unpack(s, carry):
            base = pl.multiple_of(s * MOE_STAGE * XROW_TILES, MOE_STAGE * XROW_TILES)
            for j in range(XROW_TILES):
                u = xu_ref[pl.ds(base + j, MOE_STAGE, stride=XROW_TILES), :]
                cols = slice(j * LANES, (j + 1) * LANES)
                xb_ref[stage_rows(s), cols] = pltpu.bitcast(u << 16, F32).astype(BF16)
                hcols = slice(HALF + j * LANES, HALF + (j + 1) * LANES)
                xb_ref[stage_rows(s), hcols] = pltpu.bitcast(u & jnp.uint32(0xFFFF0000), F32).astype(BF16)
            acc_ref[stage_rows(s), :] = jnp.broadcast_to(bd_ref[...], (MOE_STAGE, D_MODEL))
            return carry

        lax.fori_loop(0, nstage, unpack, 0)

    nxt = jnp.minimum(w + 1, pl.num_programs(0) - 1)
    nxt_stages = jnp.where(w + 1 < pl.num_programs(0), ns_ref[nxt], 0) * (MOE_SUB // MOE_STAGE)
    pl.when(jnp.logical_and(f == MOE_NF - 1, nsub > 0))(lambda: fetch_item(is_ref[nxt], nxt_stages))

    @pl.when(nsub > 0)
    def _():
        wg = wg_ref[...].astype(BF16)
        wu = wu_ref[...].astype(BF16)
        wd = wd_ref[...].astype(BF16)
        bg = bg_ref[...]
        bu = bu_ref[...]

        def body(s, carry):
            x = xb_ref[sub_rows(s), :]
            gt = jnp.minimum(jnp.dot(x, wg, preferred_element_type=F32) + bg, SWIGLU_LIMIT)
            up = jnp.clip(jnp.dot(x, wu, preferred_element_type=F32) + bu, -SWIGLU_LIMIT, SWIGLU_LIMIT)
            a = gt * jax.nn.sigmoid(SWIGLU_ALPHA * gt) * (up + 1.0)
            acc_ref[sub_rows(s), :] += jnp.dot(a.astype(BF16), wd, preferred_element_type=F32)
            return carry

        lax.fori_loop(0, nsub, body, 0)

    @pl.when(jnp.logical_and(f == MOE_NF - 1, nsub > 0))
    def _():
        def store(s):
            slot = s % MOE_OUT_SLOTS
            return pltpu.make_async_copy(ystage_ref.at[slot], ys_hbm.at[hbm_rows(start, s, YROW_TILES)],
                                         out_sem.at[slot])

        def write(s, carry):
            pl.when(s >= MOE_OUT_SLOTS)(lambda: store(s - MOE_OUT_SLOTS).wait())
            for j in range(YROW_TILES):
                ystage_ref[s % MOE_OUT_SLOTS, pl.ds(j, MOE_STAGE, stride=YROW_TILES), :] = \
                    acc_ref[stage_rows(s), j * LANES:(j + 1) * LANES]
            store(s).start()
            return carry

        lax.fori_loop(0, nstage, write, 0)
        for back in range(1, MOE_OUT_SLOTS + 1):
            pl.when(nstage >= back)(lambda back=back: store(nstage - back).wait())


def _moe(xs, item_expert, item_start, item_nsub, w_gate, b_gate, w_up, b_up, w_down, b_down):
    rows = xs.shape[0] // XROW_TILES
    nitems = item_expert.shape[0]

    def fcol(w, f, ie, st, ns):
        return jnp.where(ns[w] > 0, f, MOE_NF - 1)

    grid_spec = pltpu.PrefetchScalarGridSpec(
        num_scalar_prefetch=3,
        grid=(nitems, MOE_NF),
        in_specs=[pl.BlockSpec(memory_space=pl.ANY),
                  pl.BlockSpec((None, D_MODEL, MOE_TF), lambda w, f, ie, st, ns: (ie[w], 0, fcol(w, f, ie, st, ns))),
                  pl.BlockSpec((None, 1, MOE_TF), lambda w, f, ie, st, ns: (ie[w], 0, fcol(w, f, ie, st, ns))),
                  pl.BlockSpec((None, D_MODEL, MOE_TF), lambda w, f, ie, st, ns: (ie[w], 0, fcol(w, f, ie, st, ns))),
                  pl.BlockSpec((None, 1, MOE_TF), lambda w, f, ie, st, ns: (ie[w], 0, fcol(w, f, ie, st, ns))),
                  pl.BlockSpec((None, MOE_TF, D_MODEL), lambda w, f, ie, st, ns: (ie[w], fcol(w, f, ie, st, ns), 0)),
                  pl.BlockSpec((None, 1, D_MODEL), lambda w, f, ie, st, ns: (ie[w], 0, 0))],
        out_specs=pl.BlockSpec(memory_space=pl.ANY),
        scratch_shapes=[pltpu.VMEM((MOE_ROWS * XROW_TILES, LANES), jnp.uint32),
                        pltpu.VMEM((MOE_ROWS, D_MODEL), BF16),
                        pltpu.VMEM((MOE_ROWS, D_MODEL), F32),
                        pltpu.VMEM((MOE_OUT_SLOTS, MOE_STAGE * YROW_TILES, LANES), F32),
                        pltpu.SemaphoreType.DMA((1,)),
                        pltpu.SemaphoreType.DMA((MOE_OUT_SLOTS,))])
    return pl.pallas_call(
        _moe_kernel,
        out_shape=jax.ShapeDtypeStruct((rows * YROW_TILES, LANES), F32),
        grid_spec=grid_spec,
        compiler_params=pltpu.CompilerParams(dimension_semantics=("arbitrary", "arbitrary"),
                                             vmem_limit_bytes=VMEM_LIMIT_BYTES, has_side_effects=True),
        name="moe_experts",
    )(item_expert, item_start, item_nsub, xs, w_gate, b_gate, w_up, b_up, w_down, b_down)


def _combine_kernel(pos_ref, nxt_ref, x2_ref, route_ref, g_ref, ys_hbm, o_ref, buf_ref, sem, *, tokens):
    i = pl.program_id(0)
    slot = i % 2

    def gather(p_ref, dst_slot):
        def issue(t, carry):
            for k in range(TOP_K):
                pltpu.make_async_copy(ys_hbm.at[_row_tile(p_ref[t * TOP_K + k], YROW_TILES)],
                                      buf_ref.at[dst_slot, k, _row_tile(t, YROW_TILES)], sem.at[dst_slot]).start()
            return carry

        lax.fori_loop(0, tokens, issue, 0, unroll=4)

    pl.when(i == 0)(lambda: gather(pos_ref, 0))
    pl.when(i + 1 < pl.num_programs(0))(lambda: gather(nxt_ref, 1 - slot))

    def drain(t, carry):
        for k in range(TOP_K):
            pltpu.make_async_copy(ys_hbm.at[_row_tile(0, YROW_TILES)],
                                  buf_ref.at[slot, k, _row_tile(0, YROW_TILES)], sem.at[slot]).wait()
        return carry

    lax.fori_loop(0, tokens, drain, 0)

    route = route_ref[...]
    gates = [route[:, TOP_K + k:TOP_K + k + 1] for k in range(TOP_K)]
    cols = []
    for j in range(YROW_TILES):
        y = x2_ref[:, j * LANES:(j + 1) * LANES]
        for k in range(TOP_K):
            y = y + gates[k] * buf_ref[slot, k, pl.ds(j, tokens, stride=YROW_TILES), :]
        cols.append(y)
    o_ref[...] = _rms(jnp.concatenate(cols, axis=1), g_ref[...])


def _combine(x2, route, final_norm, ys, pos_flat, tokens):
    t = x2.shape[0]
    nsteps = t // tokens
    kern = functools.partial(_combine_kernel, tokens=tokens)
    return pl.pallas_call(
        kern,
        out_shape=jax.ShapeDtypeStruct((t, D_MODEL), F32),
        grid=(nsteps,),
        in_specs=[pl.BlockSpec((tokens * TOP_K,), lambda i: (i,), memory_space=pltpu.SMEM),
                  pl.BlockSpec((tokens * TOP_K,), lambda i: (jnp.minimum(i + 1, nsteps - 1),),
                               memory_space=pltpu.SMEM),
                  pl.BlockSpec((tokens, D_MODEL), lambda i: (i, 0)),
                  pl.BlockSpec((tokens, LANES), lambda i: (i, 0)),
                  pl.BlockSpec((1, D_MODEL), lambda i: (0, 0)),
                  pl.BlockSpec(memory_space=pl.ANY)],
        out_specs=pl.BlockSpec((tokens, D_MODEL), lambda i: (i, 0)),
        scratch_shapes=[pltpu.VMEM((2, TOP_K, tokens * YROW_TILES, LANES), F32),
                        pltpu.SemaphoreType.DMA((2,))],
        compiler_params=_cparams(("arbitrary",)),
        name="moe_combine",
    )(pos_flat, pos_flat, x2, route, final_norm, ys)


def _routing_tables(top_i, nitems):
    t = top_i.shape[0]
    sel = jnp.sum((top_i[:, :, None] == jnp.arange(N_EXPERTS, dtype=jnp.int32)[None, None, :]).astype(jnp.int32),
                  axis=1)
    cnt = jnp.sum(sel, axis=0)
    rank = jnp.cumsum(sel, axis=0) - sel
    cnt_pad = ((cnt + MOE_SUB - 1) // MOE_SUB) * MOE_SUB
    off = jnp.cumsum(cnt_pad) - cnt_pad
    pos = jnp.take_along_axis(off[None, :] + rank, top_i, axis=1)
    tail_start = jnp.where(cnt > 0, off + cnt_pad - MOE_SUB, -1)
    items_per = (cnt + MOE_ROWS - 1) // MOE_ROWS
    item_end = jnp.cumsum(items_per)
    total = item_end[-1]
    widx = jnp.arange(nitems, dtype=jnp.int32)
    e_of = jnp.minimum(jnp.searchsorted(item_end, widx, side="right"), N_EXPERTS - 1).astype(jnp.int32)
    j_of = widx - (item_end - items_per)[e_of]
    valid = widx < total
    e_last = e_of[jnp.maximum(total - 1, 0)]
    item_expert = jnp.where(valid, e_of, e_last).astype(jnp.int32)
    item_start = jnp.where(valid, off[e_of] + j_of * MOE_ROWS, 0).astype(jnp.int32)
    rows_left = cnt_pad[e_of] - j_of * MOE_ROWS
    item_nsub = jnp.where(valid, jnp.minimum(rows_left, MOE_ROWS) // MOE_SUB, 0).astype(jnp.int32)
    return pos.reshape(t * TOP_K).astype(jnp.int32), tail_start.astype(jnp.int32), item_expert, item_start, item_nsub


def _trunk(x, conv_hist, s0, swa_fn, mk, mv, gdn_chunk, p):
    b, t, _ = x.shape
    m = b * t
    xf = x.reshape(m, D_MODEL)
    proj = _norm_matmul(xf, p["norm_mix"], p["w_in"], min(m, 1024), PROJ_TN)
    proj3 = proj.reshape(b, t, PROJ_COLS)
    hist8 = jnp.pad(conv_hist, ((0, 0), (SUBLANES - (CONV_W - 1), 0), (0, 0)))
    o_a, conv8, s_new = _gdn(proj3, hist8, s0, p["conv_w"], p["gate_par"], p["gdn_norm"], gdn_chunk)
    o_b = swa_fn(proj3)
    x1, qm = _outproj(xf, o_a.reshape(m, V_A), o_b.reshape(m, Q_B), p["w_out"], p["norm_cross"], p["w_mq"], 256)
    tm = min(t, CROSS_ROWS)
    x2, hp, route = _cross_router(x1.reshape(b, t, D_MODEL), qm.reshape(b, t, MEM_W), mk, mv,
                                  p["w_mo"], p["norm_ffn"], p["w_router"], p["b_router"], CROSS_ROWS // tm, tm)
    return proj3, conv8[:, SUBLANES - (CONV_W - 1):], s_new, x2.reshape(m, D_MODEL), \
        hp.reshape(m * XROW_TILES, LANES), route.reshape(m, LANES)


def kernel(x_prompt, x_sample, cache_conv, state_gdn, cache_swa_k, cache_swa_v, cache_mem_k, cache_mem_v, mem_prompt, norm_mix, w_in, conv_w, gdn_a_log, gdn_dt_bias, gdn_norm, swa_sinks, rel_bias_table, w_out, norm_cross, norm_mem, w_mq, w_mk, w_mv, w_mo, norm_ffn, w_router, b_router, w_gate, b_gate, w_up, b_up, w_down, b_down, final_norm):
    depth = norm_mix.shape[0]
    assert depth == 1, "kernel is written for the single-layer trunk"
    bp, sp, _ = x_prompt.shape
    bs, ss, _ = x_sample.shape
    l = 0
    w = w_in[l]
    n_ab = 2 * GDN_HEADS
    w_perm = jnp.concatenate(
        [w[:, :CONV_CH + V_A], w[:, CONV_CH + V_A + n_ab:], w[:, CONV_CH + V_A:CONV_CH + V_A + n_ab],
         jnp.zeros((D_MODEL, PROJ_COLS - w.shape[1]), w.dtype)], axis=1).astype(BF16)
    gate_par = jnp.zeros((SUBLANES, LANES), F32)
    gate_par = gate_par.at[0, :GDN_HEADS].set(gdn_a_log[l]).at[1, :GDN_HEADS].set(gdn_dt_bias[l])
    p = dict(
        norm_mix=norm_mix[l].reshape(1, D_MODEL), w_in=w_perm, conv_w=conv_w[l], gate_par=gate_par,
        gdn_norm=gdn_norm[l].reshape(1, GDN_D), w_out=w_out[l].astype(BF16),
        norm_cross=norm_cross[l].reshape(1, D_MODEL), w_mq=w_mq[l].astype(BF16), w_mo=w_mo[l].astype(BF16),
        norm_ffn=norm_ffn[l].reshape(1, D_MODEL),
        w_router=jnp.pad(w_router[l], ((0, 0), (0, LANES - N_EXPERTS))),
        b_router=jnp.pad(b_router[l], (0, LANES - N_EXPERTS)).reshape(1, LANES))

    w_mkv = jnp.concatenate([w_mk[l], w_mv[l]], axis=1).astype(BF16)
    mkv = _norm_matmul(mem_prompt.reshape(bp * MEM_LEN, D_MODEL), norm_mem[l].reshape(1, D_MODEL), w_mkv,
                       min(bp * MEM_LEN, 1024), PROJ_TN)
    mk_p = mkv[:, :MEM_W].reshape(bp, MEM_LEN * MEM_HEADS, MEM_DH)
    mv_p = mkv[:, MEM_W:].reshape(bp, MEM_LEN * MEM_HEADS, MEM_DH)

    zero_hist = jnp.zeros((bp, CONV_W - 1, CONV_CH), F32)
    zero_state = jnp.zeros((bp, GDN_HEADS, GDN_D, GDN_D), F32)
    swa_p = functools.partial(_swa_prompt, table=rel_bias_table, sinks=swa_sinks[l])
    proj_p, conv_p, st_p, x2_p, hp_p, route_p = _trunk(x_prompt, zero_hist, zero_state, swa_p, mk_p, mv_p, CHUNK, p)

    k_past = cache_swa_k[l].reshape(bs, WIN_PAST, KV_B)
    v_past = cache_swa_v[l].reshape(bs, WIN_PAST, KV_B)
    swa_s = functools.partial(_swa_sample, k_past=k_past, v_past=v_past, table=rel_bias_table, sinks=swa_sinks[l])
    proj_s, conv_s, st_s, x2_s, hp_s, route_s = _trunk(
        x_sample, cache_conv[l], state_gdn[l], swa_s,
        cache_mem_k[l].reshape(bs, MEM_LEN * MEM_HEADS, MEM_DH),
        cache_mem_v[l].reshape(bs, MEM_LEN * MEM_HEADS, MEM_DH), ss, p)

    mp, ms = bp * sp, bs * ss
    ntok = mp + ms
    top_i = jnp.concatenate([route_p[:, :TOP_K], route_s[:, :TOP_K]], axis=0).astype(jnp.int32)
    rows = ntok * TOP_K + N_EXPERTS * MOE_SUB
    nitems = (ntok * TOP_K) // MOE_ROWS + N_EXPERTS
    pos_flat, tail_start, item_expert, item_start, item_nsub = _routing_tables(top_i, nitems)
    pos_p, pos_s = pos_flat[:mp * TOP_K], pos_flat[mp * TOP_K:]
    xs = _dispatch(hp_p, pos_p, tail_start, rows, 512)
    xs = _dispatch(hp_s, pos_s, tail_start, rows, 512, xs_prev=xs)
    ys = _moe(xs, item_expert, item_start, item_nsub,
              w_gate[l], b_gate[l].reshape(N_EXPERTS, 1, D_FF), w_up[l], b_up[l].reshape(N_EXPERTS, 1, D_FF),
              w_down[l], b_down[l].reshape(N_EXPERTS, 1, D_MODEL))
    fnorm = final_norm.reshape(1, D_MODEL)
    y_p = _combine(x2_p, route_p, fnorm, ys, pos_p, 256).reshape(bp, sp, D_MODEL)
    y_s = _combine(x2_s, route_s, fnorm, ys, pos_s, 256).reshape(bs, ss, D_MODEL)

    def kv_window(proj3, col, past=None):
        new = proj3[:, :, col * LANES:(col + 1) * LANES]
        full = new if past is None else jnp.concatenate([past, new], axis=1)
        win = full[:, -WIN_PAST:]
        return win.reshape(win.shape[0], WIN_PAST, SWA_KV_HEADS, SWA_DH)[None]

    return (y_p, y_s,
            conv_p[None], st_p[None], kv_window(proj_p, COL_KB), kv_window(proj_p, COL_VB),
            mk_p.reshape(bp, MEM_LEN, MEM_HEADS, MEM_DH)[None], mv_p.reshape(bp, MEM_LEN, MEM_HEADS, MEM_DH)[None],
            conv_s[None], st_s[None], kv_window(proj_s, COL_KB, k_past), kv_window(proj_s, COL_VB, v_past))
```

```python
import functools
import math

import numpy as np
import jax
import jax.numpy as jnp
from jax import lax
from jax.experimental import pallas as pl
from jax.experimental.pallas import tpu as pltpu

F32 = jnp.float32
BF16 = jnp.bfloat16
HIGHEST = lax.Precision.HIGHEST

D_MODEL = 2048
CHUNK = 64
GDN_HEADS = 8
GDN_D = 128
CONV_W = 4
SWA_HEADS = 16
SWA_KV_HEADS = 2
SWA_GROUP = SWA_HEADS // SWA_KV_HEADS
SWA_DH = 64
WIN_PAST = 128
REL_BUCKETS = 32
REL_MAX_DIST = 128
MEM_LEN = 256
MEM_HEADS = 4
MEM_DH = 128
N_EXPERTS = 32
TOP_K = 4
D_FF = D_MODEL
SWIGLU_LIMIT = 7.0
SWIGLU_ALPHA = 1.702
EPS = 1e-6

QK_A = GDN_HEADS * GDN_D
V_A = GDN_HEADS * GDN_D
CONV_CH = 2 * QK_A + V_A
Q_B = SWA_HEADS * SWA_DH
KV_B = SWA_KV_HEADS * SWA_DH
MEM_W = MEM_HEADS * MEM_DH

LANES = 128
SUBLANES = 8
VMEM_LIMIT_BYTES = 58 * 1024 * 1024
MOE_VMEM_LIMIT_BYTES = 60 * 1024 * 1024

PROJ_COLS = 5632
PROJ_TN = 512
COL_Z = CONV_CH // V_A
COL_QB = (CONV_CH + V_A) // Q_B
COL_KB = (CONV_CH + V_A + Q_B) // LANES
COL_VB = COL_KB + 1
COL_AB = COL_KB + 2

MOE_SUB = 256
MOE_ROWS = 1280
MOE_STAGE = 128
MOE_OUT_SLOTS = 4
MOE_TF = 512
MOE_NF = D_FF // MOE_TF
HALF = D_MODEL // 2
NORM_ROWS = 256
SWA_CHUNKS_PER_STEP = 4
CROSS_ROWS = 256
GDN_SEQS_PER_STEP = 4
XROW_TILES = HALF // LANES
YROW_TILES = D_MODEL // LANES


def _cparams(sem):
    return pltpu.CompilerParams(dimension_semantics=sem, vmem_limit_bytes=VMEM_LIMIT_BYTES)


def _rms(x, gain):
    return x * lax.rsqrt(jnp.mean(x * x, axis=-1, keepdims=True) + EPS) * gain


def _norm_matmul_kernel(x_ref, g_ref, w_ref, o_ref, h_ref):
    @pl.when(pl.program_id(1) == 0)
    def _():
        def body(r, carry):
            rows = pl.ds(pl.multiple_of(r * NORM_ROWS, NORM_ROWS), NORM_ROWS)
            h_ref[rows, :] = _rms(x_ref[rows, :], g_ref[...]).astype(BF16)
            return carry

        lax.fori_loop(0, x_ref.shape[0] // NORM_ROWS, body, 0)

    o_ref[...] = jnp.dot(h_ref[...], w_ref[...], preferred_element_type=F32)


def _norm_matmul(x, gain, w, tm, tn):
    m, k = x.shape
    n = w.shape[1]
    return pl.pallas_call(
        _norm_matmul_kernel,
        out_shape=jax.ShapeDtypeStruct((m, n), F32),
        grid=(m // tm, n // tn),
        in_specs=[pl.BlockSpec((tm, k), lambda i, j: (i, 0)),
                  pl.BlockSpec((1, k), lambda i, j: (0, 0)),
                  pl.BlockSpec((k, tn), lambda i, j: (0, j))],
        out_specs=pl.BlockSpec((tm, tn), lambda i, j: (i, j)),
        scratch_shapes=[pltpu.VMEM((tm, k), BF16)],
        compiler_params=_cparams(("arbitrary", "arbitrary")),
        name="norm_matmul",
    )(x, gain, w)


def _tri_inverse(lows, c):
    row = lax.broadcasted_iota(jnp.int32, (c, c), 0)
    col = lax.broadcasted_iota(jnp.int32, (c, c), 1)
    eye = jnp.where(row == col, 1.0, 0.0).astype(F32)
    ps = [eye - low for low in lows]
    ms = list(lows)
    span = 1
    while 2 * span < c:
        mbs = [m.astype(BF16) for m in ms]
        ms = [jnp.dot(mb, mb, preferred_element_type=F32) for mb in mbs]
        ps = [p + jnp.dot(p.astype(BF16), m.astype(BF16), preferred_element_type=F32) for p, m in zip(ps, ms)]
        span *= 2
    return ps


def _gdn_kernel(qkv_ref, z_ref, ab_ref, hist_ref, s0_ref, convw_ref, gp_ref, norm_ref,
                o_ref, convnew_ref, s_ref, ext_ref, *, c):
    step = pl.program_id(1)
    nb = qkv_ref.shape[0]

    @pl.when(step == 0)
    def _():
        ext_ref[:, 0:SUBLANES, :] = hist_ref[...]
        s_ref[...] = s0_ref[...]

    row = lax.broadcasted_iota(jnp.int32, (c, c), 0)
    col = lax.broadcasted_iota(jnp.int32, (c, c), 1)
    incl = row >= col
    strict = row > col
    tril = jnp.where(incl, 1.0, 0.0).astype(F32)
    gp = gp_ref[...]
    scale = GDN_D ** -0.5
    heads = [(bi, h) for bi in range(nb) for h in range(GDN_HEADS)]
    s_old = [s_ref[bi, h] for bi, h in heads]

    acts, gc_alls, gc_ts, beta_alls = [], [], [], []
    for bi in range(nb):
        ext_ref[bi, SUBLANES:SUBLANES + c, :] = qkv_ref[bi]
        base = SUBLANES - (CONV_W - 1)
        conv = ext_ref[bi, base:base + c, :] * convw_ref[0:1, :]
        for j in range(1, CONV_W):
            conv = conv + ext_ref[bi, base + j:base + j + c, :] * convw_ref[j:j + 1, :]
        tail = ext_ref[bi, c:c + SUBLANES, :]
        convnew_ref[bi] = tail
        ext_ref[bi, 0:SUBLANES, :] = tail
        acts.append(conv * jax.nn.sigmoid(conv))
        ab = ab_ref[bi]
        g = -jnp.exp(gp[0:1, :]) * jax.nn.softplus(ab + gp[1:2, :])
        beta_alls.append(jax.nn.sigmoid(ab))
        gc_all = jnp.dot(tril, g, precision=HIGHEST, preferred_element_type=F32)
        gc_pad = jnp.concatenate([gc_all, jnp.zeros((LANES - c, LANES), F32)], axis=0) if c < LANES else gc_all
        gc_alls.append(gc_all)
        gc_ts.append(gc_pad.T)

    q, k, v = [], [], []
    for bi, h in heads:
        lo = h * GDN_D
        qh = acts[bi][:, lo:lo + GDN_D]
        kh = acts[bi][:, QK_A + lo:QK_A + lo + GDN_D]
        q.append(qh * lax.rsqrt(jnp.sum(qh * qh, axis=-1, keepdims=True) + EPS) * scale)
        k.append(kh * lax.rsqrt(jnp.sum(kh * kh, axis=-1, keepdims=True) + EPS))
        v.append(acts[bi][:, 2 * QK_A + lo:2 * QK_A + lo + GDN_D])
    beta = [beta_alls[bi][:, GDN_HEADS + h:GDN_HEADS + h + 1] for bi, h in heads]
    gc = [gc_alls[bi][:, h:h + 1] for bi, h in heads]
    gc_last = [gc_alls[bi][c - 1:c, h:h + 1] for bi, h in heads]
    heads = range(len(heads))
    decay = [jnp.exp(jnp.where(incl, gc[h] - gc_ts[h // GDN_HEADS][h % GDN_HEADS:h % GDN_HEADS + 1, 0:c],
                               -jnp.inf)) for h in heads]
    egc = [jnp.exp(gc[h]) for h in heads]
    kb = [k[h] * beta[h] for h in heads]
    qk = [lax.dot_general(jnp.concatenate([q[h], kb[h]], axis=0).astype(BF16), k[h].astype(BF16),
                          (((1,), (1,)), ((), ())), preferred_element_type=F32) for h in heads]
    a_qk = [jnp.where(incl, qk[h][0:c] * decay[h], 0.0) for h in heads]
    tinv = _tri_inverse([jnp.where(strict, qk[h][c:2 * c] * decay[h], 0.0) for h in heads], c)
    uw = [jnp.dot(tinv[h].astype(BF16),
                  jnp.concatenate([v[h] * beta[h], kb[h] * egc[h]], axis=1).astype(BF16),
                  preferred_element_type=F32) for h in heads]
    ws_qs = [jnp.dot(jnp.concatenate([uw[h][:, GDN_D:2 * GDN_D], q[h] * egc[h]], axis=0).astype(BF16),
                     s_old[h].astype(BF16), preferred_element_type=F32) for h in heads]
    v_new = [(uw[h][:, 0:GDN_D] - ws_qs[h][0:c]).astype(BF16) for h in heads]
    o = [ws_qs[h][c:2 * c] + jnp.dot(a_qk[h].astype(BF16), v_new[h], preferred_element_type=F32)
         for h in heads]
    s_new = [s_old[h] * jnp.exp(gc_last[h]) + lax.dot_general(
        (k[h] * jnp.exp(gc_last[h] - gc[h])).astype(BF16), v_new[h], (((0,), (0,)), ((), ())),
        preferred_element_type=F32) for h in heads]
    for bi in range(nb):
        z_all = z_ref[bi]
        gate = z_all * jax.nn.sigmoid(z_all)
        o_ref[bi] = jnp.concatenate([_rms(o[bi * GDN_HEADS + h], norm_ref[...]) for h in range(GDN_HEADS)],
                                    axis=1) * gate
    for h in heads:
        s_ref[h // GDN_HEADS, h % GDN_HEADS] = s_new[h]


def _gdn(proj3, hist8, s0, conv_w, gate_par, gdn_norm, c, nb):
    b, t, _ = proj3.shape
    nchunks = t // c
    kern = functools.partial(_gdn_kernel, c=c)
    return pl.pallas_call(
        kern,
        out_shape=(jax.ShapeDtypeStruct((b, t, V_A), F32),
                   jax.ShapeDtypeStruct((b, SUBLANES, CONV_CH), F32),
                   jax.ShapeDtypeStruct((b, GDN_HEADS, GDN_D, GDN_D), F32)),
        grid=(b // nb, nchunks),
        in_specs=[pl.BlockSpec((nb, c, CONV_CH), lambda i, j: (i, j, 0)),
                  pl.BlockSpec((nb, c, V_A), lambda i, j: (i, j, COL_Z)),
                  pl.BlockSpec((nb, c, LANES), lambda i, j: (i, j, COL_AB)),
                  pl.BlockSpec((nb, SUBLANES, CONV_CH), lambda i, j: (i, 0, 0)),
                  pl.BlockSpec((nb, GDN_HEADS, GDN_D, GDN_D), lambda i, j: (i, 0, 0, 0)),
                  pl.BlockSpec((CONV_W, CONV_CH), lambda i, j: (0, 0)),
                  pl.BlockSpec((SUBLANES, LANES), lambda i, j: (0, 0)),
                  pl.BlockSpec((1, GDN_D), lambda i, j: (0, 0))],
        out_specs=(pl.BlockSpec((nb, c, V_A), lambda i, j: (i, j, 0)),
                   pl.BlockSpec((nb, SUBLANES, CONV_CH), lambda i, j: (i, 0, 0)),
                   pl.BlockSpec((nb, GDN_HEADS, GDN_D, GDN_D), lambda i, j: (i, 0, 0, 0))),
        scratch_shapes=[pltpu.VMEM((nb, SUBLANES + c, CONV_CH), F32)],
        compiler_params=_cparams(("arbitrary", "arbitrary")),
        name="gdn_mixer",
    )(proj3, proj3, proj3, hist8, s0, conv_w, gate_par, gdn_norm)


def _swa_kernel(*refs, nq, nsub, piece_rows, masked):
    npieces = len(piece_rows)
    q_ref = refs[0]
    k_refs = refs[1:1 + npieces]
    v_refs = refs[1 + npieces:1 + 2 * npieces]
    bias_ref, sink_ref, o_ref = refs[1 + 2 * npieces:]
    nk = WIN_PAST + nq
    kcat = jnp.concatenate([r[...] for r in k_refs], axis=0).astype(BF16)
    vcat = jnp.concatenate([r[...] for r in v_refs], axis=0).astype(BF16)
    q = q_ref[...]
    kidx = lax.broadcasted_iota(jnp.int32, (SWA_GROUP * nq, nk), 1)
    blocks = [(i, kv) for i in range(nsub) for kv in range(SWA_KV_HEADS)]
    scores = []
    for i, kv in blocks:
        qs = jnp.concatenate(
            [q[i * nq:(i + 1) * nq, (kv * SWA_GROUP + g) * SWA_DH:(kv * SWA_GROUP + g + 1) * SWA_DH]
             for g in range(SWA_GROUP)], axis=0).astype(BF16)
        kh = kcat[i * nq:i * nq + nk, kv * SWA_DH:(kv + 1) * SWA_DH]
        s = lax.dot_general(qs, kh, (((1,), (1,)), ((), ())), preferred_element_type=F32)
        s = s * SWA_DH ** -0.5 + bias_ref[kv]
        if masked:
            first = (pl.program_id(1) * nsub + i) * nq - WIN_PAST
            s = jnp.where(kidx + first >= 0, s, -jnp.inf)
        scores.append(s)
    sinks = [sink_ref[kv] for _, kv in blocks]
    maxes = [jnp.maximum(jnp.max(s, axis=-1, keepdims=True), sk) for s, sk in zip(scores, sinks)]
    probs = [jnp.exp(s - m) for s, m in zip(scores, maxes)]
    dens = [jnp.sum(p, axis=-1, keepdims=True) + jnp.exp(sk - m) for p, sk, m in zip(probs, sinks, maxes)]
    outs = [jnp.dot((p / den).astype(BF16), vcat[i * nq:i * nq + nk, kv * SWA_DH:(kv + 1) * SWA_DH],
                    preferred_element_type=F32) for (i, kv), p, den in zip(blocks, probs, dens)]
    out_rows = []
    for i in range(nsub):
        heads = []
        for kv in range(SWA_KV_HEADS):
            o = outs[i * SWA_KV_HEADS + kv]
            heads += [o[g * nq:(g + 1) * nq, :] for g in range(SWA_GROUP)]
        out_rows.append(jnp.concatenate(heads, axis=1))
    o_ref[...] = out_rows[0] if nsub == 1 else jnp.concatenate(out_rows, axis=0)


def _rel_bucket(nq, nk):
    rel = jnp.arange(nk)[None, :] - WIN_PAST - jnp.arange(nq)[:, None]
    nb = REL_BUCKETS // 2
    max_exact = nb // 2
    n = jnp.abs(rel)
    large = max_exact + (jnp.log(jnp.maximum(n, 1).astype(F32) / max_exact)
                         / math.log(REL_MAX_DIST / max_exact) * (nb - max_exact)).astype(jnp.int32)
    large = jnp.minimum(large, nb - 1)
    return jnp.where(rel > 0, nb, 0) + jnp.where(n < max_exact, n, large)


def _swa_tables(table, sinks, nq, nk):
    onehot = (_rel_bucket(nq, nk)[:, :, None] == jnp.arange(REL_BUCKETS)[None, None, :]).astype(F32)
    bias = jnp.einsum("qkb,bh->qkh", onehot, table.astype(F32), precision=HIGHEST)
    bias = jnp.transpose(bias, (2, 0, 1)).astype(F32)
    bias = bias.reshape(SWA_KV_HEADS, SWA_GROUP * nq, nk)
    sk = jnp.broadcast_to(sinks.astype(F32).reshape(SWA_KV_HEADS, SWA_GROUP, 1, 1),
                          (SWA_KV_HEADS, SWA_GROUP, nq, 1)).reshape(SWA_KV_HEADS, SWA_GROUP * nq, 1)
    return bias, sk


def _swa_prompt(proj3, table, sinks):
    b, t, _ = proj3.shape
    nq = CHUNK
    nsub = SWA_CHUNKS_PER_STEP
    rows = nsub * nq
    assert rows % WIN_PAST == 0 and t % rows == 0
    past_per_step = rows // WIN_PAST
    bias, sk = _swa_tables(table, sinks, nq, WIN_PAST + nq)
    kern = functools.partial(_swa_kernel, nq=nq, nsub=nsub, piece_rows=(WIN_PAST, rows), masked=True)

    def past_spec(colblk):
        return pl.BlockSpec((None, WIN_PAST, LANES),
                            lambda i, j: (i, jnp.maximum(j * past_per_step - 1, 0), colblk))

    def cur_spec(colblk):
        return pl.BlockSpec((None, rows, LANES), lambda i, j: (i, j, colblk))

    in_specs = [pl.BlockSpec((None, rows, Q_B), lambda i, j: (i, j, COL_QB)),
                past_spec(COL_KB), cur_spec(COL_KB), past_spec(COL_VB), cur_spec(COL_VB),
                pl.BlockSpec(bias.shape, lambda i, j: (0, 0, 0)),
                pl.BlockSpec(sk.shape, lambda i, j: (0, 0, 0))]
    return pl.pallas_call(
        kern,
        out_shape=jax.ShapeDtypeStruct((b, t, Q_B), F32),
        grid=(b, t // rows),
        in_specs=in_specs,
        out_specs=pl.BlockSpec((None, rows, Q_B), lambda i, j: (i, j, 0)),
        compiler_params=_cparams(("arbitrary", "arbitrary")),
        name="swa_prompt",
    )(proj3, proj3, proj3, proj3, proj3, bias, sk)


def _swa_sample(proj3, k_past, v_past, table, sinks):
    b, t, _ = proj3.shape
    bias, sk = _swa_tables(table, sinks, t, WIN_PAST + t)
    kern = functools.partial(_swa_kernel, nq=t, nsub=1, piece_rows=(WIN_PAST, t), masked=False)
    in_specs = [pl.BlockSpec((None, t, Q_B), lambda i, j: (i, 0, COL_QB)),
                pl.BlockSpec((None, WIN_PAST, LANES), lambda i, j: (i, 0, 0)),
                pl.BlockSpec((None, t, LANES), lambda i, j: (i, 0, COL_KB)),
                pl.BlockSpec((None, WIN_PAST, LANES), lambda i, j: (i, 0, 0)),
                pl.BlockSpec((None, t, LANES), lambda i, j: (i, 0, COL_VB)),
                pl.BlockSpec(bias.shape, lambda i, j: (0, 0, 0)),
                pl.BlockSpec(sk.shape, lambda i, j: (0, 0, 0))]
    return pl.pallas_call(
        kern,
        out_shape=jax.ShapeDtypeStruct((b, t, Q_B), F32),
        grid=(b, 1),
        in_specs=in_specs,
        out_specs=pl.BlockSpec((None, t, Q_B), lambda i, j: (i, 0, 0)),
        compiler_params=_cparams(("arbitrary", "arbitrary")),
        name="swa_sample",
    )(proj3, k_past, proj3, v_past, proj3, bias, sk)


def _outproj_kernel(x_ref, oa_ref, ob_ref, wa_ref, wb_ref, g_ref, wq_ref, x1_ref, q_ref):
    x1 = (x_ref[...]
          + jnp.dot(oa_ref[...].astype(BF16), wa_ref[...], preferred_element_type=F32)
          + jnp.dot(ob_ref[...].astype(BF16), wb_ref[...], preferred_element_type=F32))
    x1_ref[...] = x1
    hc = _rms(x1, g_ref[...]).astype(BF16)
    q_ref[...] = jnp.dot(hc, wq_ref[...], preferred_element_type=F32)


def _outproj(x, o_a, o_b, w_out, norm_cross, w_mq, tm):
    m = x.shape[0]
    return pl.pallas_call(
        _outproj_kernel,
        out_shape=(jax.ShapeDtypeStruct((m, D_MODEL), F32), jax.ShapeDtypeStruct((m, MEM_W), F32)),
        grid=(m // tm,),
        in_specs=[pl.BlockSpec((tm, D_MODEL), lambda i: (i, 0)),
                  pl.BlockSpec((tm, V_A), lambda i: (i, 0)),
                  pl.BlockSpec((tm, Q_B), lambda i: (i, 0)),
                  pl.BlockSpec((V_A, D_MODEL), lambda i: (0, 0)),
                  pl.BlockSpec((Q_B, D_MODEL), lambda i: (1, 0)),
                  pl.BlockSpec((1, D_MODEL), lambda i: (0, 0)),
                  pl.BlockSpec((D_MODEL, MEM_W), lambda i: (0, 0))],
        out_specs=(pl.BlockSpec((tm, D_MODEL), lambda i: (i, 0)),
                   pl.BlockSpec((tm, MEM_W), lambda i: (i, 0))),
        compiler_params=_cparams(("arbitrary",)),
        name="outproj",
    )(x, o_a, o_b, w_out, w_out, norm_cross, w_mq)


def _cross_router_kernel(x1_ref, q_ref, mk_ref, mv_ref, wo_ref, g_ref, wr_ref, br_ref,
                         x2_ref, hp_ref, route_ref):
    nb, tm = x1_ref.shape[0], x1_ref.shape[1]
    rows = []
    for bi in range(nb):
        q = q_ref[bi]
        outs = []
        for h in range(MEM_HEADS):
            sl = slice(h * MEM_DH, (h + 1) * MEM_DH)
            mk = mk_ref[bi, pl.ds(h, MEM_LEN, stride=MEM_HEADS), :].astype(BF16)
            mv = mv_ref[bi, pl.ds(h, MEM_LEN, stride=MEM_HEADS), :].astype(BF16)
            s = lax.dot_general(q[:, sl].astype(BF16), mk, (((1,), (1,)), ((), ())),
                                preferred_element_type=F32) * MEM_DH ** -0.5
            m = jnp.max(s, axis=-1, keepdims=True)
            p = jnp.exp(s - m)
            p = (p / jnp.sum(p, axis=-1, keepdims=True)).astype(BF16)
            outs.append(jnp.dot(p, mv, preferred_element_type=F32))
        rows.append(jnp.concatenate(outs, axis=1))
    att = (rows[0] if nb == 1 else jnp.concatenate(rows, axis=0)).astype(BF16)
    x2 = x1_ref[...].reshape(nb * tm, D_MODEL) + jnp.dot(att, wo_ref[...], preferred_element_type=F32)
    x2_ref[...] = x2.reshape(nb, tm, D_MODEL)
    hf = _rms(x2, g_ref[...])
    lo = pltpu.bitcast(hf[:, 0:HALF].astype(BF16).astype(F32), jnp.uint32)
    hi = pltpu.bitcast(hf[:, HALF:D_MODEL].astype(BF16).astype(F32), jnp.uint32)
    packed = (lo >> 16) | (hi & jnp.uint32(0xFFFF0000))
    for bi in range(nb):
        for j in range(XROW_TILES):
            hp_ref[bi, pl.ds(j, tm, stride=XROW_TILES), :] = packed[bi * tm:(bi + 1) * tm, j * LANES:(j + 1) * LANES]
    logits = jnp.dot(hf, wr_ref[...], precision=HIGHEST, preferred_element_type=F32) + br_ref[...]
    lane = lax.broadcasted_iota(jnp.int32, logits.shape, 1)
    lanef = lane.astype(F32)
    l = jnp.where(lane < N_EXPERTS, logits, -jnp.inf)
    vals, idxs = [], []
    for _ in range(TOP_K):
        m = jnp.max(l, axis=-1, keepdims=True)
        idx = jnp.min(jnp.where(l == m, lanef, float(LANES)), axis=-1, keepdims=True)
        vals.append(m)
        idxs.append(idx)
        l = jnp.where(lanef == idx, -jnp.inf, l)
    es = [jnp.exp(v - vals[0]) for v in vals]
    den = es[0] + es[1] + es[2] + es[3]
    route = jnp.zeros(logits.shape, F32)
    for k in range(TOP_K):
        route = jnp.where(lane == k, idxs[k], route)
        route = jnp.where(lane == TOP_K + k, es[k] / den, route)
    route_ref[...] = route.reshape(nb, tm, LANES)


def _cross_router(x1, qm, mk, mv, w_mo, norm_ffn, w_router, b_router, nb, tm):
    b, t, _ = x1.shape
    nt = t // tm
    return pl.pallas_call(
        _cross_router_kernel,
        out_shape=(jax.ShapeDtypeStruct((b, t, D_MODEL), F32),
                   jax.ShapeDtypeStruct((b, t * XROW_TILES, LANES), jnp.uint32),
                   jax.ShapeDtypeStruct((b, t, LANES), F32)),
        grid=(b // nb, nt),
        in_specs=[pl.BlockSpec((nb, tm, D_MODEL), lambda i, j: (i, j, 0)),
                  pl.BlockSpec((nb, tm, MEM_W), lambda i, j: (i, j, 0)),
                  pl.BlockSpec((nb, MEM_LEN * MEM_HEADS, MEM_DH), lambda i, j: (i, 0, 0)),
                  pl.BlockSpec((nb, MEM_LEN * MEM_HEADS, MEM_DH), lambda i, j: (i, 0, 0)),
                  pl.BlockSpec((MEM_W, D_MODEL), lambda i, j: (0, 0)),
                  pl.BlockSpec((1, D_MODEL), lambda i, j: (0, 0)),
                  pl.BlockSpec((D_MODEL, LANES), lambda i, j: (0, 0)),
                  pl.BlockSpec((1, LANES), lambda i, j: (0, 0))],
        out_specs=(pl.BlockSpec((nb, tm, D_MODEL), lambda i, j: (i, j, 0)),
                   pl.BlockSpec((nb, tm * XROW_TILES, LANES), lambda i, j: (i, j, 0)),
                   pl.BlockSpec((nb, tm, LANES), lambda i, j: (i, j, 0))),
        compiler_params=_cparams(("arbitrary", "arbitrary")),
        name="cross_router",
    )(x1, qm, mk, mv, w_mo, norm_ffn, w_router, b_router)


def _row_tile(idx, tiles):
    return pl.ds(pl.multiple_of(idx * tiles, tiles), tiles)


def _dispatch_kernel(*refs, tokens, first_group):
    if first_group:
        pos_ref, tail_ref, hp_ref, xs_hbm, zero_ref, sem = refs
    else:
        pos_ref, tail_ref, hp_ref, _, xs_hbm, zero_ref, sem = refs

    def zero_tails():
        zero_ref[...] = jnp.zeros(zero_ref.shape, zero_ref.dtype)

        def tail_copy(e):
            dst = xs_hbm.at[pl.ds(pl.multiple_of(tail_ref[e] * XROW_TILES, XROW_TILES), MOE_SUB * XROW_TILES)]
            return pltpu.make_async_copy(zero_ref, dst, sem.at[0])

        for e in range(N_EXPERTS):
            pl.when(tail_ref[e] >= 0)(lambda e=e: tail_copy(e).start())
        for e in range(N_EXPERTS):
            pl.when(tail_ref[e] >= 0)(lambda e=e: tail_copy(e).wait())

    if first_group:
        pl.when(pl.program_id(0) == 0)(zero_tails)

    def issue(t, carry):
        src = hp_ref.at[_row_tile(t, XROW_TILES)]
        for k in range(TOP_K):
            dst = xs_hbm.at[_row_tile(pos_ref[t * TOP_K + k], XROW_TILES)]
            pltpu.make_async_copy(src, dst, sem.at[1]).start()
        return carry

    lax.fori_loop(0, tokens, issue, 0, unroll=4)

    def drain(t, carry):
        for k in range(TOP_K):
            pltpu.make_async_copy(hp_ref.at[_row_tile(0, XROW_TILES)], xs_hbm.at[_row_tile(0, XROW_TILES)],
                                  sem.at[1]).wait()
        return carry

    lax.fori_loop(0, tokens, drain, 0)


def _dispatch(hp, pos_flat, tail_start, rows, tokens, xs_prev=None):
    t = hp.shape[0] // XROW_TILES
    first_group = xs_prev is None
    kern = functools.partial(_dispatch_kernel, tokens=tokens, first_group=first_group)
    in_specs = [pl.BlockSpec((tokens * TOP_K,), lambda i: (i,), memory_space=pltpu.SMEM),
                pl.BlockSpec((N_EXPERTS,), lambda i: (0,), memory_space=pltpu.SMEM),
                pl.BlockSpec((tokens * XROW_TILES, LANES), lambda i: (i, 0))]
    args = [pos_flat, tail_start, hp]
    if not first_group:
        in_specs.append(pl.BlockSpec(memory_space=pl.ANY))
        args.append(xs_prev)
    return pl.pallas_call(
        kern,
        out_shape=jax.ShapeDtypeStruct((rows * XROW_TILES, LANES), jnp.uint32),
        grid=(t // tokens,),
        in_specs=in_specs,
        out_specs=pl.BlockSpec(memory_space=pl.ANY),
        scratch_shapes=[pltpu.VMEM((MOE_SUB * XROW_TILES, LANES), jnp.uint32),
                        pltpu.SemaphoreType.DMA((2,))],
        input_output_aliases={} if first_group else {3: 0},
        compiler_params=pltpu.CompilerParams(dimension_semantics=("arbitrary",),
                                             vmem_limit_bytes=VMEM_LIMIT_BYTES, has_side_effects=True),
        name="moe_dispatch",
    )(*args)


def _moe_kernel(ie_ref, is_ref, ns_ref, xs_hbm, wg_ref, bg_ref, wu_ref, bu_ref, wd_ref, bd_ref,
                ys_hbm, xu_ref, acc_ref, ystage_ref, in_sem, out_sem):
    w = pl.program_id(0)
    f = pl.program_id(1)
    nsub = ns_ref[w]
    start = is_ref[w]

    def sub_rows(s):
        return pl.ds(pl.multiple_of(s * MOE_SUB, MOE_SUB), MOE_SUB)

    nstage = nsub * (MOE_SUB // MOE_STAGE)

    def stage_rows(s):
        return pl.ds(pl.multiple_of(s * MOE_STAGE, MOE_STAGE), MOE_STAGE)

    def hbm_rows(first, s, tiles):
        return pl.ds(pl.multiple_of((first + s * MOE_STAGE) * tiles, MOE_STAGE * tiles), MOE_STAGE * tiles)

    def load(first, s):
        dst = xu_ref.at[pl.ds(pl.multiple_of(s * MOE_STAGE * XROW_TILES, MOE_STAGE * XROW_TILES),
                              MOE_STAGE * XROW_TILES)]
        return pltpu.make_async_copy(xs_hbm.at[hbm_rows(first, s, XROW_TILES)], dst, in_sem.at[0])

    def fetch_item(first, stages):
        def issue(s, carry):
            load(first, s).start()
            return carry

        lax.fori_loop(0, stages, issue, 0)

    pl.when(jnp.logical_and(jnp.logical_and(w == 0, f == 0), nsub > 0))(lambda: fetch_item(start, nstage))

    @pl.when(jnp.logical_and(f == 0, nsub > 0))
    def _():
        def landed(s, carry):
            load(start, s).wait()
            return carry

        lax.fori_loop(0, nstage, landed, 0)

        def init(s, carry):
            acc_ref[sub_rows(s), :] = jnp.broadcast_to(bd_ref[...], (MOE_SUB, D_MODEL))
            return carry

        lax.fori_loop(0, nsub, init, 0)

    def x_tile(s):
        base = pl.multiple_of(s * MOE_SUB * XROW_TILES, MOE_SUB * XROW_TILES)
        lows, highs = [], []
        for j in range(XROW_TILES):
            u = xu_ref[pl.ds(base + j, MOE_SUB, stride=XROW_TILES), :]
            lows.append(pltpu.bitcast(u << 16, F32).astype(BF16))
            highs.append(pltpu.bitcast(u & jnp.uint32(0xFFFF0000), F32).astype(BF16))
        return jnp.concatenate(lows + highs, axis=1)

    @pl.when(nsub > 0)
    def _():
        wg = wg_ref[...].astype(BF16)
        wu = wu_ref[...].astype(BF16)
        wd = wd_ref[...].astype(BF16)
        bg = bg_ref[...]
        bu = bu_ref[...]

        def expert_out(s):
            x = x_tile(s)
            gt = jnp.minimum(jnp.dot(x, wg, preferred_element_type=F32) + bg, SWIGLU_LIMIT)
            up = jnp.clip(jnp.dot(x, wu, preferred_element_type=F32) + bu, -SWIGLU_LIMIT, SWIGLU_LIMIT)
            a = gt * jax.nn.sigmoid(SWIGLU_ALPHA * gt) * (up + 1.0)
            return jnp.dot(a.astype(BF16), wd, preferred_element_type=F32)

        def body(s, carry):
            acc_ref[sub_rows(s), :] += expert_out(s)
            return carry

        lax.fori_loop(0, nsub, body, 0)

    @pl.when(jnp.logical_and(f == MOE_NF - 1, nsub > 0))
    def _():
        nxt = jnp.minimum(w + 1, pl.num_programs(0) - 1)
        nxt_stages = jnp.where(w + 1 < pl.num_programs(0), ns_ref[nxt], 0) * (MOE_SUB // MOE_STAGE)
        fetch_item(is_ref[nxt], nxt_stages)

        def store(s):
            slot = s % MOE_OUT_SLOTS
            return pltpu.make_async_copy(ystage_ref.at[slot], ys_hbm.at[hbm_rows(start, s, YROW_TILES)],
                                         out_sem.at[slot])

        def write(s, carry):
            pl.when(s >= MOE_OUT_SLOTS)(lambda: store(s - MOE_OUT_SLOTS).wait())
            for j in range(YROW_TILES):
                ystage_ref[s % MOE_OUT_SLOTS, pl.ds(j, MOE_STAGE, stride=YROW_TILES), :] = \
                    acc_ref[stage_rows(s), j * LANES:(j + 1) * LANES]
            store(s).start()
            return carry

        lax.fori_loop(0, nstage, write, 0)
        for back in range(1, MOE_OUT_SLOTS + 1):
            pl.when(nstage >= back)(lambda back=back: store(nstage - back).wait())


def _moe(xs, item_expert, item_start, item_nsub, w_gate, b_gate, w_up, b_up, w_down, b_down):
    rows = xs.shape[0] // XROW_TILES
    nitems = item_expert.shape[0]

    def fcol(w, f, ie, st, ns):
        return jnp.where(ns[w] > 0, f, MOE_NF - 1)

    grid_spec = pltpu.PrefetchScalarGridSpec(
        num_scalar_prefetch=3,
        grid=(nitems, MOE_NF),
        in_specs=[pl.BlockSpec(memory_space=pl.ANY),
                  pl.BlockSpec((None, D_MODEL, MOE_TF), lambda w, f, ie, st, ns: (ie[w], 0, fcol(w, f, ie, st, ns))),
                  pl.BlockSpec((None, 1, MOE_TF), lambda w, f, ie, st, ns: (ie[w], 0, fcol(w, f, ie, st, ns))),
                  pl.BlockSpec((None, D_MODEL, MOE_TF), lambda w, f, ie, st, ns: (ie[w], 0, fcol(w, f, ie, st, ns))),
                  pl.BlockSpec((None, 1, MOE_TF), lambda w, f, ie, st, ns: (ie[w], 0, fcol(w, f, ie, st, ns))),
                  pl.BlockSpec((None, MOE_TF, D_MODEL), lambda w, f, ie, st, ns: (ie[w], fcol(w, f, ie, st, ns), 0)),
                  pl.BlockSpec((None, 1, D_MODEL), lambda w, f, ie, st, ns: (ie[w], 0, 0))],
        out_specs=pl.BlockSpec(memory_space=pl.ANY),
        scratch_shapes=[pltpu.VMEM((MOE_ROWS * XROW_TILES, LANES), jnp.uint32),
                        pltpu.VMEM((MOE_ROWS, D_MODEL), F32),
                        pltpu.VMEM((MOE_OUT_SLOTS, MOE_STAGE * YROW_TILES, LANES), F32),
                        pltpu.SemaphoreType.DMA((1,)),
                        pltpu.SemaphoreType.DMA((MOE_OUT_SLOTS,))])
    return pl.pallas_call(
        _moe_kernel,
        out_shape=jax.ShapeDtypeStruct((rows * YROW_TILES, LANES), F32),
        grid_spec=grid_spec,
        compiler_params=pltpu.CompilerParams(dimension_semantics=("arbitrary", "arbitrary"),
                                             vmem_limit_bytes=MOE_VMEM_LIMIT_BYTES, has_side_effects=True),
        name="moe_experts",
    )(item_expert, item_start, item_nsub, xs, w_gate, b_gate, w_up, b_up, w_down, b_down)


def _combine_kernel(pos_ref, nxt_ref, x2_ref, route_ref, g_ref, ys_hbm, o_ref, buf_ref, sem, *, tokens):
    i = pl.program_id(0)
    slot = i % 2

    def gather(p_ref, dst_slot):
        def issue(t, carry):
            for k in range(TOP_K):
                pltpu.make_async_copy(ys_hbm.at[_row_tile(p_ref[t * TOP_K + k], YROW_TILES)],
                                      buf_ref.at[dst_slot, k, _row_tile(t, YROW_TILES)], sem.at[dst_slot]).start()
            return carry

        lax.fori_loop(0, tokens, issue, 0, unroll=4)

    pl.when(i == 0)(lambda: gather(pos_ref, 0))
    pl.when(i + 1 < pl.num_programs(0))(lambda: gather(nxt_ref, 1 - slot))

    def drain(t, carry):
        for k in range(TOP_K):
            pltpu.make_async_copy(ys_hbm.at[_row_tile(0, YROW_TILES)],
                                  buf_ref.at[slot, k, _row_tile(0, YROW_TILES)], sem.at[slot]).wait()
        return carry

    lax.fori_loop(0, tokens, drain, 0)

    route = route_ref[...]
    gates = [route[:, TOP_K + k:TOP_K + k + 1] for k in range(TOP_K)]
    cols = []
    for j in range(YROW_TILES):
        y = x2_ref[:, j * LANES:(j + 1) * LANES]
        for k in range(TOP_K):
            y = y + gates[k] * buf_ref[slot, k, pl.ds(j, tokens, stride=YROW_TILES), :]
        cols.append(y)
    o_ref[...] = _rms(jnp.concatenate(cols, axis=1), g_ref[...])


def _combine(x2, route, final_norm, ys, pos_flat, tokens):
    t = x2.shape[0]
    nsteps = t // tokens
    kern = functools.partial(_combine_kernel, tokens=tokens)
    return pl.pallas_call(
        kern,
        out_shape=jax.ShapeDtypeStruct((t, D_MODEL), F32),
        grid=(nsteps,),
        in_specs=[pl.BlockSpec((tokens * TOP_K,), lambda i: (i,), memory_space=pltpu.SMEM),
                  pl.BlockSpec((tokens * TOP_K,), lambda i: (jnp.minimum(i + 1, nsteps - 1),),
                               memory_space=pltpu.SMEM),
                  pl.BlockSpec((tokens, D_MODEL), lambda i: (i, 0)),
                  pl.BlockSpec((tokens, LANES), lambda i: (i, 0)),
                  pl.BlockSpec((1, D_MODEL), lambda i: (0, 0)),
                  pl.BlockSpec(memory_space=pl.ANY)],
        out_specs=pl.BlockSpec((tokens, D_MODEL), lambda i: (i, 0)),
        scratch_shapes=[pltpu.VMEM((2, TOP_K, tokens * YROW_TILES, LANES), F32),
                        pltpu.SemaphoreType.DMA((2,))],
        compiler_params=_cparams(("arbitrary",)),
        name="moe_combine",
    )(pos_flat, pos_flat, x2, route, final_norm, ys)


def _routing_tables(top_i, nitems):
    t = top_i.shape[0]
    sel = jnp.sum((top_i[:, :, None] == jnp.arange(N_EXPERTS, dtype=jnp.int32)[None, None, :]).astype(jnp.int32),
                  axis=1)
    cnt = jnp.sum(sel, axis=0)
    rank = jnp.cumsum(sel, axis=0) - sel
    cnt_pad = ((cnt + MOE_SUB - 1) // MOE_SUB) * MOE_SUB
    off = jnp.cumsum(cnt_pad) - cnt_pad
    pos = jnp.take_along_axis(off[None, :] + rank, top_i, axis=1)
    tail_start = jnp.where(cnt > 0, off + cnt_pad - MOE_SUB, -1)
    items_per = (cnt + MOE_ROWS - 1) // MOE_ROWS
    item_end = jnp.cumsum(items_per)
    total = item_end[-1]
    widx = jnp.arange(nitems, dtype=jnp.int32)
    e_of = jnp.minimum(jnp.searchsorted(item_end, widx, side="right"), N_EXPERTS - 1).astype(jnp.int32)
    j_of = widx - (item_end - items_per)[e_of]
    valid = widx < total
    e_last = e_of[jnp.maximum(total - 1, 0)]
    item_expert = jnp.where(valid, e_of, e_last).astype(jnp.int32)
    item_start = jnp.where(valid, off[e_of] + j_of * MOE_ROWS, 0).astype(jnp.int32)
    rows_left = cnt_pad[e_of] - j_of * MOE_ROWS
    item_nsub = jnp.where(valid, jnp.minimum(rows_left, MOE_ROWS) // MOE_SUB, 0).astype(jnp.int32)
    return pos.reshape(t * TOP_K).astype(jnp.int32), tail_start.astype(jnp.int32), item_expert, item_start, item_nsub


def _trunk(x, conv_hist, s0, swa_fn, mk, mv, gdn_chunk, p):
    b, t, _ = x.shape
    m = b * t
    xf = x.reshape(m, D_MODEL)
    proj = _norm_matmul(xf, p["norm_mix"], p["w_in"], min(m, 1024), PROJ_TN)
    proj3 = proj.reshape(b, t, PROJ_COLS)
    hist8 = jnp.pad(conv_hist, ((0, 0), (SUBLANES - (CONV_W - 1), 0), (0, 0)))
    o_a, conv8, s_new = _gdn(proj3, hist8, s0, p["conv_w"], p["gate_par"], p["gdn_norm"], gdn_chunk,
                             GDN_SEQS_PER_STEP)
    o_b = swa_fn(proj3)
    x1, qm = _outproj(xf, o_a.reshape(m, V_A), o_b.reshape(m, Q_B), p["w_out"], p["norm_cross"], p["w_mq"], 256)
    tm = min(t, CROSS_ROWS)
    x2, hp, route = _cross_router(x1.reshape(b, t, D_MODEL), qm.reshape(b, t, MEM_W), mk, mv,
                                  p["w_mo"], p["norm_ffn"], p["w_router"], p["b_router"], CROSS_ROWS // tm, tm)
    return proj3, conv8[:, SUBLANES - (CONV_W - 1):], s_new, x2.reshape(m, D_MODEL), \
        hp.reshape(m * XROW_TILES, LANES), route.reshape(m, LANES)


def kernel(x_prompt, x_sample, cache_conv, state_gdn, cache_swa_k, cache_swa_v, cache_mem_k, cache_mem_v, mem_prompt, norm_mix, w_in, conv_w, gdn_a_log, gdn_dt_bias, gdn_norm, swa_sinks, rel_bias_table, w_out, norm_cross, norm_mem, w_mq, w_mk, w_mv, w_mo, norm_ffn, w_router, b_router, w_gate, b_gate, w_up, b_up, w_down, b_down, final_norm):
    depth = norm_mix.shape[0]
    assert depth == 1, "kernel is written for the single-layer trunk"
    bp, sp, _ = x_prompt.shape
    bs, ss, _ = x_sample.shape
    l = 0
    w = w_in[l]
    n_ab = 2 * GDN_HEADS
    w_perm = jnp.concatenate(
        [w[:, :CONV_CH + V_A], w[:, CONV_CH + V_A + n_ab:], w[:, CONV_CH + V_A:CONV_CH + V_A + n_ab],
         jnp.zeros((D_MODEL, PROJ_COLS - w.shape[1]), w.dtype)], axis=1).astype(BF16)
    gate_par = jnp.zeros((SUBLANES, LANES), F32)
    gate_par = gate_par.at[0, :GDN_HEADS].set(gdn_a_log[l]).at[1, :GDN_HEADS].set(gdn_dt_bias[l])
    p = dict(
        norm_mix=norm_mix[l].reshape(1, D_MODEL), w_in=w_perm, conv_w=conv_w[l], gate_par=gate_par,
        gdn_norm=gdn_norm[l].reshape(1, GDN_D), w_out=w_out[l].astype(BF16),
        norm_cross=norm_cross[l].reshape(1, D_MODEL), w_mq=w_mq[l].astype(BF16), w_mo=w_mo[l].astype(BF16),
        norm_ffn=norm_ffn[l].reshape(1, D_MODEL),
        w_router=jnp.pad(w_router[l], ((0, 0), (0, LANES - N_EXPERTS))),
        b_router=jnp.pad(b_router[l], (0, LANES - N_EXPERTS)).reshape(1, LANES))

    w_mkv = jnp.concatenate([w_mk[l], w_mv[l]], axis=1).astype(BF16)
    mkv = _norm_matmul(mem_prompt.reshape(bp * MEM_LEN, D_MODEL), norm_mem[l].reshape(1, D_MODEL), w_mkv,
                       min(bp * MEM_LEN, 1024), PROJ_TN)
    mk_p = mkv[:, :MEM_W].reshape(bp, MEM_LEN * MEM_HEADS, MEM_DH)
    mv_p = mkv[:, MEM_W:].reshape(bp, MEM_LEN * MEM_HEADS, MEM_DH)

    zero_hist = jnp.zeros((bp, CONV_W - 1, CONV_CH), F32)
    zero_state = jnp.zeros((bp, GDN_HEADS, GDN_D, GDN_D), F32)
    swa_p = functools.partial(_swa_prompt, table=rel_bias_table, sinks=swa_sinks[l])
    proj_p, conv_p, st_p, x2_p, hp_p, route_p = _trunk(x_prompt, zero_hist, zero_state, swa_p, mk_p, mv_p, CHUNK, p)

    k_past = cache_swa_k[l].reshape(bs, WIN_PAST, KV_B)
    v_past = cache_swa_v[l].reshape(bs, WIN_PAST, KV_B)
    swa_s = functools.partial(_swa_sample, k_past=k_past, v_past=v_past, table=rel_bias_table, sinks=swa_sinks[l])
    proj_s, conv_s, st_s, x2_s, hp_s, route_s = _trunk(
        x_sample, cache_conv[l], state_gdn[l], swa_s,
        cache_mem_k[l].reshape(bs, MEM_LEN * MEM_HEADS, MEM_DH),
        cache_mem_v[l].reshape(bs, MEM_LEN * MEM_HEADS, MEM_DH), ss, p)

    mp, ms = bp * sp, bs * ss
    ntok = mp + ms
    top_i = jnp.concatenate([route_p[:, :TOP_K], route_s[:, :TOP_K]], axis=0).astype(jnp.int32)
    rows = ntok * TOP_K + N_EXPERTS * MOE_SUB
    nitems = (ntok * TOP_K) // MOE_ROWS + N_EXPERTS
    pos_flat, tail_start, item_expert, item_start, item_nsub = _routing_tables(top_i, nitems)
    pos_p, pos_s = pos_flat[:mp * TOP_K], pos_flat[mp * TOP_K:]
    xs = _dispatch(hp_p, pos_p, tail_start, rows, 512)
    xs = _dispatch(hp_s, pos_s, tail_start, rows, 512, xs_prev=xs)
    ys = _moe(xs, item_expert, item_start, item_nsub,
              w_gate[l], b_gate[l].reshape(N_EXPERTS, 1, D_FF), w_up[l], b_up[l].reshape(N_EXPERTS, 1, D_FF),
              w_down[l], b_down[l].reshape(N_EXPERTS, 1, D_MODEL))
    fnorm = final_norm.reshape(1, D_MODEL)
    y_p = _combine(x2_p, route_p, fnorm, ys, pos_p, 256).reshape(bp, sp, D_MODEL)
    y_s = _combine(x2_s, route_s, fnorm, ys, pos_s, 256).reshape(bs, ss, D_MODEL)

    def kv_window(proj3, col, past=None):
        new = proj3[:, :, col * LANES:(col + 1) * LANES]
        full = new if past is None else jnp.concatenate([past, new], axis=1)
        win = full[:, -WIN_PAST:]
        return win.reshape(win.shape[0], WIN_PAST, SWA_KV_HEADS, SWA_DH)[None]

    return (y_p, y_s,
            conv_p[None], st_p[None], kv_window(proj_p, COL_KB), kv_window(proj_p, COL_VB),
            mk_p.reshape(bp, MEM_LEN, MEM_HEADS, MEM_DH)[None], mv_p.reshape(bp, MEM_LEN, MEM_HEADS, MEM_DH)[None],
            conv_s[None], st_s[None], kv_window(proj_s, COL_KB, k_past), kv_window(proj_s, COL_VB, v_past))
```

```python
import functools
import math

import numpy as np
import jax
import jax.numpy as jnp
from jax import lax
from jax.experimental import pallas as pl
from jax.experimental.pallas import tpu as pltpu

F32 = jnp.float32
BF16 = jnp.bfloat16
HIGHEST = lax.Precision.HIGHEST

D_MODEL = 2048
CHUNK = 64
GDN_HEADS = 8
GDN_D = 128
CONV_W = 4
SWA_HEADS = 16
SWA_KV_HEADS = 2
SWA_GROUP = SWA_HEADS // SWA_KV_HEADS
SWA_DH = 64
WIN_PAST = 128
REL_BUCKETS = 32
REL_MAX_DIST = 128
MEM_LEN = 256
MEM_HEADS = 4
MEM_DH = 128
N_EXPERTS = 32
TOP_K = 4
D_FF = D_MODEL
SWIGLU_LIMIT = 7.0
SWIGLU_ALPHA = 1.702
EPS = 1e-6

QK_A = GDN_HEADS * GDN_D
V_A = GDN_HEADS * GDN_D
CONV_CH = 2 * QK_A + V_A
Q_B = SWA_HEADS * SWA_DH
KV_B = SWA_KV_HEADS * SWA_DH
MEM_W = MEM_HEADS * MEM_DH

LANES = 128
SUBLANES = 8
VMEM_LIMIT_BYTES = 58 * 1024 * 1024
MOE_VMEM_LIMIT_BYTES = 60 * 1024 * 1024

PROJ_COLS = 5632
PROJ_TN = 1408
MEMKV_TN = 512
COL_Z = CONV_CH // V_A
COL_QB = (CONV_CH + V_A) // Q_B
COL_KB = (CONV_CH + V_A + Q_B) // LANES
COL_VB = COL_KB + 1
COL_AB = COL_KB + 2

MOE_SUB = 256
MOE_ROWS = 1536
MOE_STAGE = 128
MOE_OUT_SLOTS = 4
MOE_TF = 512
MOE_NF = D_FF // MOE_TF
HALF = D_MODEL // 2
NORM_ROWS = 256
SWA_CHUNKS_PER_STEP = 4
CROSS_ROWS = 256
GDN_SEQS_PER_STEP = 4
XROW_TILES = HALF // LANES
YROW_TILES = HALF // LANES


def _cparams(sem):
    return pltpu.CompilerParams(dimension_semantics=sem, vmem_limit_bytes=VMEM_LIMIT_BYTES)


def _rms(x, gain):
    return x * lax.rsqrt(jnp.mean(x * x, axis=-1, keepdims=True) + EPS) * gain


def _pack_halves(lo, hi):
    lo_bits = pltpu.bitcast(lo.astype(BF16).astype(F32), jnp.uint32)
    hi_bits = pltpu.bitcast(hi.astype(BF16).astype(F32), jnp.uint32)
    return (lo_bits >> 16) | (hi_bits & jnp.uint32(0xFFFF0000))


def _unpack_halves(u):
    return pltpu.bitcast(u << 16, F32), pltpu.bitcast(u & jnp.uint32(0xFFFF0000), F32)


def _norm_matmul_kernel(x_ref, g_ref, w_ref, o_ref, h_ref):
    @pl.when(pl.program_id(1) == 0)
    def _():
        def body(r, carry):
            rows = pl.ds(pl.multiple_of(r * NORM_ROWS, NORM_ROWS), NORM_ROWS)
            h_ref[rows, :] = _rms(x_ref[rows, :], g_ref[...]).astype(BF16)
            return carry

        lax.fori_loop(0, x_ref.shape[0] // NORM_ROWS, body, 0)

    o_ref[...] = jnp.dot(h_ref[...], w_ref[...], preferred_element_type=F32)


def _norm_matmul(x, gain, w, tm, tn):
    m, k = x.shape
    n = w.shape[1]
    return pl.pallas_call(
        _norm_matmul_kernel,
        out_shape=jax.ShapeDtypeStruct((m, n), F32),
        grid=(m // tm, n // tn),
        in_specs=[pl.BlockSpec((tm, k), lambda i, j: (i, 0)),
                  pl.BlockSpec((1, k), lambda i, j: (0, 0)),
                  pl.BlockSpec((k, tn), lambda i, j: (0, j))],
        out_specs=pl.BlockSpec((tm, tn), lambda i, j: (i, j)),
        scratch_shapes=[pltpu.VMEM((tm, k), BF16)],
        compiler_params=_cparams(("arbitrary", "arbitrary")),
        name="norm_matmul",
    )(x, gain, w)


def _tri_inverse(lows, c):
    row = lax.broadcasted_iota(jnp.int32, (c, c), 0)
    col = lax.broadcasted_iota(jnp.int32, (c, c), 1)
    eye = jnp.where(row == col, 1.0, 0.0).astype(F32)
    ps = [eye - low for low in lows]
    ms = list(lows)
    span = 1
    while 2 * span < c:
        mbs = [m.astype(BF16) for m in ms]
        ms = [jnp.dot(mb, mb, preferred_element_type=F32) for mb in mbs]
        ps = [p + jnp.dot(p.astype(BF16), m.astype(BF16), preferred_element_type=F32) for p, m in zip(ps, ms)]
        span *= 2
    return ps


def _gdn_kernel(qkv_ref, z_ref, ab_ref, hist_ref, s0_ref, convw_ref, gp_ref, norm_ref,
                o_ref, convnew_ref, s_ref, ext_ref, *, c):
    step = pl.program_id(1)
    nb = qkv_ref.shape[0]

    @pl.when(step == 0)
    def _():
        ext_ref[:, 0:SUBLANES, :] = hist_ref[...]
        s_ref[...] = s0_ref[...]

    row = lax.broadcasted_iota(jnp.int32, (c, c), 0)
    col = lax.broadcasted_iota(jnp.int32, (c, c), 1)
    incl = row >= col
    strict = row > col
    tril = jnp.where(incl, 1.0, 0.0).astype(F32)
    gp = gp_ref[...]
    scale = GDN_D ** -0.5
    heads = [(bi, h) for bi in range(nb) for h in range(GDN_HEADS)]
    s_old = [s_ref[bi, h] for bi, h in heads]

    acts, gc_alls, gc_ts, beta_alls = [], [], [], []
    for bi in range(nb):
        ext_ref[bi, SUBLANES:SUBLANES + c, :] = qkv_ref[bi]
        base = SUBLANES - (CONV_W - 1)
        conv = ext_ref[bi, base:base + c, :] * convw_ref[0:1, :]
        for j in range(1, CONV_W):
            conv = conv + ext_ref[bi, base + j:base + j + c, :] * convw_ref[j:j + 1, :]
        tail = ext_ref[bi, c:c + SUBLANES, :]
        convnew_ref[bi] = tail
        ext_ref[bi, 0:SUBLANES, :] = tail
        acts.append(conv * jax.nn.sigmoid(conv))
        ab = ab_ref[bi]
        g = -jnp.exp(gp[0:1, :]) * jax.nn.softplus(ab + gp[1:2, :])
        beta_alls.append(jax.nn.sigmoid(ab))
        gc_all = jnp.dot(tril, g, precision=HIGHEST, preferred_element_type=F32)
        gc_pad = jnp.concatenate([gc_all, jnp.zeros((LANES - c, LANES), F32)], axis=0) if c < LANES else gc_all
        gc_alls.append(gc_all)
        gc_ts.append(gc_pad.T)

    q, k, v = [], [], []
    for bi, h in heads:
        lo = h * GDN_D
        qh = acts[bi][:, lo:lo + GDN_D]
        kh = acts[bi][:, QK_A + lo:QK_A + lo + GDN_D]
        q.append(qh * lax.rsqrt(jnp.sum(qh * qh, axis=-1, keepdims=True) + EPS) * scale)
        k.append(kh * lax.rsqrt(jnp.sum(kh * kh, axis=-1, keepdims=True) + EPS))
        v.append(acts[bi][:, 2 * QK_A + lo:2 * QK_A + lo + GDN_D])
    beta = [beta_alls[bi][:, GDN_HEADS + h:GDN_HEADS + h + 1] for bi, h in heads]
    gc = [gc_alls[bi][:, h:h + 1] for bi, h in heads]
    gc_last = [gc_alls[bi][c - 1:c, h:h + 1] for bi, h in heads]
    heads = range(len(heads))
    decay = [jnp.exp(jnp.where(incl, gc[h] - gc_ts[h // GDN_HEADS][h % GDN_HEADS:h % GDN_HEADS + 1, 0:c],
                               -jnp.inf)) for h in heads]
    egc = [jnp.exp(gc[h]) for h in heads]
    kb = [k[h] * beta[h] for h in heads]
    qk = [lax.dot_general(jnp.concatenate([q[h], kb[h]], axis=0).astype(BF16), k[h].astype(BF16),
                          (((1,), (1,)), ((), ())), preferred_element_type=F32) for h in heads]
    a_qk = [jnp.where(incl, qk[h][0:c] * decay[h], 0.0) for h in heads]
    tinv = _tri_inverse([jnp.where(strict, qk[h][c:2 * c] * decay[h], 0.0) for h in heads], c)
    uw = [jnp.dot(tinv[h].astype(BF16),
                  jnp.concatenate([v[h] * beta[h], kb[h] * egc[h]], axis=1).astype(BF16),
                  preferred_element_type=F32) for h in heads]
    ws_qs = [jnp.dot(jnp.concatenate([uw[h][:, GDN_D:2 * GDN_D], q[h] * egc[h]], axis=0).astype(BF16),
                     s_old[h].astype(BF16), preferred_element_type=F32) for h in heads]
    v_new = [(uw[h][:, 0:GDN_D] - ws_qs[h][0:c]).astype(BF16) for h in heads]
    o = [ws_qs[h][c:2 * c] + jnp.dot(a_qk[h].astype(BF16), v_new[h], preferred_element_type=F32)
         for h in heads]
    s_new = [s_old[h] * jnp.exp(gc_last[h]) + lax.dot_general(
        (k[h] * jnp.exp(gc_last[h] - gc[h])).astype(BF16), v_new[h], (((0,), (0,)), ((), ())),
        preferred_element_type=F32) for h in heads]
    for bi in range(nb):
        z_all = z_ref[bi]
        gate = z_all * jax.nn.sigmoid(z_all)
        o_ref[bi] = jnp.concatenate([_rms(o[bi * GDN_HEADS + h], norm_ref[...]) for h in range(GDN_HEADS)],
                                    axis=1) * gate
    for h in heads:
        s_ref[h // GDN_HEADS, h % GDN_HEADS] = s_new[h]


def _gdn(proj3, hist8, s0, conv_w, gate_par, gdn_norm, c, nb):
    b, t, _ = proj3.shape
    nchunks = t // c
    kern = functools.partial(_gdn_kernel, c=c)
    return pl.pallas_call(
        kern,
        out_shape=(jax.ShapeDtypeStruct((b, t, V_A), F32),
                   jax.ShapeDtypeStruct((b, SUBLANES, CONV_CH), F32),
                   jax.ShapeDtypeStruct((b, GDN_HEADS, GDN_D, GDN_D), F32)),
        grid=(b // nb, nchunks),
        in_specs=[pl.BlockSpec((nb, c, CONV_CH), lambda i, j: (i, j, 0)),
                  pl.BlockSpec((nb, c, V_A), lambda i, j: (i, j, COL_Z)),
                  pl.BlockSpec((nb, c, LANES), lambda i, j: (i, j, COL_AB)),
                  pl.BlockSpec((nb, SUBLANES, CONV_CH), lambda i, j: (i, 0, 0)),
                  pl.BlockSpec((nb, GDN_HEADS, GDN_D, GDN_D), lambda i, j: (i, 0, 0, 0)),
                  pl.BlockSpec((CONV_W, CONV_CH), lambda i, j: (0, 0)),
                  pl.BlockSpec((SUBLANES, LANES), lambda i, j: (0, 0)),
                  pl.BlockSpec((1, GDN_D), lambda i, j: (0, 0))],
        out_specs=(pl.BlockSpec((nb, c, V_A), lambda i, j: (i, j, 0)),
                   pl.BlockSpec((nb, SUBLANES, CONV_CH), lambda i, j: (i, 0, 0)),
                   pl.BlockSpec((nb, GDN_HEADS, GDN_D, GDN_D), lambda i, j: (i, 0, 0, 0))),
        scratch_shapes=[pltpu.VMEM((nb, SUBLANES + c, CONV_CH), F32)],
        compiler_params=_cparams(("arbitrary", "arbitrary")),
        name="gdn_mixer",
    )(proj3, proj3, proj3, hist8, s0, conv_w, gate_par, gdn_norm)


def _swa_kernel(*refs, nq, nsub, piece_rows, masked):
    npieces = len(piece_rows)
    q_ref = refs[0]
    k_refs = refs[1:1 + npieces]
    v_refs = refs[1 + npieces:1 + 2 * npieces]
    bias_ref, sink_ref, o_ref = refs[1 + 2 * npieces:]
    nk = WIN_PAST + nq
    kcat = jnp.concatenate([r[...] for r in k_refs], axis=0).astype(BF16)
    vcat = jnp.concatenate([r[...] for r in v_refs], axis=0).astype(BF16)
    q = q_ref[...]
    kidx = lax.broadcasted_iota(jnp.int32, (SWA_GROUP * nq, nk), 1)
    blocks = [(i, kv) for i in range(nsub) for kv in range(SWA_KV_HEADS)]
    scores = []
    for i, kv in blocks:
        qs = jnp.concatenate(
            [q[i * nq:(i + 1) * nq, (kv * SWA_GROUP + g) * SWA_DH:(kv * SWA_GROUP + g + 1) * SWA_DH]
             for g in range(SWA_GROUP)], axis=0).astype(BF16)
        kh = kcat[i * nq:i * nq + nk, kv * SWA_DH:(kv + 1) * SWA_DH]
        s = lax.dot_general(qs, kh, (((1,), (1,)), ((), ())), preferred_element_type=F32)
        s = s * SWA_DH ** -0.5 + bias_ref[kv]
        if masked:
            first = (pl.program_id(1) * nsub + i) * nq - WIN_PAST
            s = jnp.where(kidx + first >= 0, s, -jnp.inf)
        scores.append(s)
    sinks = [sink_ref[kv] for _, kv in blocks]
    maxes = [jnp.maximum(jnp.max(s, axis=-1, keepdims=True), sk) for s, sk in zip(scores, sinks)]
    probs = [jnp.exp(s - m) for s, m in zip(scores, maxes)]
    dens = [jnp.sum(p, axis=-1, keepdims=True) + jnp.exp(sk - m) for p, sk, m in zip(probs, sinks, maxes)]
    outs = [jnp.dot((p / den).astype(BF16), vcat[i * nq:i * nq + nk, kv * SWA_DH:(kv + 1) * SWA_DH],
                    preferred_element_type=F32) for (i, kv), p, den in zip(blocks, probs, dens)]
    out_rows = []
    for i in range(nsub):
        heads = []
        for kv in range(SWA_KV_HEADS):
            o = outs[i * SWA_KV_HEADS + kv]
            heads += [o[g * nq:(g + 1) * nq, :] for g in range(SWA_GROUP)]
        out_rows.append(jnp.concatenate(heads, axis=1))
    o_ref[...] = out_rows[0] if nsub == 1 else jnp.concatenate(out_rows, axis=0)


def _rel_bucket(nq, nk):
    rel = jnp.arange(nk)[None, :] - WIN_PAST - jnp.arange(nq)[:, None]
    nb = REL_BUCKETS // 2
    max_exact = nb // 2
    n = jnp.abs(rel)
    large = max_exact + (jnp.log(jnp.maximum(n, 1).astype(F32) / max_exact)
                         / math.log(REL_MAX_DIST / max_exact) * (nb - max_exact)).astype(jnp.int32)
    large = jnp.minimum(large, nb - 1)
    return jnp.where(rel > 0, nb, 0) + jnp.where(n < max_exact, n, large)


def _swa_tables(table, sinks, nq, nk):
    onehot = (_rel_bucket(nq, nk)[:, :, None] == jnp.arange(REL_BUCKETS)[None, None, :]).astype(F32)
    bias = jnp.einsum("qkb,bh->qkh", onehot, table.astype(F32), precision=HIGHEST)
    bias = jnp.transpose(bias, (2, 0, 1)).astype(F32)
    bias = bias.reshape(SWA_KV_HEADS, SWA_GROUP * nq, nk)
    sk = jnp.broadcast_to(sinks.astype(F32).reshape(SWA_KV_HEADS, SWA_GROUP, 1, 1),
                          (SWA_KV_HEADS, SWA_GROUP, nq, 1)).reshape(SWA_KV_HEADS, SWA_GROUP * nq, 1)
    return bias, sk


def _swa_prompt(proj3, table, sinks):
    b, t, _ = proj3.shape
    nq = CHUNK
    nsub = SWA_CHUNKS_PER_STEP
    rows = nsub * nq
    assert rows % WIN_PAST == 0 and t % rows == 0
    past_per_step = rows // WIN_PAST
    bias, sk = _swa_tables(table, sinks, nq, WIN_PAST + nq)
    kern = functools.partial(_swa_kernel, nq=nq, nsub=nsub, piece_rows=(WIN_PAST, rows), masked=True)

    def past_spec(colblk):
        return pl.BlockSpec((None, WIN_PAST, LANES),
                            lambda i, j: (i, jnp.maximum(j * past_per_step - 1, 0), colblk))

    def cur_spec(colblk):
        return pl.BlockSpec((None, rows, LANES), lambda i, j: (i, j, colblk))

    in_specs = [pl.BlockSpec((None, rows, Q_B), lambda i, j: (i, j, COL_QB)),
                past_spec(COL_KB), cur_spec(COL_KB), past_spec(COL_VB), cur_spec(COL_VB),
                pl.BlockSpec(bias.shape, lambda i, j: (0, 0, 0)),
                pl.BlockSpec(sk.shape, lambda i, j: (0, 0, 0))]
    return pl.pallas_call(
        kern,
        out_shape=jax.ShapeDtypeStruct((b, t, Q_B), F32),
        grid=(b, t // rows),
        in_specs=in_specs,
        out_specs=pl.BlockSpec((None, rows, Q_B), lambda i, j: (i, j, 0)),
        compiler_params=_cparams(("arbitrary", "arbitrary")),
        name="swa_prompt",
    )(proj3, proj3, proj3, proj3, proj3, bias, sk)


def _swa_sample(proj3, k_past, v_past, table, sinks):
    b, t, _ = proj3.shape
    bias, sk = _swa_tables(table, sinks, t, WIN_PAST + t)
    kern = functools.partial(_swa_kernel, nq=t, nsub=1, piece_rows=(WIN_PAST, t), masked=False)
    in_specs = [pl.BlockSpec((None, t, Q_B), lambda i, j: (i, 0, COL_QB)),
                pl.BlockSpec((None, WIN_PAST, LANES), lambda i, j: (i, 0, 0)),
                pl.BlockSpec((None, t, LANES), lambda i, j: (i, 0, COL_KB)),
                pl.BlockSpec((None, WIN_PAST, LANES), lambda i, j: (i, 0, 0)),
                pl.BlockSpec((None, t, LANES), lambda i, j: (i, 0, COL_VB)),
                pl.BlockSpec(bias.shape, lambda i, j: (0, 0, 0)),
                pl.BlockSpec(sk.shape, lambda i, j: (0, 0, 0))]
    return pl.pallas_call(
        kern,
        out_shape=jax.ShapeDtypeStruct((b, t, Q_B), F32),
        grid=(b, 1),
        in_specs=in_specs,
        out_specs=pl.BlockSpec((None, t, Q_B), lambda i, j: (i, 0, 0)),
        compiler_params=_cparams(("arbitrary", "arbitrary")),
        name="swa_sample",
    )(proj3, k_past, proj3, v_past, proj3, bias, sk)


def _outproj_kernel(x_ref, oa_ref, ob_ref, wa_ref, wb_ref, g_ref, wq_ref, x1_ref, q_ref):
    x1 = (x_ref[...]
          + jnp.dot(oa_ref[...].astype(BF16), wa_ref[...], preferred_element_type=F32)
          + jnp.dot(ob_ref[...].astype(BF16), wb_ref[...], preferred_element_type=F32))
    x1_ref[...] = x1
    hc = _rms(x1, g_ref[...]).astype(BF16)
    q_ref[...] = jnp.dot(hc, wq_ref[...], preferred_element_type=F32)


def _outproj(x, o_a, o_b, w_out, norm_cross, w_mq, tm):
    m = x.shape[0]
    return pl.pallas_call(
        _outproj_kernel,
        out_shape=(jax.ShapeDtypeStruct((m, D_MODEL), F32), jax.ShapeDtypeStruct((m, MEM_W), F32)),
        grid=(m // tm,),
        in_specs=[pl.BlockSpec((tm, D_MODEL), lambda i: (i, 0)),
                  pl.BlockSpec((tm, V_A), lambda i: (i, 0)),
                  pl.BlockSpec((tm, Q_B), lambda i: (i, 0)),
                  pl.BlockSpec((V_A, D_MODEL), lambda i: (0, 0)),
                  pl.BlockSpec((Q_B, D_MODEL), lambda i: (1, 0)),
                  pl.BlockSpec((1, D_MODEL), lambda i: (0, 0)),
                  pl.BlockSpec((D_MODEL, MEM_W), lambda i: (0, 0))],
        out_specs=(pl.BlockSpec((tm, D_MODEL), lambda i: (i, 0)),
                   pl.BlockSpec((tm, MEM_W), lambda i: (i, 0))),
        compiler_params=_cparams(("arbitrary",)),
        name="outproj",
    )(x, o_a, o_b, w_out, w_out, norm_cross, w_mq)


def _cross_router_kernel(x1_ref, q_ref, mk_ref, mv_ref, wo_ref, g_ref, wr_ref, br_ref,
                         x2_ref, hp_ref, route_ref):
    nb, tm = x1_ref.shape[0], x1_ref.shape[1]
    rows = []
    for bi in range(nb):
        q = q_ref[bi]
        outs = []
        for h in range(MEM_HEADS):
            sl = slice(h * MEM_DH, (h + 1) * MEM_DH)
            mk = mk_ref[bi, pl.ds(h, MEM_LEN, stride=MEM_HEADS), :].astype(BF16)
            mv = mv_ref[bi, pl.ds(h, MEM_LEN, stride=MEM_HEADS), :].astype(BF16)
            s = lax.dot_general(q[:, sl].astype(BF16), mk, (((1,), (1,)), ((), ())),
                                preferred_element_type=F32) * MEM_DH ** -0.5
            m = jnp.max(s, axis=-1, keepdims=True)
            p = jnp.exp(s - m)
            p = (p / jnp.sum(p, axis=-1, keepdims=True)).astype(BF16)
            outs.append(jnp.dot(p, mv, preferred_element_type=F32))
        rows.append(jnp.concatenate(outs, axis=1))
    att = (rows[0] if nb == 1 else jnp.concatenate(rows, axis=0)).astype(BF16)
    x2 = x1_ref[...].reshape(nb * tm, D_MODEL) + jnp.dot(att, wo_ref[...], preferred_element_type=F32)
    x2_ref[...] = x2.reshape(nb, tm, D_MODEL)
    hf = _rms(x2, g_ref[...])
    packed = _pack_halves(hf[:, 0:HALF], hf[:, HALF:D_MODEL])
    for bi in range(nb):
        for j in range(XROW_TILES):
            hp_ref[bi, pl.ds(j, tm, stride=XROW_TILES), :] = packed[bi * tm:(bi + 1) * tm, j * LANES:(j + 1) * LANES]
    hf_hi = hf.astype(BF16)
    hf_lo = (hf - hf_hi.astype(F32)).astype(BF16)
    wr = wr_ref[...]
    r_hi = jnp.dot(hf_hi, wr, preferred_element_type=F32)
    r_lo = jnp.dot(hf_lo, wr[:, 0:LANES], preferred_element_type=F32)
    logits = r_hi[:, 0:LANES] + r_hi[:, LANES:2 * LANES] + r_lo + br_ref[...]
    lane = lax.broadcasted_iota(jnp.int32, logits.shape, 1)
    lanef = lane.astype(F32)
    l = jnp.where(lane < N_EXPERTS, logits, -jnp.inf)
    vals, idxs = [], []
    for _ in range(TOP_K):
        m = jnp.max(l, axis=-1, keepdims=True)
        idx = jnp.min(jnp.where(l == m, lanef, float(LANES)), axis=-1, keepdims=True)
        vals.append(m)
        idxs.append(idx)
        l = jnp.where(lanef == idx, -jnp.inf, l)
    es = [jnp.exp(v - vals[0]) for v in vals]
    den = es[0] + es[1] + es[2] + es[3]
    route = jnp.zeros(logits.shape, F32)
    for k in range(TOP_K):
        route = jnp.where(lane == k, idxs[k], route)
        route = jnp.where(lane == TOP_K + k, es[k] / den, route)
    route_ref[...] = route.reshape(nb, tm, LANES)


def _split_bf16(w):
    w_hi = w.astype(BF16)
    w_lo = (w - w_hi.astype(F32)).astype(BF16)
    return jnp.concatenate([w_hi, w_lo], axis=1)


def _cross_router(x1, qm, mk, mv, w_mo, norm_ffn, w_router, b_router, nb, tm):
    b, t, _ = x1.shape
    nt = t // tm
    return pl.pallas_call(
        _cross_router_kernel,
        out_shape=(jax.ShapeDtypeStruct((b, t, D_MODEL), F32),
                   jax.ShapeDtypeStruct((b, t * XROW_TILES, LANES), jnp.uint32),
                   jax.ShapeDtypeStruct((b, t, LANES), F32)),
        grid=(b // nb, nt),
        in_specs=[pl.BlockSpec((nb, tm, D_MODEL), lambda i, j: (i, j, 0)),
                  pl.BlockSpec((nb, tm, MEM_W), lambda i, j: (i, j, 0)),
                  pl.BlockSpec((nb, MEM_LEN * MEM_HEADS, MEM_DH), lambda i, j: (i, 0, 0)),
                  pl.BlockSpec((nb, MEM_LEN * MEM_HEADS, MEM_DH), lambda i, j: (i, 0, 0)),
                  pl.BlockSpec((MEM_W, D_MODEL), lambda i, j: (0, 0)),
                  pl.BlockSpec((1, D_MODEL), lambda i, j: (0, 0)),
                  pl.BlockSpec((D_MODEL, 2 * LANES), lambda i, j: (0, 0)),
                  pl.BlockSpec((1, LANES), lambda i, j: (0, 0))],
        out_specs=(pl.BlockSpec((nb, tm, D_MODEL), lambda i, j: (i, j, 0)),
                   pl.BlockSpec((nb, tm * XROW_TILES, LANES), lambda i, j: (i, j, 0)),
                   pl.BlockSpec((nb, tm, LANES), lambda i, j: (i, j, 0))),
        compiler_params=_cparams(("arbitrary", "arbitrary")),
        name="cross_router",
    )(x1, qm, mk, mv, w_mo, norm_ffn, w_router, b_router)


def _row_tile(idx, tiles):
    return pl.ds(pl.multiple_of(idx * tiles, tiles), tiles)


def _dispatch_kernel(*refs, tokens, first_group):
    if first_group:
        pos_ref, tail_ref, hp_ref, xs_hbm, zero_ref, sem = refs
    else:
        pos_ref, tail_ref, hp_ref, _, xs_hbm, zero_ref, sem = refs

    def zero_tails():
        zero_ref[...] = jnp.zeros(zero_ref.shape, zero_ref.dtype)

        def tail_copy(e):
            dst = xs_hbm.at[pl.ds(pl.multiple_of(tail_ref[e] * XROW_TILES, XROW_TILES), MOE_SUB * XROW_TILES)]
            return pltpu.make_async_copy(zero_ref, dst, sem.at[0])

        for e in range(N_EXPERTS):
            pl.when(tail_ref[e] >= 0)(lambda e=e: tail_copy(e).start())
        for e in range(N_EXPERTS):
            pl.when(tail_ref[e] >= 0)(lambda e=e: tail_copy(e).wait())

    if first_group:
        pl.when(pl.program_id(0) == 0)(zero_tails)

    def issue(t, carry):
        src = hp_ref.at[_row_tile(t, XROW_TILES)]
        for k in range(TOP_K):
            dst = xs_hbm.at[_row_tile(pos_ref[t * TOP_K + k], XROW_TILES)]
            pltpu.make_async_copy(src, dst, sem.at[1]).start()
        return carry

    lax.fori_loop(0, tokens, issue, 0, unroll=4)

    def drain(t, carry):
        for k in range(TOP_K):
            pltpu.make_async_copy(hp_ref.at[_row_tile(0, XROW_TILES)], xs_hbm.at[_row_tile(0, XROW_TILES)],
                                  sem.at[1]).wait()
        return carry

    lax.fori_loop(0, tokens, drain, 0)


def _dispatch(hp, pos_flat, tail_start, rows, tokens, xs_prev=None):
    t = hp.shape[0] // XROW_TILES
    first_group = xs_prev is None
    kern = functools.partial(_dispatch_kernel, tokens=tokens, first_group=first_group)
    in_specs = [pl.BlockSpec((tokens * TOP_K,), lambda i: (i,), memory_space=pltpu.SMEM),
                pl.BlockSpec((N_EXPERTS,), lambda i: (0,), memory_space=pltpu.SMEM),
                pl.BlockSpec((tokens * XROW_TILES, LANES), lambda i: (i, 0))]
    args = [pos_flat, tail_start, hp]
    if not first_group:
        in_specs.append(pl.BlockSpec(memory_space=pl.ANY))
        args.append(xs_prev)
    return pl.pallas_call(
        kern,
        out_shape=jax.ShapeDtypeStruct((rows * XROW_TILES, LANES), jnp.uint32),
        grid=(t // tokens,),
        in_specs=in_specs,
        out_specs=pl.BlockSpec(memory_space=pl.ANY),
        scratch_shapes=[pltpu.VMEM((MOE_SUB * XROW_TILES, LANES), jnp.uint32),
                        pltpu.SemaphoreType.DMA((2,))],
        input_output_aliases={} if first_group else {3: 0},
        compiler_params=pltpu.CompilerParams(dimension_semantics=("arbitrary",),
                                             vmem_limit_bytes=VMEM_LIMIT_BYTES, has_side_effects=True),
        name="moe_dispatch",
    )(*args)


def _moe_kernel(ie_ref, is_ref, ns_ref, xs_hbm, wg_ref, bg_ref, wu_ref, bu_ref, wd_ref, bd_ref,
                ys_hbm, xu_ref, acc_ref, ystage_ref, in_sem, out_sem):
    w = pl.program_id(0)
    f = pl.program_id(1)
    nsub = ns_ref[w]
    start = is_ref[w]

    def sub_rows(s):
        return pl.ds(pl.multiple_of(s * MOE_SUB, MOE_SUB), MOE_SUB)

    nstage = nsub * (MOE_SUB // MOE_STAGE)

    def stage_rows(s):
        return pl.ds(pl.multiple_of(s * MOE_STAGE, MOE_STAGE), MOE_STAGE)

    def hbm_rows(first, s, tiles):
        return pl.ds(pl.multiple_of((first + s * MOE_STAGE) * tiles, MOE_STAGE * tiles), MOE_STAGE * tiles)

    def load(first, s):
        dst = xu_ref.at[pl.ds(pl.multiple_of(s * MOE_STAGE * XROW_TILES, MOE_STAGE * XROW_TILES),
                              MOE_STAGE * XROW_TILES)]
        return pltpu.make_async_copy(xs_hbm.at[hbm_rows(first, s, XROW_TILES)], dst, in_sem.at[0])

    def fetch_item(first, stages):
        def issue(s, carry):
            load(first, s).start()
            return carry

        lax.fori_loop(0, stages, issue, 0)

    pl.when(jnp.logical_and(jnp.logical_and(w == 0, f == 0), nsub > 0))(lambda: fetch_item(start, nstage))

    @pl.when(jnp.logical_and(f == 0, nsub > 0))
    def _():
        def landed(s, carry):
            load(start, s).wait()
            return carry

        lax.fori_loop(0, nstage, landed, 0)

        def init(s, carry):
            acc_ref[sub_rows(s), :] = jnp.broadcast_to(bd_ref[...], (MOE_SUB, D_MODEL))
            return carry

        lax.fori_loop(0, nsub, init, 0)

    def x_tile(s):
        base = pl.multiple_of(s * MOE_SUB * XROW_TILES, MOE_SUB * XROW_TILES)
        lows, highs = [], []
        for j in range(XROW_TILES):
            lo, hi = _unpack_halves(xu_ref[pl.ds(base + j, MOE_SUB, stride=XROW_TILES), :])
            lows.append(lo.astype(BF16))
            highs.append(hi.astype(BF16))
        return jnp.concatenate(lows + highs, axis=1)

    @pl.when(nsub > 0)
    def _():
        bg = bg_ref[...]
        bu = bu_ref[...]

        def expert_out(s):
            x = x_tile(s)
            gt = jnp.dot(x, wg_ref[...].astype(BF16), preferred_element_type=F32) + bg
            up = jnp.dot(x, wu_ref[...].astype(BF16), preferred_element_type=F32) + bu
            gt = jnp.minimum(gt, SWIGLU_LIMIT)
            up = jnp.clip(up, -SWIGLU_LIMIT, SWIGLU_LIMIT)
            a = gt * jax.nn.sigmoid(SWIGLU_ALPHA * gt) * (up + 1.0)
            return jnp.dot(a.astype(BF16), wd_ref[...].astype(BF16), preferred_element_type=F32)

        def body(s, carry):
            acc_ref[sub_rows(s), :] += expert_out(s)
            return carry

        lax.fori_loop(0, nsub, body, 0)

    @pl.when(jnp.logical_and(f == MOE_NF - 1, nsub > 0))
    def _():
        nxt = jnp.minimum(w + 1, pl.num_programs(0) - 1)
        nxt_stages = jnp.where(w + 1 < pl.num_programs(0), ns_ref[nxt], 0) * (MOE_SUB // MOE_STAGE)
        fetch_item(is_ref[nxt], nxt_stages)

        def store(s):
            slot = s % MOE_OUT_SLOTS
            return pltpu.make_async_copy(ystage_ref.at[slot], ys_hbm.at[hbm_rows(start, s, YROW_TILES)],
                                         out_sem.at[slot])

        def write(s, carry):
            pl.when(s >= MOE_OUT_SLOTS)(lambda: store(s - MOE_OUT_SLOTS).wait())
            for j in range(YROW_TILES):
                lo = acc_ref[stage_rows(s), j * LANES:(j + 1) * LANES]
                hi = acc_ref[stage_rows(s), HALF + j * LANES:HALF + (j + 1) * LANES]
                ystage_ref[s % MOE_OUT_SLOTS, pl.ds(j, MOE_STAGE, stride=YROW_TILES), :] = _pack_halves(lo, hi)
            store(s).start()
            return carry

        lax.fori_loop(0, nstage, write, 0)
        for back in range(1, MOE_OUT_SLOTS + 1):
            pl.when(nstage >= back)(lambda back=back: store(nstage - back).wait())


def _moe(xs, item_expert, item_start, item_nsub, w_gate, b_gate, w_up, b_up, w_down, b_down):
    rows = xs.shape[0] // XROW_TILES
    nitems = item_expert.shape[0]

    def fcol(w, f, ie, st, ns):
        return jnp.where(ns[w] > 0, f, MOE_NF - 1)

    grid_spec = pltpu.PrefetchScalarGridSpec(
        num_scalar_prefetch=3,
        grid=(nitems, MOE_NF),
        in_specs=[pl.BlockSpec(memory_space=pl.ANY),
                  pl.BlockSpec((None, D_MODEL, MOE_TF), lambda w, f, ie, st, ns: (ie[w], 0, fcol(w, f, ie, st, ns))),
                  pl.BlockSpec((None, 1, MOE_TF), lambda w, f, ie, st, ns: (ie[w], 0, fcol(w, f, ie, st, ns))),
                  pl.BlockSpec((None, D_MODEL, MOE_TF), lambda w, f, ie, st, ns: (ie[w], 0, fcol(w, f, ie, st, ns))),
                  pl.BlockSpec((None, 1, MOE_TF), lambda w, f, ie, st, ns: (ie[w], 0, fcol(w, f, ie, st, ns))),
                  pl.BlockSpec((None, MOE_TF, D_MODEL), lambda w, f, ie, st, ns: (ie[w], fcol(w, f, ie, st, ns), 0)),
                  pl.BlockSpec((None, 1, D_MODEL), lambda w, f, ie, st, ns: (ie[w], 0, 0))],
        out_specs=pl.BlockSpec(memory_space=pl.ANY),
        scratch_shapes=[pltpu.VMEM((MOE_ROWS * XROW_TILES, LANES), jnp.uint32),
                        pltpu.VMEM((MOE_ROWS, D_MODEL), F32),
                        pltpu.VMEM((MOE_OUT_SLOTS, MOE_STAGE * YROW_TILES, LANES), jnp.uint32),
                        pltpu.SemaphoreType.DMA((1,)),
                        pltpu.SemaphoreType.DMA((MOE_OUT_SLOTS,))])
    return pl.pallas_call(
        _moe_kernel,
        out_shape=jax.ShapeDtypeStruct((rows * YROW_TILES, LANES), jnp.uint32),
        grid_spec=grid_spec,
        compiler_params=pltpu.CompilerParams(dimension_semantics=("arbitrary", "arbitrary"),
                                             vmem_limit_bytes=MOE_VMEM_LIMIT_BYTES, has_side_effects=True),
        name="moe_experts",
    )(item_expert, item_start, item_nsub, xs, w_gate, b_gate, w_up, b_up, w_down, b_down)


def _combine_kernel(pos_ref, nxt_ref, x2_ref, route_ref, g_ref, ys_hbm, o_ref, buf_ref, sem, *, tokens):
    i = pl.program_id(0)
    slot = i % 2

    def gather(p_ref, dst_slot):
        def issue(t, carry):
            for k in range(TOP_K):
                pltpu.make_async_copy(ys_hbm.at[_row_tile(p_ref[t * TOP_K + k], YROW_TILES)],
                                      buf_ref.at[dst_slot, k, _row_tile(t, YROW_TILES)], sem.at[dst_slot]).start()
            return carry

        lax.fori_loop(0, tokens, issue, 0, unroll=4)

    pl.when(i == 0)(lambda: gather(pos_ref, 0))
    pl.when(i + 1 < pl.num_programs(0))(lambda: gather(nxt_ref, 1 - slot))

    def drain(t, carry):
        for k in range(TOP_K):
            pltpu.make_async_copy(ys_hbm.at[_row_tile(0, YROW_TILES)],
                                  buf_ref.at[slot, k, _row_tile(0, YROW_TILES)], sem.at[slot]).wait()
        return carry

    lax.fori_loop(0, tokens, drain, 0)

    route = route_ref[...]
    gates = [route[:, TOP_K + k:TOP_K + k + 1] for k in range(TOP_K)]
    lows, highs = [], []
    for j in range(YROW_TILES):
        y_lo = x2_ref[:, j * LANES:(j + 1) * LANES]
        y_hi = x2_ref[:, HALF + j * LANES:HALF + (j + 1) * LANES]
        for k in range(TOP_K):
            lo, hi = _unpack_halves(buf_ref[slot, k, pl.ds(j, tokens, stride=YROW_TILES), :])
            y_lo = y_lo + gates[k] * lo
            y_hi = y_hi + gates[k] * hi
        lows.append(y_lo)
        highs.append(y_hi)
    o_ref[...] = _rms(jnp.concatenate(lows + highs, axis=1), g_ref[...])


def _combine(x2, route, final_norm, ys, pos_flat, tokens):
    t = x2.shape[0]
    nsteps = t // tokens
    kern = functools.partial(_combine_kernel, tokens=tokens)
    return pl.pallas_call(
        kern,
        out_shape=jax.ShapeDtypeStruct((t, D_MODEL), F32),
        grid=(nsteps,),
        in_specs=[pl.BlockSpec((tokens * TOP_K,), lambda i: (i,), memory_space=pltpu.SMEM),
                  pl.BlockSpec((tokens * TOP_K,), lambda i: (jnp.minimum(i + 1, nsteps - 1),),
                               memory_space=pltpu.SMEM),
                  pl.BlockSpec((tokens, D_MODEL), lambda i: (i, 0)),
                  pl.BlockSpec((tokens, LANES), lambda i: (i, 0)),
                  pl.BlockSpec((1, D_MODEL), lambda i: (0, 0)),
                  pl.BlockSpec(memory_space=pl.ANY)],
        out_specs=pl.BlockSpec((tokens, D_MODEL), lambda i: (i, 0)),
        scratch_shapes=[pltpu.VMEM((2, TOP_K, tokens * YROW_TILES, LANES), jnp.uint32),
                        pltpu.SemaphoreType.DMA((2,))],
        compiler_params=_cparams(("arbitrary",)),
        name="moe_combine",
    )(pos_flat, pos_flat, x2, route, final_norm, ys)


def _routing_tables(top_i, nitems):
    t = top_i.shape[0]
    sel = jnp.sum((top_i[:, :, None] == jnp.arange(N_EXPERTS, dtype=jnp.int32)[None, None, :]).astype(jnp.int32),
                  axis=1)
    cnt = jnp.sum(sel, axis=0)
    rank = jnp.cumsum(sel, axis=0) - sel
    cnt_pad = ((cnt + MOE_SUB - 1) // MOE_SUB) * MOE_SUB
    off = jnp.cumsum(cnt_pad) - cnt_pad
    pos = jnp.take_along_axis(off[None, :] + rank, top_i, axis=1)
    tail_start = jnp.where(cnt > 0, off + cnt_pad - MOE_SUB, -1)
    items_per = (cnt + MOE_ROWS - 1) // MOE_ROWS
    item_end = jnp.cumsum(items_per)
    total = item_end[-1]
    widx = jnp.arange(nitems, dtype=jnp.int32)
    e_of = jnp.minimum(jnp.searchsorted(item_end, widx, side="right"), N_EXPERTS - 1).astype(jnp.int32)
    j_of = widx - (item_end - items_per)[e_of]
    valid = widx < total
    e_last = e_of[jnp.maximum(total - 1, 0)]
    item_expert = jnp.where(valid, e_of, e_last).astype(jnp.int32)
    item_start = jnp.where(valid, off[e_of] + j_of * MOE_ROWS, 0).astype(jnp.int32)
    rows_left = cnt_pad[e_of] - j_of * MOE_ROWS
    item_nsub = jnp.where(valid, jnp.minimum(rows_left, MOE_ROWS) // MOE_SUB, 0).astype(jnp.int32)
    return pos.reshape(t * TOP_K).astype(jnp.int32), tail_start.astype(jnp.int32), item_expert, item_start, item_nsub


def _trunk(x, conv_hist, s0, swa_fn, mk, mv, gdn_chunk, p):
    b, t, _ = x.shape
    m = b * t
    xf = x.reshape(m, D_MODEL)
    proj = _norm_matmul(xf, p["norm_mix"], p["w_in"], min(m, 1024), PROJ_TN)
    proj3 = proj.reshape(b, t, PROJ_COLS)
    hist8 = jnp.pad(conv_hist, ((0, 0), (SUBLANES - (CONV_W - 1), 0), (0, 0)))
    o_a, conv8, s_new = _gdn(proj3, hist8, s0, p["conv_w"], p["gate_par"], p["gdn_norm"], gdn_chunk,
                             GDN_SEQS_PER_STEP)
    o_b = swa_fn(proj3)
    x1, qm = _outproj(xf, o_a.reshape(m, V_A), o_b.reshape(m, Q_B), p["w_out"], p["norm_cross"], p["w_mq"], 256)
    tm = min(t, CROSS_ROWS)
    x2, hp, route = _cross_router(x1.reshape(b, t, D_MODEL), qm.reshape(b, t, MEM_W), mk, mv,
                                  p["w_mo"], p["norm_ffn"], p["w_router"], p["b_router"], CROSS_ROWS // tm, tm)
    return proj3, conv8[:, SUBLANES - (CONV_W - 1):], s_new, x2.reshape(m, D_MODEL), \
        hp.reshape(m * XROW_TILES, LANES), route.reshape(m, LANES)


def kernel(x_prompt, x_sample, cache_conv, state_gdn, cache_swa_k, cache_swa_v, cache_mem_k, cache_mem_v, mem_prompt, norm_mix, w_in, conv_w, gdn_a_log, gdn_dt_bias, gdn_norm, swa_sinks, rel_bias_table, w_out, norm_cross, norm_mem, w_mq, w_mk, w_mv, w_mo, norm_ffn, w_router, b_router, w_gate, b_gate, w_up, b_up, w_down, b_down, final_norm):
    depth = norm_mix.shape[0]
    assert depth == 1, "kernel is written for the single-layer trunk"
    bp, sp, _ = x_prompt.shape
    bs, ss, _ = x_sample.shape
    l = 0
    w = w_in[l]
    n_ab = 2 * GDN_HEADS
    w_perm = jnp.concatenate(
        [w[:, :CONV_CH + V_A], w[:, CONV_CH + V_A + n_ab:], w[:, CONV_CH + V_A:CONV_CH + V_A + n_ab],
         jnp.zeros((D_MODEL, PROJ_COLS - w.shape[1]), w.dtype)], axis=1).astype(BF16)
    gate_par = jnp.zeros((SUBLANES, LANES), F32)
    gate_par = gate_par.at[0, :GDN_HEADS].set(gdn_a_log[l]).at[1, :GDN_HEADS].set(gdn_dt_bias[l])
    p = dict(
        norm_mix=norm_mix[l].reshape(1, D_MODEL), w_in=w_perm, conv_w=conv_w[l], gate_par=gate_par,
        gdn_norm=gdn_norm[l].reshape(1, GDN_D), w_out=w_out[l].astype(BF16),
        norm_cross=norm_cross[l].reshape(1, D_MODEL), w_mq=w_mq[l].astype(BF16), w_mo=w_mo[l].astype(BF16),
        norm_ffn=norm_ffn[l].reshape(1, D_MODEL),
        w_router=_split_bf16(jnp.pad(w_router[l], ((0, 0), (0, LANES - N_EXPERTS)))),
        b_router=jnp.pad(b_router[l], (0, LANES - N_EXPERTS)).reshape(1, LANES))

    w_mkv = jnp.concatenate([w_mk[l], w_mv[l]], axis=1).astype(BF16)
    mkv = _norm_matmul(mem_prompt.reshape(bp * MEM_LEN, D_MODEL), norm_mem[l].reshape(1, D_MODEL), w_mkv,
                       min(bp * MEM_LEN, 1024), MEMKV_TN)
    mk_p = mkv[:, :MEM_W].reshape(bp, MEM_LEN * MEM_HEADS, MEM_DH)
    mv_p = mkv[:, MEM_W:].reshape(bp, MEM_LEN * MEM_HEADS, MEM_DH)

    zero_hist = jnp.zeros((bp, CONV_W - 1, CONV_CH), F32)
    zero_state = jnp.zeros((bp, GDN_HEADS, GDN_D, GDN_D), F32)
    swa_p = functools.partial(_swa_prompt, table=rel_bias_table, sinks=swa_sinks[l])
    proj_p, conv_p, st_p, x2_p, hp_p, route_p = _trunk(x_prompt, zero_hist, zero_state, swa_p, mk_p, mv_p, CHUNK, p)

    k_past = cache_swa_k[l].reshape(bs, WIN_PAST, KV_B)
    v_past = cache_swa_v[l].reshape(bs, WIN_PAST, KV_B)
    swa_s = functools.partial(_swa_sample, k_past=k_past, v_past=v_past, table=rel_bias_table, sinks=swa_sinks[l])
    proj_s, conv_s, st_s, x2_s, hp_s, route_s = _trunk(
        x_sample, cache_conv[l], state_gdn[l], swa_s,
        cache_mem_k[l].reshape(bs, MEM_LEN * MEM_HEADS, MEM_DH),
        cache_mem_v[l].reshape(bs, MEM_LEN * MEM_HEADS, MEM_DH), ss, p)

    mp, ms = bp * sp, bs * ss
    ntok = mp + ms
    top_i = jnp.concatenate([route_p[:, :TOP_K], route_s[:, :TOP_K]], axis=0).astype(jnp.int32)
    rows = ntok * TOP_K + N_EXPERTS * MOE_SUB
    nitems = (ntok * TOP_K) // MOE_ROWS + N_EXPERTS
    pos_flat, tail_start, item_expert, item_start, item_nsub = _routing_tables(top_i, nitems)
    pos_p, pos_s = pos_flat[:mp * TOP_K], pos_flat[mp * TOP_K:]
    xs = _dispatch(hp_p, pos_p, tail_start, rows, 512)
    xs = _dispatch(hp_s, pos_s, tail_start, rows, 512, xs_prev=xs)
    ys = _moe(xs, item_expert, item_start, item_nsub,
              w_gate[l], b_gate[l].reshape(N_EXPERTS, 1, D_FF), w_up[l], b_up[l].reshape(N_EXPERTS, 1, D_FF),
              w_down[l], b_down[l].reshape(N_EXPERTS, 1, D_MODEL))
    fnorm = final_norm.reshape(1, D_MODEL)
    y_p = _combine(x2_p, route_p, fnorm, ys, pos_p, 256).reshape(bp, sp, D_MODEL)
    y_s = _combine(x2_s, route_s, fnorm, ys, pos_s, 256).reshape(bs, ss, D_MODEL)

    def kv_window(proj3, col, past=None):
        new = proj3[:, :, col * LANES:(col + 1) * LANES]
        full = new if past is None else jnp.concatenate([past, new], axis=1)
        win = full[:, -WIN_PAST:]
        return win.reshape(win.shape[0], WIN_PAST, SWA_KV_HEADS, SWA_DH)[None]

    return (y_p, y_s,
            conv_p[None], st_p[None], kv_window(proj_p, COL_KB), kv_window(proj_p, COL_VB),
            mk_p.reshape(bp, MEM_LEN, MEM_HEADS, MEM_DH)[None], mv_p.reshape(bp, MEM_LEN, MEM_HEADS, MEM_DH)[None],
            conv_s[None], st_s[None], kv_window(proj_s, COL_KB, k_past), kv_window(proj_s, COL_VB, v_past))
```

```python
import functools
import math

import numpy as np
import jax
import jax.numpy as jnp
from jax import lax
from jax.experimental import pallas as pl
from jax.experimental.pallas import tpu as pltpu

F32 = jnp.float32
BF16 = jnp.bfloat16
HIGHEST = lax.Precision.HIGHEST

D_MODEL = 2048
CHUNK = 64
GDN_HEADS = 8
GDN_D = 128
CONV_W = 4
SWA_HEADS = 16
SWA_KV_HEADS = 2
SWA_GROUP = SWA_HEADS // SWA_KV_HEADS
SWA_DH = 64
WIN_PAST = 128
REL_BUCKETS = 32
REL_MAX_DIST = 128
MEM_LEN = 256
MEM_HEADS = 4
MEM_DH = 128
N_EXPERTS = 32
TOP_K = 4
D_FF = D_MODEL
SWIGLU_LIMIT = 7.0
SWIGLU_ALPHA = 1.702
EPS = 1e-6

QK_A = GDN_HEADS * GDN_D
V_A = GDN_HEADS * GDN_D
CONV_CH = 2 * QK_A + V_A
Q_B = SWA_HEADS * SWA_DH
KV_B = SWA_KV_HEADS * SWA_DH
MEM_W = MEM_HEADS * MEM_DH

LANES = 128
SUBLANES = 8
VMEM_LIMIT_BYTES = 58 * 1024 * 1024
MOE_VMEM_LIMIT_BYTES = 60 * 1024 * 1024

PROJ_COLS = 5632
PROJ_TN = 512
PROJ_MAIN = CONV_CH + V_A
MEMKV_TN = 512
COL_Z = CONV_CH // V_A
COL_QB = (CONV_CH + V_A) // Q_B
COL_KB = (CONV_CH + V_A + Q_B) // LANES
COL_VB = COL_KB + 1
COL_AB = COL_KB + 2

MOE_SUB = 256
MOE_ROWS = 1536
MOE_STAGE = 128
MOE_OUT_SLOTS = 4
MOE_TF = 512
MOE_NF = D_FF // MOE_TF
HALF = D_MODEL // 2
NORM_ROWS = 256
SWA_CHUNKS_PER_STEP = 4
CROSS_ROWS = 256
GDN_SEQS_PER_STEP = 4
XROW_TILES = HALF // LANES
YROW_TILES = HALF // LANES


def _cparams(sem):
    return pltpu.CompilerParams(dimension_semantics=sem, vmem_limit_bytes=VMEM_LIMIT_BYTES)


def _rms(x, gain):
    return x * lax.rsqrt(jnp.mean(x * x, axis=-1, keepdims=True) + EPS) * gain


def _pack_halves(lo, hi):
    lo_bits = pltpu.bitcast(lo.astype(BF16).astype(F32), jnp.uint32)
    hi_bits = pltpu.bitcast(hi.astype(BF16).astype(F32), jnp.uint32)
    return (lo_bits >> 16) | (hi_bits & jnp.uint32(0xFFFF0000))


def _unpack_halves(u):
    return pltpu.bitcast(u << 16, F32), pltpu.bitcast(u & jnp.uint32(0xFFFF0000), F32)


def _norm_matmul_kernel(x_ref, g_ref, w_ref, o_ref, h_ref):
    @pl.when(pl.program_id(1) == 0)
    def _():
        def body(r, carry):
            rows = pl.ds(pl.multiple_of(r * NORM_ROWS, NORM_ROWS), NORM_ROWS)
            h_ref[rows, :] = _rms(x_ref[rows, :], g_ref[...]).astype(BF16)
            return carry

        lax.fori_loop(0, x_ref.shape[0] // NORM_ROWS, body, 0)

    o_ref[...] = jnp.dot(h_ref[...], w_ref[...], preferred_element_type=F32)


def _norm_matmul(x, gain, w, tm, tn):
    m, k = x.shape
    n = w.shape[1]
    return pl.pallas_call(
        _norm_matmul_kernel,
        out_shape=jax.ShapeDtypeStruct((m, n), F32),
        grid=(m // tm, n // tn),
        in_specs=[pl.BlockSpec((tm, k), lambda i, j: (i, 0)),
                  pl.BlockSpec((1, k), lambda i, j: (0, 0)),
                  pl.BlockSpec((k, tn), lambda i, j: (0, j))],
        out_specs=pl.BlockSpec((tm, tn), lambda i, j: (i, j)),
        scratch_shapes=[pltpu.VMEM((tm, k), BF16)],
        compiler_params=_cparams(("arbitrary", "arbitrary")),
        name="norm_matmul",
    )(x, gain, w)


def _inproj_kernel(x_ref, g_ref, wmain_ref, wtail_ref, o_ref, h_ref, *, nmain):
    j = pl.program_id(1)

    @pl.when(j == 0)
    def _():
        def body(r, carry):
            rows = pl.ds(pl.multiple_of(r * NORM_ROWS, NORM_ROWS), NORM_ROWS)
            h_ref[rows, :] = _rms(x_ref[rows, :], g_ref[...]).astype(BF16)
            return carry

        lax.fori_loop(0, x_ref.shape[0] // NORM_ROWS, body, 0)

    @pl.when(j < nmain)
    def _():
        o_ref[...] = jnp.dot(h_ref[...], wmain_ref[...].astype(BF16), preferred_element_type=F32)

    @pl.when(j >= nmain)
    def _():
        o_ref[...] = jnp.dot(h_ref[...], wtail_ref[...], preferred_element_type=F32)


def _inproj(x, gain, w_main, w_tail, tm):
    m, k = x.shape
    nmain = PROJ_MAIN // PROJ_TN
    ntail = w_tail.shape[1] // PROJ_TN
    kern = functools.partial(_inproj_kernel, nmain=nmain)
    return pl.pallas_call(
        kern,
        out_shape=jax.ShapeDtypeStruct((m, PROJ_COLS), F32),
        grid=(m // tm, nmain + ntail),
        in_specs=[pl.BlockSpec((tm, k), lambda i, j: (i, 0)),
                  pl.BlockSpec((1, k), lambda i, j: (0, 0)),
                  pl.BlockSpec((k, PROJ_TN), lambda i, j: (0, jnp.minimum(j, nmain - 1))),
                  pl.BlockSpec((k, PROJ_TN), lambda i, j: (0, jnp.maximum(j - nmain, 0)))],
        out_specs=pl.BlockSpec((tm, PROJ_TN), lambda i, j: (i, j)),
        scratch_shapes=[pltpu.VMEM((tm, k), BF16)],
        compiler_params=_cparams(("arbitrary", "arbitrary")),
        name="inproj",
    )(x, gain, w_main, w_tail)


def _tri_inverse(lows, c):
    row = lax.broadcasted_iota(jnp.int32, (c, c), 0)
    col = lax.broadcasted_iota(jnp.int32, (c, c), 1)
    eye = jnp.where(row == col, 1.0, 0.0).astype(F32)
    ps = [eye - low for low in lows]
    ms = list(lows)
    span = 1
    while 2 * span < c:
        mbs = [m.astype(BF16) for m in ms]
        ms = [jnp.dot(mb, mb, preferred_element_type=F32) for mb in mbs]
        ps = [p + jnp.dot(p.astype(BF16), m.astype(BF16), preferred_element_type=F32) for p, m in zip(ps, ms)]
        span *= 2
    return ps


def _gdn_kernel(qkv_ref, z_ref, ab_ref, hist_ref, s0_ref, convw_ref, gp_ref, norm_ref,
                o_ref, convnew_ref, s_ref, ext_ref, *, c):
    step = pl.program_id(1)
    nb = qkv_ref.shape[0]

    @pl.when(step == 0)
    def _():
        ext_ref[:, 0:SUBLANES, :] = hist_ref[...]
        s_ref[...] = s0_ref[...]

    row = lax.broadcasted_iota(jnp.int32, (c, c), 0)
    col = lax.broadcasted_iota(jnp.int32, (c, c), 1)
    incl = row >= col
    strict = row > col
    tril = jnp.where(incl, 1.0, 0.0).astype(F32)
    gp = gp_ref[...]
    scale = GDN_D ** -0.5
    heads = [(bi, h) for bi in range(nb) for h in range(GDN_HEADS)]
    s_old = [s_ref[bi, h] for bi, h in heads]

    acts, gc_alls, gc_ts, beta_alls = [], [], [], []
    for bi in range(nb):
        ext_ref[bi, SUBLANES:SUBLANES + c, :] = qkv_ref[bi]
        base = SUBLANES - (CONV_W - 1)
        conv = ext_ref[bi, base:base + c, :] * convw_ref[0:1, :]
        for j in range(1, CONV_W):
            conv = conv + ext_ref[bi, base + j:base + j + c, :] * convw_ref[j:j + 1, :]
        tail = ext_ref[bi, c:c + SUBLANES, :]
        convnew_ref[bi] = tail
        ext_ref[bi, 0:SUBLANES, :] = tail
        acts.append(conv * jax.nn.sigmoid(conv))
        ab = ab_ref[bi]
        g = -jnp.exp(gp[0:1, :]) * jax.nn.softplus(ab + gp[1:2, :])
        beta_alls.append(jax.nn.sigmoid(ab))
        gc_all = jnp.dot(tril, g, precision=HIGHEST, preferred_element_type=F32)
        gc_pad = jnp.concatenate([gc_all, jnp.zeros((LANES - c, LANES), F32)], axis=0) if c < LANES else gc_all
        gc_alls.append(gc_all)
        gc_ts.append(gc_pad.T)

    q, k, v = [], [], []
    for bi, h in heads:
        lo = h * GDN_D
        qh = acts[bi][:, lo:lo + GDN_D]
        kh = acts[bi][:, QK_A + lo:QK_A + lo + GDN_D]
        q.append(qh * lax.rsqrt(jnp.sum(qh * qh, axis=-1, keepdims=True) + EPS) * scale)
        k.append(kh * lax.rsqrt(jnp.sum(kh * kh, axis=-1, keepdims=True) + EPS))
        v.append(acts[bi][:, 2 * QK_A + lo:2 * QK_A + lo + GDN_D])
    beta = [beta_alls[bi][:, GDN_HEADS + h:GDN_HEADS + h + 1] for bi, h in heads]
    gc = [gc_alls[bi][:, h:h + 1] for bi, h in heads]
    gc_last = [gc_alls[bi][c - 1:c, h:h + 1] for bi, h in heads]
    heads = range(len(heads))
    decay = [jnp.exp(jnp.where(incl, gc[h] - gc_ts[h // GDN_HEADS][h % GDN_HEADS:h % GDN_HEADS + 1, 0:c],
                               -jnp.inf)) for h in heads]
    egc = [jnp.exp(gc[h]) for h in heads]
    kb = [k[h] * beta[h] for h in heads]
    qk = [lax.dot_general(jnp.concatenate([q[h], kb[h]], axis=0).astype(BF16), k[h].astype(BF16),
                          (((1,), (1,)), ((), ())), preferred_element_type=F32) for h in heads]
    a_qk = [jnp.where(incl, qk[h][0:c] * decay[h], 0.0) for h in heads]
    tinv = _tri_inverse([jnp.where(strict, qk[h][c:2 * c] * decay[h], 0.0) for h in heads], c)
    uw = [jnp.dot(tinv[h].astype(BF16),
                  jnp.concatenate([v[h] * beta[h], kb[h] * egc[h]], axis=1).astype(BF16),
                  preferred_element_type=F32) for h in heads]
    ws_qs = [jnp.dot(jnp.concatenate([uw[h][:, GDN_D:2 * GDN_D], q[h] * egc[h]], axis=0).astype(BF16),
                     s_old[h].astype(BF16), preferred_element_type=F32) for h in heads]
    v_new = [(uw[h][:, 0:GDN_D] - ws_qs[h][0:c]).astype(BF16) for h in heads]
    o = [ws_qs[h][c:2 * c] + jnp.dot(a_qk[h].astype(BF16), v_new[h], preferred_element_type=F32)
         for h in heads]
    s_new = [s_old[h] * jnp.exp(gc_last[h]) + lax.dot_general(
        (k[h] * jnp.exp(gc_last[h] - gc[h])).astype(BF16), v_new[h], (((0,), (0,)), ((), ())),
        preferred_element_type=F32) for h in heads]
    for bi in range(nb):
        z_all = z_ref[bi]
        gate = z_all * jax.nn.sigmoid(z_all)
        o_ref[bi] = jnp.concatenate([_rms(o[bi * GDN_HEADS + h], norm_ref[...]) for h in range(GDN_HEADS)],
                                    axis=1) * gate
    for h in heads:
        s_ref[h // GDN_HEADS, h % GDN_HEADS] = s_new[h]


def _gdn(proj3, hist8, s0, conv_w, gate_par, gdn_norm, c, nb):
    b, t, _ = proj3.shape
    nchunks = t // c
    kern = functools.partial(_gdn_kernel, c=c)
    return pl.pallas_call(
        kern,
        out_shape=(jax.ShapeDtypeStruct((b, t, V_A), F32),
                   jax.ShapeDtypeStruct((b, SUBLANES, CONV_CH), F32),
                   jax.ShapeDtypeStruct((b, GDN_HEADS, GDN_D, GDN_D), F32)),
        grid=(b // nb, nchunks),
        in_specs=[pl.BlockSpec((nb, c, CONV_CH), lambda i, j: (i, j, 0)),
                  pl.BlockSpec((nb, c, V_A), lambda i, j: (i, j, COL_Z)),
                  pl.BlockSpec((nb, c, LANES), lambda i, j: (i, j, COL_AB)),
                  pl.BlockSpec((nb, SUBLANES, CONV_CH), lambda i, j: (i, 0, 0)),
                  pl.BlockSpec((nb, GDN_HEADS, GDN_D, GDN_D), lambda i, j: (i, 0, 0, 0)),
                  pl.BlockSpec((CONV_W, CONV_CH), lambda i, j: (0, 0)),
                  pl.BlockSpec((SUBLANES, LANES), lambda i, j: (0, 0)),
                  pl.BlockSpec((1, GDN_D), lambda i, j: (0, 0))],
        out_specs=(pl.BlockSpec((nb, c, V_A), lambda i, j: (i, j, 0)),
                   pl.BlockSpec((nb, SUBLANES, CONV_CH), lambda i, j: (i, 0, 0)),
                   pl.BlockSpec((nb, GDN_HEADS, GDN_D, GDN_D), lambda i, j: (i, 0, 0, 0))),
        scratch_shapes=[pltpu.VMEM((nb, SUBLANES + c, CONV_CH), F32)],
        compiler_params=_cparams(("arbitrary", "arbitrary")),
        name="gdn_mixer",
    )(proj3, proj3, proj3, hist8, s0, conv_w, gate_par, gdn_norm)


def _swa_kernel(*refs, nq, nsub, piece_rows, masked):
    npieces = len(piece_rows)
    q_ref = refs[0]
    k_refs = refs[1:1 + npieces]
    v_refs = refs[1 + npieces:1 + 2 * npieces]
    bias_ref, sink_ref, o_ref = refs[1 + 2 * npieces:]
    nk = WIN_PAST + nq
    kcat = jnp.concatenate([r[...] for r in k_refs], axis=0).astype(BF16)
    vcat = jnp.concatenate([r[...] for r in v_refs], axis=0).astype(BF16)
    q = q_ref[...]
    kidx = lax.broadcasted_iota(jnp.int32, (SWA_GROUP * nq, nk), 1)
    blocks = [(i, kv) for i in range(nsub) for kv in range(SWA_KV_HEADS)]
    scores = []
    for i, kv in blocks:
        qs = jnp.concatenate(
            [q[i * nq:(i + 1) * nq, (kv * SWA_GROUP + g) * SWA_DH:(kv * SWA_GROUP + g + 1) * SWA_DH]
             for g in range(SWA_GROUP)], axis=0).astype(BF16)
        kh = kcat[i * nq:i * nq + nk, kv * SWA_DH:(kv + 1) * SWA_DH]
        s = lax.dot_general(qs, kh, (((1,), (1,)), ((), ())), preferred_element_type=F32)
        s = s * SWA_DH ** -0.5 + bias_ref[kv]
        if masked:
            first = (pl.program_id(1) * nsub + i) * nq - WIN_PAST
            s = jnp.where(kidx + first >= 0, s, -jnp.inf)
        scores.append(s)
    sinks = [sink_ref[kv] for _, kv in blocks]
    maxes = [jnp.maximum(jnp.max(s, axis=-1, keepdims=True), sk) for s, sk in zip(scores, sinks)]
    probs = [jnp.exp(s - m) for s, m in zip(scores, maxes)]
    dens = [jnp.sum(p, axis=-1, keepdims=True) + jnp.exp(sk - m) for p, sk, m in zip(probs, sinks, maxes)]
    outs = [jnp.dot((p / den).astype(BF16), vcat[i * nq:i * nq + nk, kv * SWA_DH:(kv + 1) * SWA_DH],
                    preferred_element_type=F32) for (i, kv), p, den in zip(blocks, probs, dens)]
    out_rows = []
    for i in range(nsub):
        heads = []
        for kv in range(SWA_KV_HEADS):
            o = outs[i * SWA_KV_HEADS + kv]
            heads += [o[g * nq:(g + 1) * nq, :] for g in range(SWA_GROUP)]
        out_rows.append(jnp.concatenate(heads, axis=1))
    o_ref[...] = out_rows[0] if nsub == 1 else jnp.concatenate(out_rows, axis=0)


def _rel_bucket(nq, nk):
    rel = jnp.arange(nk)[None, :] - WIN_PAST - jnp.arange(nq)[:, None]
    nb = REL_BUCKETS // 2
    max_exact = nb // 2
    n = jnp.abs(rel)
    large = max_exact + (jnp.log(jnp.maximum(n, 1).astype(F32) / max_exact)
                         / math.log(REL_MAX_DIST / max_exact) * (nb - max_exact)).astype(jnp.int32)
    large = jnp.minimum(large, nb - 1)
    return jnp.where(rel > 0, nb, 0) + jnp.where(n < max_exact, n, large)


def _swa_tables(table, sinks, nq, nk):
    onehot = (_rel_bucket(nq, nk)[:, :, None] == jnp.arange(REL_BUCKETS)[None, None, :]).astype(F32)
    bias = jnp.einsum("qkb,bh->qkh", onehot, table.astype(F32), precision=HIGHEST)
    bias = jnp.transpose(bias, (2, 0, 1)).astype(F32)
    bias = bias.reshape(SWA_KV_HEADS, SWA_GROUP * nq, nk)
    sk = jnp.broadcast_to(sinks.astype(F32).reshape(SWA_KV_HEADS, SWA_GROUP, 1, 1),
                          (SWA_KV_HEADS, SWA_GROUP, nq, 1)).reshape(SWA_KV_HEADS, SWA_GROUP * nq, 1)
    return bias, sk


def _swa_prompt(proj3, table, sinks):
    b, t, _ = proj3.shape
    nq = CHUNK
    nsub = SWA_CHUNKS_PER_STEP
    rows = nsub * nq
    assert rows % WIN_PAST == 0 and t % rows == 0
    past_per_step = rows // WIN_PAST
    bias, sk = _swa_tables(table, sinks, nq, WIN_PAST + nq)
    kern = functools.partial(_swa_kernel, nq=nq, nsub=nsub, piece_rows=(WIN_PAST, rows), masked=True)

    def past_spec(colblk):
        return pl.BlockSpec((None, WIN_PAST, LANES),
                            lambda i, j: (i, jnp.maximum(j * past_per_step - 1, 0), colblk))

    def cur_spec(colblk):
        return pl.BlockSpec((None, rows, LANES), lambda i, j: (i, j, colblk))

    in_specs = [pl.BlockSpec((None, rows, Q_B), lambda i, j: (i, j, COL_QB)),
                past_spec(COL_KB), cur_spec(COL_KB), past_spec(COL_VB), cur_spec(COL_VB),
                pl.BlockSpec(bias.shape, lambda i, j: (0, 0, 0)),
                pl.BlockSpec(sk.shape, lambda i, j: (0, 0, 0))]
    return pl.pallas_call(
        kern,
        out_shape=jax.ShapeDtypeStruct((b, t, Q_B), F32),
        grid=(b, t // rows),
        in_specs=in_specs,
        out_specs=pl.BlockSpec((None, rows, Q_B), lambda i, j: (i, j, 0)),
        compiler_params=_cparams(("arbitrary", "arbitrary")),
        name="swa_prompt",
    )(proj3, proj3, proj3, proj3, proj3, bias, sk)


def _swa_sample(proj3, k_past, v_past, table, sinks):
    b, t, _ = proj3.shape
    bias, sk = _swa_tables(table, sinks, t, WIN_PAST + t)
    kern = functools.partial(_swa_kernel, nq=t, nsub=1, piece_rows=(WIN_PAST, t), masked=False)
    in_specs = [pl.BlockSpec((None, t, Q_B), lambda i, j: (i, 0, COL_QB)),
                pl.BlockSpec((None, WIN_PAST, LANES), lambda i, j: (i, 0, 0)),
                pl.BlockSpec((None, t, LANES), lambda i, j: (i, 0, COL_KB)),
                pl.BlockSpec((None, WIN_PAST, LANES), lambda i, j: (i, 0, 0)),
                pl.BlockSpec((None, t, LANES), lambda i, j: (i, 0, COL_VB)),
                pl.BlockSpec(bias.shape, lambda i, j: (0, 0, 0)),
                pl.BlockSpec(sk.shape, lambda i, j: (0, 0, 0))]
    return pl.pallas_call(
        kern,
        out_shape=jax.ShapeDtypeStruct((b, t, Q_B), F32),
        grid=(b, 1),
        in_specs=in_specs,
        out_specs=pl.BlockSpec((None, t, Q_B), lambda i, j: (i, 0, 0)),
        compiler_params=_cparams(("arbitrary", "arbitrary")),
        name="swa_sample",
    )(proj3, k_past, proj3, v_past, proj3, bias, sk)


def _outproj_kernel(x_ref, oa_ref, ob_ref, wa_ref, wb_ref, g_ref, wq_ref, x1_ref, q_ref):
    x1 = (x_ref[...]
          + jnp.dot(oa_ref[...].astype(BF16), wa_ref[...], preferred_element_type=F32)
          + jnp.dot(ob_ref[...].astype(BF16), wb_ref[...], preferred_element_type=F32))
    x1_ref[...] = x1
    hc = _rms(x1, g_ref[...]).astype(BF16)
    q_ref[...] = jnp.dot(hc, wq_ref[...], preferred_element_type=F32)


def _outproj(x, o_a, o_b, w_out, norm_cross, w_mq, tm):
    m = x.shape[0]
    return pl.pallas_call(
        _outproj_kernel,
        out_shape=(jax.ShapeDtypeStruct((m, D_MODEL), F32), jax.ShapeDtypeStruct((m, MEM_W), F32)),
        grid=(m // tm,),
        in_specs=[pl.BlockSpec((tm, D_MODEL), lambda i: (i, 0)),
                  pl.BlockSpec((tm, V_A), lambda i: (i, 0)),
                  pl.BlockSpec((tm, Q_B), lambda i: (i, 0)),
                  pl.BlockSpec((V_A, D_MODEL), lambda i: (0, 0)),
                  pl.BlockSpec((Q_B, D_MODEL), lambda i: (1, 0)),
                  pl.BlockSpec((1, D_MODEL), lambda i: (0, 0)),
                  pl.BlockSpec((D_MODEL, MEM_W), lambda i: (0, 0))],
        out_specs=(pl.BlockSpec((tm, D_MODEL), lambda i: (i, 0)),
                   pl.BlockSpec((tm, MEM_W), lambda i: (i, 0))),
        compiler_params=_cparams(("arbitrary",)),
        name="outproj",
    )(x, o_a, o_b, w_out, w_out, norm_cross, w_mq)


def _cross_router_kernel(x1_ref, q_ref, mk_ref, mv_ref, wo_ref, g_ref, wr_ref, br_ref,
                         x2_ref, hp_ref, route_ref):
    nb, tm = x1_ref.shape[0], x1_ref.shape[1]
    rows = []
    for bi in range(nb):
        q = q_ref[bi]
        outs = []
        for h in range(MEM_HEADS):
            sl = slice(h * MEM_DH, (h + 1) * MEM_DH)
            mk = mk_ref[bi, pl.ds(h, MEM_LEN, stride=MEM_HEADS), :].astype(BF16)
            mv = mv_ref[bi, pl.ds(h, MEM_LEN, stride=MEM_HEADS), :].astype(BF16)
            s = lax.dot_general(q[:, sl].astype(BF16), mk, (((1,), (1,)), ((), ())),
                                preferred_element_type=F32) * MEM_DH ** -0.5
            m = jnp.max(s, axis=-1, keepdims=True)
            p = jnp.exp(s - m)
            p = (p / jnp.sum(p, axis=-1, keepdims=True)).astype(BF16)
            outs.append(jnp.dot(p, mv, preferred_element_type=F32))
        rows.append(jnp.concatenate(outs, axis=1))
    att = (rows[0] if nb == 1 else jnp.concatenate(rows, axis=0)).astype(BF16)
    x2 = x1_ref[...].reshape(nb * tm, D_MODEL) + jnp.dot(att, wo_ref[...], preferred_element_type=F32)
    x2_ref[...] = x2.reshape(nb, tm, D_MODEL)
    hf = _rms(x2, g_ref[...])
    packed = _pack_halves(hf[:, 0:HALF], hf[:, HALF:D_MODEL])
    for bi in range(nb):
        for j in range(XROW_TILES):
            hp_ref[bi, pl.ds(j, tm, stride=XROW_TILES), :] = packed[bi * tm:(bi + 1) * tm, j * LANES:(j + 1) * LANES]
    hf_hi = hf.astype(BF16)
    hf_lo = (hf - hf_hi.astype(F32)).astype(BF16)
    wr = wr_ref[...]
    r_hi = jnp.dot(hf_hi, wr, preferred_element_type=F32)
    r_lo = jnp.dot(hf_lo, wr[:, 0:LANES], preferred_element_type=F32)
    logits = r_hi[:, 0:LANES] + r_hi[:, LANES:2 * LANES] + r_lo + br_ref[...]
    lane = lax.broadcasted_iota(jnp.int32, logits.shape, 1)
    lanef = lane.astype(F32)
    l = jnp.where(lane < N_EXPERTS, logits, -jnp.inf)
    vals, idxs = [], []
    for _ in range(TOP_K):
        m = jnp.max(l, axis=-1, keepdims=True)
        idx = jnp.min(jnp.where(l == m, lanef, float(LANES)), axis=-1, keepdims=True)
        vals.append(m)
        idxs.append(idx)
        l = jnp.where(lanef == idx, -jnp.inf, l)
    es = [jnp.exp(v - vals[0]) for v in vals]
    den = es[0] + es[1] + es[2] + es[3]
    route = jnp.zeros(logits.shape, F32)
    for k in range(TOP_K):
        route = jnp.where(lane == k, idxs[k], route)
        route = jnp.where(lane == TOP_K + k, es[k] / den, route)
    route_ref[...] = route.reshape(nb, tm, LANES)


def _split_bf16(w):
    w_hi = w.astype(BF16)
    w_lo = (w - w_hi.astype(F32)).astype(BF16)
    return jnp.concatenate([w_hi, w_lo], axis=1)


def _cross_router(x1, qm, mk, mv, w_mo, norm_ffn, w_router, b_router, nb, tm):
    b, t, _ = x1.shape
    nt = t // tm
    return pl.pallas_call(
        _cross_router_kernel,
        out_shape=(jax.ShapeDtypeStruct((b, t, D_MODEL), F32),
                   jax.ShapeDtypeStruct((b, t * XROW_TILES, LANES), jnp.uint32),
                   jax.ShapeDtypeStruct((b, t, LANES), F32)),
        grid=(b // nb, nt),
        in_specs=[pl.BlockSpec((nb, tm, D_MODEL), lambda i, j: (i, j, 0)),
                  pl.BlockSpec((nb, tm, MEM_W), lambda i, j: (i, j, 0)),
                  pl.BlockSpec((nb, MEM_LEN * MEM_HEADS, MEM_DH), lambda i, j: (i, 0, 0)),
                  pl.BlockSpec((nb, MEM_LEN * MEM_HEADS, MEM_DH), lambda i, j: (i, 0, 0)),
                  pl.BlockSpec((MEM_W, D_MODEL), lambda i, j: (0, 0)),
                  pl.BlockSpec((1, D_MODEL), lambda i, j: (0, 0)),
                  pl.BlockSpec((D_MODEL, 2 * LANES), lambda i, j: (0, 0)),
                  pl.BlockSpec((1, LANES), lambda i, j: (0, 0))],
        out_specs=(pl.BlockSpec((nb, tm, D_MODEL), lambda i, j: (i, j, 0)),
                   pl.BlockSpec((nb, tm * XROW_TILES, LANES), lambda i, j: (i, j, 0)),
                   pl.BlockSpec((nb, tm, LANES), lambda i, j: (i, j, 0))),
        compiler_params=_cparams(("arbitrary", "arbitrary")),
        name="cross_router",
    )(x1, qm, mk, mv, w_mo, norm_ffn, w_router, b_router)


def _row_tile(idx, tiles):
    return pl.ds(pl.multiple_of(idx * tiles, tiles), tiles)


def _dispatch_kernel(*refs, tokens, first_group):
    if first_group:
        pos_ref, tail_ref, hp_ref, xs_hbm, zero_ref, sem = refs
    else:
        pos_ref, tail_ref, hp_ref, _, xs_hbm, zero_ref, sem = refs

    def zero_tails():
        zero_ref[...] = jnp.zeros(zero_ref.shape, zero_ref.dtype)

        def tail_copy(e):
            dst = xs_hbm.at[pl.ds(pl.multiple_of(tail_ref[e] * XROW_TILES, XROW_TILES), MOE_SUB * XROW_TILES)]
            return pltpu.make_async_copy(zero_ref, dst, sem.at[0])

        for e in range(N_EXPERTS):
            pl.when(tail_ref[e] >= 0)(lambda e=e: tail_copy(e).start())
        for e in range(N_EXPERTS):
            pl.when(tail_ref[e] >= 0)(lambda e=e: tail_copy(e).wait())

    if first_group:
        pl.when(pl.program_id(0) == 0)(zero_tails)

    def issue(t, carry):
        src = hp_ref.at[_row_tile(t, XROW_TILES)]
        for k in range(TOP_K):
            dst = xs_hbm.at[_row_tile(pos_ref[t * TOP_K + k], XROW_TILES)]
            pltpu.make_async_copy(src, dst, sem.at[1]).start(priority=k % 2)
        return carry

    lax.fori_loop(0, tokens, issue, 0, unroll=4)

    def drain(t, carry):
        for k in range(TOP_K):
            pltpu.make_async_copy(hp_ref.at[_row_tile(0, XROW_TILES)], xs_hbm.at[_row_tile(0, XROW_TILES)],
                                  sem.at[1]).wait()
        return carry

    lax.fori_loop(0, tokens, drain, 0)


def _dispatch(hp, pos_flat, tail_start, rows, tokens, xs_prev=None):
    t = hp.shape[0] // XROW_TILES
    first_group = xs_prev is None
    kern = functools.partial(_dispatch_kernel, tokens=tokens, first_group=first_group)
    in_specs = [pl.BlockSpec((tokens * TOP_K,), lambda i: (i,), memory_space=pltpu.SMEM),
                pl.BlockSpec((N_EXPERTS,), lambda i: (0,), memory_space=pltpu.SMEM),
                pl.BlockSpec((tokens * XROW_TILES, LANES), lambda i: (i, 0))]
    args = [pos_flat, tail_start, hp]
    if not first_group:
        in_specs.append(pl.BlockSpec(memory_space=pl.ANY))
        args.append(xs_prev)
    return pl.pallas_call(
        kern,
        out_shape=jax.ShapeDtypeStruct((rows * XROW_TILES, LANES), jnp.uint32),
        grid=(t // tokens,),
        in_specs=in_specs,
        out_specs=pl.BlockSpec(memory_space=pl.ANY),
        scratch_shapes=[pltpu.VMEM((MOE_SUB * XROW_TILES, LANES), jnp.uint32),
                        pltpu.SemaphoreType.DMA((2,))],
        input_output_aliases={} if first_group else {3: 0},
        compiler_params=pltpu.CompilerParams(dimension_semantics=("arbitrary",),
                                             vmem_limit_bytes=VMEM_LIMIT_BYTES, has_side_effects=True),
        name="moe_dispatch",
    )(*args)


def _moe_kernel(ie_ref, is_ref, ns_ref, xs_hbm, wg_ref, bg_ref, wu_ref, bu_ref, wd_ref, bd_ref,
                ys_hbm, xu_ref, acc_ref, ystage_ref, in_sem, out_sem):
    w = pl.program_id(0)
    f = pl.program_id(1)
    nsub = ns_ref[w]
    start = is_ref[w]

    def sub_rows(s):
        return pl.ds(pl.multiple_of(s * MOE_SUB, MOE_SUB), MOE_SUB)

    nstage = nsub * (MOE_SUB // MOE_STAGE)

    def stage_rows(s):
        return pl.ds(pl.multiple_of(s * MOE_STAGE, MOE_STAGE), MOE_STAGE)

    def hbm_rows(first, s, tiles):
        return pl.ds(pl.multiple_of((first + s * MOE_STAGE) * tiles, MOE_STAGE * tiles), MOE_STAGE * tiles)

    def load(first, s):
        dst = xu_ref.at[pl.ds(pl.multiple_of(s * MOE_STAGE * XROW_TILES, MOE_STAGE * XROW_TILES),
                              MOE_STAGE * XROW_TILES)]
        return pltpu.make_async_copy(xs_hbm.at[hbm_rows(first, s, XROW_TILES)], dst, in_sem.at[0])

    def fetch_item(first, stages):
        def issue(s, carry):
            load(first, s).start()
            return carry

        lax.fori_loop(0, stages, issue, 0)

    pl.when(jnp.logical_and(jnp.logical_and(w == 0, f == 0), nsub > 0))(lambda: fetch_item(start, nstage))

    @pl.when(jnp.logical_and(f == 0, nsub > 0))
    def _():
        def landed(s, carry):
            load(start, s).wait()
            return carry

        lax.fori_loop(0, nstage, landed, 0)

        def init(s, carry):
            acc_ref[sub_rows(s), :] = jnp.broadcast_to(bd_ref[...], (MOE_SUB, D_MODEL))
            return carry

        lax.fori_loop(0, nsub, init, 0)

    def x_tile(s):
        base = pl.multiple_of(s * MOE_SUB * XROW_TILES, MOE_SUB * XROW_TILES)
        lows, highs = [], []
        for j in range(XROW_TILES):
            lo, hi = _unpack_halves(xu_ref[pl.ds(base + j, MOE_SUB, stride=XROW_TILES), :])
            lows.append(lo.astype(BF16))
            highs.append(hi.astype(BF16))
        return jnp.concatenate(lows + highs, axis=1)

    @pl.when(nsub > 0)
    def _():
        bg = bg_ref[...]
        bu = bu_ref[...]

        def activations(s):
            x = x_tile(s)
            gt = jnp.dot(x, wg_ref[...].astype(BF16), preferred_element_type=F32) + bg
            up = jnp.dot(x, wu_ref[...].astype(BF16), preferred_element_type=F32) + bu
            gt = jnp.minimum(gt, SWIGLU_LIMIT)
            up = jnp.clip(up, -SWIGLU_LIMIT, SWIGLU_LIMIT)
            return (gt * jax.nn.sigmoid(SWIGLU_ALPHA * gt) * (up + 1.0)).astype(BF16)

        def accumulate(s, a):
            acc_ref[sub_rows(s), :] += jnp.dot(a, wd_ref[...].astype(BF16), preferred_element_type=F32)

        def body(s, carry):
            accumulate(s, activations(s))
            return carry

        lax.fori_loop(0, nsub, body, 0)

    @pl.when(jnp.logical_and(f == MOE_NF - 1, nsub > 0))
    def _():
        nxt = jnp.minimum(w + 1, pl.num_programs(0) - 1)
        nxt_stages = jnp.where(w + 1 < pl.num_programs(0), ns_ref[nxt], 0) * (MOE_SUB // MOE_STAGE)
        fetch_item(is_ref[nxt], nxt_stages)

        def store(s):
            slot = s % MOE_OUT_SLOTS
            return pltpu.make_async_copy(ystage_ref.at[slot], ys_hbm.at[hbm_rows(start, s, YROW_TILES)],
                                         out_sem.at[slot])

        def write(s, carry):
            pl.when(s >= MOE_OUT_SLOTS)(lambda: store(s - MOE_OUT_SLOTS).wait())
            for j in range(YROW_TILES):
                lo = acc_ref[stage_rows(s), j * LANES:(j + 1) * LANES]
                hi = acc_ref[stage_rows(s), HALF + j * LANES:HALF + (j + 1) * LANES]
                ystage_ref[s % MOE_OUT_SLOTS, pl.ds(j, MOE_STAGE, stride=YROW_TILES), :] = _pack_halves(lo, hi)
            store(s).start()
            return carry

        lax.fori_loop(0, nstage, write, 0)
        for back in range(1, MOE_OUT_SLOTS + 1):
            pl.when(nstage >= back)(lambda back=back: store(nstage - back).wait())


def _moe(xs, item_expert, item_start, item_nsub, w_gate, b_gate, w_up, b_up, w_down, b_down):
    rows = xs.shape[0] // XROW_TILES
    nitems = item_expert.shape[0]

    def fcol(w, f, ie, st, ns):
        return jnp.where(ns[w] > 0, f, MOE_NF - 1)

    grid_spec = pltpu.PrefetchScalarGridSpec(
        num_scalar_prefetch=3,
        grid=(nitems, MOE_NF),
        in_specs=[pl.BlockSpec(memory_space=pl.ANY),
                  pl.BlockSpec((None, D_MODEL, MOE_TF), lambda w, f, ie, st, ns: (ie[w], 0, fcol(w, f, ie, st, ns))),
                  pl.BlockSpec((None, 1, MOE_TF), lambda w, f, ie, st, ns: (ie[w], 0, fcol(w, f, ie, st, ns))),
                  pl.BlockSpec((None, D_MODEL, MOE_TF), lambda w, f, ie, st, ns: (ie[w], 0, fcol(w, f, ie, st, ns))),
                  pl.BlockSpec((None, 1, MOE_TF), lambda w, f, ie, st, ns: (ie[w], 0, fcol(w, f, ie, st, ns))),
                  pl.BlockSpec((None, MOE_TF, D_MODEL), lambda w, f, ie, st, ns: (ie[w], fcol(w, f, ie, st, ns), 0)),
                  pl.BlockSpec((None, 1, D_MODEL), lambda w, f, ie, st, ns: (ie[w], 0, 0))],
        out_specs=pl.BlockSpec(memory_space=pl.ANY),
        scratch_shapes=[pltpu.VMEM((MOE_ROWS * XROW_TILES, LANES), jnp.uint32),
                        pltpu.VMEM((MOE_ROWS, D_MODEL), F32),
                        pltpu.VMEM((MOE_OUT_SLOTS, MOE_STAGE * YROW_TILES, LANES), jnp.uint32),
                        pltpu.SemaphoreType.DMA((1,)),
                        pltpu.SemaphoreType.DMA((MOE_OUT_SLOTS,))])
    return pl.pallas_call(
        _moe_kernel,
        out_shape=jax.ShapeDtypeStruct((rows * YROW_TILES, LANES), jnp.uint32),
        grid_spec=grid_spec,
        compiler_params=pltpu.CompilerParams(dimension_semantics=("arbitrary", "arbitrary"),
                                             vmem_limit_bytes=MOE_VMEM_LIMIT_BYTES, has_side_effects=True),
        name="moe_experts",
    )(item_expert, item_start, item_nsub, xs, w_gate, b_gate, w_up, b_up, w_down, b_down)


def _combine_kernel(pos_ref, nxt_ref, x2_ref, route_ref, g_ref, ys_hbm, o_ref, buf_ref, sem, *, tokens):
    i = pl.program_id(0)
    slot = i % 2

    def gather(p_ref, dst_slot):
        def issue(t, carry):
            for k in range(TOP_K):
                pltpu.make_async_copy(ys_hbm.at[_row_tile(p_ref[t * TOP_K + k], YROW_TILES)],
                                      buf_ref.at[dst_slot, k, _row_tile(t, YROW_TILES)],
                                      sem.at[dst_slot]).start(priority=k % 2)
            return carry

        lax.fori_loop(0, tokens, issue, 0, unroll=4)

    pl.when(i == 0)(lambda: gather(pos_ref, 0))
    pl.when(i + 1 < pl.num_programs(0))(lambda: gather(nxt_ref, 1 - slot))

    def drain(t, carry):
        for k in range(TOP_K):
            pltpu.make_async_copy(ys_hbm.at[_row_tile(0, YROW_TILES)],
                                  buf_ref.at[slot, k, _row_tile(0, YROW_TILES)], sem.at[slot]).wait()
        return carry

    lax.fori_loop(0, tokens, drain, 0)

    route = route_ref[...]
    gates = [route[:, TOP_K + k:TOP_K + k + 1] for k in range(TOP_K)]
    lows, highs = [], []
    for j in range(YROW_TILES):
        y_lo = x2_ref[:, j * LANES:(j + 1) * LANES]
        y_hi = x2_ref[:, HALF + j * LANES:HALF + (j + 1) * LANES]
        for k in range(TOP_K):
            lo, hi = _unpack_halves(buf_ref[slot, k, pl.ds(j, tokens, stride=YROW_TILES), :])
            y_lo = y_lo + gates[k] * lo
            y_hi = y_hi + gates[k] * hi
        lows.append(y_lo)
        highs.append(y_hi)
    o_ref[...] = _rms(jnp.concatenate(lows + highs, axis=1), g_ref[...])


def _combine(x2, route, final_norm, ys, pos_flat, tokens):
    t = x2.shape[0]
    nsteps = t // tokens
    kern = functools.partial(_combine_kernel, tokens=tokens)
    return pl.pallas_call(
        kern,
        out_shape=jax.ShapeDtypeStruct((t, D_MODEL), F32),
        grid=(nsteps,),
        in_specs=[pl.BlockSpec((tokens * TOP_K,), lambda i: (i,), memory_space=pltpu.SMEM),
                  pl.BlockSpec((tokens * TOP_K,), lambda i: (jnp.minimum(i + 1, nsteps - 1),),
                               memory_space=pltpu.SMEM),
                  pl.BlockSpec((tokens, D_MODEL), lambda i: (i, 0)),
                  pl.BlockSpec((tokens, LANES), lambda i: (i, 0)),
                  pl.BlockSpec((1, D_MODEL), lambda i: (0, 0)),
                  pl.BlockSpec(memory_space=pl.ANY)],
        out_specs=pl.BlockSpec((tokens, D_MODEL), lambda i: (i, 0)),
        scratch_shapes=[pltpu.VMEM((2, TOP_K, tokens * YROW_TILES, LANES), jnp.uint32),
                        pltpu.SemaphoreType.DMA((2,))],
        compiler_params=_cparams(("arbitrary",)),
        name="moe_combine",
    )(pos_flat, pos_flat, x2, route, final_norm, ys)


def _routing_tables(top_i, nitems):
    t = top_i.shape[0]
    sel = jnp.sum((top_i[:, :, None] == jnp.arange(N_EXPERTS, dtype=jnp.int32)[None, None, :]).astype(jnp.int32),
                  axis=1)
    cnt = jnp.sum(sel, axis=0)
    rank = jnp.cumsum(sel, axis=0) - sel
    cnt_pad = ((cnt + MOE_SUB - 1) // MOE_SUB) * MOE_SUB
    off = jnp.cumsum(cnt_pad) - cnt_pad
    pos = jnp.take_along_axis(off[None, :] + rank, top_i, axis=1)
    tail_start = jnp.where(cnt > 0, off + cnt_pad - MOE_SUB, -1)
    items_per = (cnt + MOE_ROWS - 1) // MOE_ROWS
    item_end = jnp.cumsum(items_per)
    total = item_end[-1]
    widx = jnp.arange(nitems, dtype=jnp.int32)
    e_of = jnp.minimum(jnp.searchsorted(item_end, widx, side="right"), N_EXPERTS - 1).astype(jnp.int32)
    j_of = widx - (item_end - items_per)[e_of]
    valid = widx < total
    e_last = e_of[jnp.maximum(total - 1, 0)]
    item_expert = jnp.where(valid, e_of, e_last).astype(jnp.int32)
    item_start = jnp.where(valid, off[e_of] + j_of * MOE_ROWS, 0).astype(jnp.int32)
    rows_left = cnt_pad[e_of] - j_of * MOE_ROWS
    item_nsub = jnp.where(valid, jnp.minimum(rows_left, MOE_ROWS) // MOE_SUB, 0).astype(jnp.int32)
    return pos.reshape(t * TOP_K).astype(jnp.int32), tail_start.astype(jnp.int32), item_expert, item_start, item_nsub


def _trunk(x, conv_hist, s0, swa_fn, mk, mv, gdn_chunk, p):
    b, t, _ = x.shape
    m = b * t
    xf = x.reshape(m, D_MODEL)
    proj = _inproj(xf, p["norm_mix"], p["w_in"], p["w_in_tail"], min(m, 1024))
    proj3 = proj.reshape(b, t, PROJ_COLS)
    hist8 = jnp.pad(conv_hist, ((0, 0), (SUBLANES - (CONV_W - 1), 0), (0, 0)))
    o_a, conv8, s_new = _gdn(proj3, hist8, s0, p["conv_w"], p["gate_par"], p["gdn_norm"], gdn_chunk,
                             GDN_SEQS_PER_STEP)
    o_b = swa_fn(proj3)
    x1, qm = _outproj(xf, o_a.reshape(m, V_A), o_b.reshape(m, Q_B), p["w_out"], p["norm_cross"], p["w_mq"], 256)
    tm = min(t, CROSS_ROWS)
    x2, hp, route = _cross_router(x1.reshape(b, t, D_MODEL), qm.reshape(b, t, MEM_W), mk, mv,
                                  p["w_mo"], p["norm_ffn"], p["w_router"], p["b_router"], CROSS_ROWS // tm, tm)
    return proj3, conv8[:, SUBLANES - (CONV_W - 1):], s_new, x2.reshape(m, D_MODEL), \
        hp.reshape(m * XROW_TILES, LANES), route.reshape(m, LANES)


def kernel(x_prompt, x_sample, cache_conv, state_gdn, cache_swa_k, cache_swa_v, cache_mem_k, cache_mem_v, mem_prompt, norm_mix, w_in, conv_w, gdn_a_log, gdn_dt_bias, gdn_norm, swa_sinks, rel_bias_table, w_out, norm_cross, norm_mem, w_mq, w_mk, w_mv, w_mo, norm_ffn, w_router, b_router, w_gate, b_gate, w_up, b_up, w_down, b_down, final_norm):
    depth = norm_mix.shape[0]
    assert depth == 1, "kernel is written for the single-layer trunk"
    bp, sp, _ = x_prompt.shape
    bs, ss, _ = x_sample.shape
    l = 0
    w = w_in[l]
    n_ab = 2 * GDN_HEADS
    w_tail = jnp.concatenate(
        [w[:, PROJ_MAIN + n_ab:], w[:, PROJ_MAIN:PROJ_MAIN + n_ab],
         jnp.zeros((D_MODEL, PROJ_COLS - w.shape[1]), w.dtype)], axis=1).astype(BF16)
    gate_par = jnp.zeros((SUBLANES, LANES), F32)
    gate_par = gate_par.at[0, :GDN_HEADS].set(gdn_a_log[l]).at[1, :GDN_HEADS].set(gdn_dt_bias[l])
    p = dict(
        norm_mix=norm_mix[l].reshape(1, D_MODEL), w_in=w, w_in_tail=w_tail, conv_w=conv_w[l], gate_par=gate_par,
        gdn_norm=gdn_norm[l].reshape(1, GDN_D), w_out=w_out[l].astype(BF16),
        norm_cross=norm_cross[l].reshape(1, D_MODEL), w_mq=w_mq[l].astype(BF16), w_mo=w_mo[l].astype(BF16),
        norm_ffn=norm_ffn[l].reshape(1, D_MODEL),
        w_router=_split_bf16(jnp.pad(w_router[l], ((0, 0), (0, LANES - N_EXPERTS)))),
        b_router=jnp.pad(b_router[l], (0, LANES - N_EXPERTS)).reshape(1, LANES))

    w_mkv = jnp.concatenate([w_mk[l], w_mv[l]], axis=1).astype(BF16)
    mkv = _norm_matmul(mem_prompt.reshape(bp * MEM_LEN, D_MODEL), norm_mem[l].reshape(1, D_MODEL), w_mkv,
                       min(bp * MEM_LEN, 1024), MEMKV_TN)
    mk_p = mkv[:, :MEM_W].reshape(bp, MEM_LEN * MEM_HEADS, MEM_DH)
    mv_p = mkv[:, MEM_W:].reshape(bp, MEM_LEN * MEM_HEADS, MEM_DH)

    zero_hist = jnp.zeros((bp, CONV_W - 1, CONV_CH), F32)
    zero_state = jnp.zeros((bp, GDN_HEADS, GDN_D, GDN_D), F32)
    swa_p = functools.partial(_swa_prompt, table=rel_bias_table, sinks=swa_sinks[l])
    proj_p, conv_p, st_p, x2_p, hp_p, route_p = _trunk(x_prompt, zero_hist, zero_state, swa_p, mk_p, mv_p, CHUNK, p)

    k_past = cache_swa_k[l].reshape(bs, WIN_PAST, KV_B)
    v_past = cache_swa_v[l].reshape(bs, WIN_PAST, KV_B)
    swa_s = functools.partial(_swa_sample, k_past=k_past, v_past=v_past, table=rel_bias_table, sinks=swa_sinks[l])
    proj_s, conv_s, st_s, x2_s, hp_s, route_s = _trunk(
        x_sample, cache_conv[l], state_gdn[l], swa_s,
        cache_mem_k[l].reshape(bs, MEM_LEN * MEM_HEADS, MEM_DH),
        cache_mem_v[l].reshape(bs, MEM_LEN * MEM_HEADS, MEM_DH), ss, p)

    mp, ms = bp * sp, bs * ss
    ntok = mp + ms
    top_i = jnp.concatenate([route_p[:, :TOP_K], route_s[:, :TOP_K]], axis=0).astype(jnp.int32)
    rows = ntok * TOP_K + N_EXPERTS * MOE_SUB
    nitems = (ntok * TOP_K) // MOE_ROWS + N_EXPERTS
    pos_flat, tail_start, item_expert, item_start, item_nsub = _routing_tables(top_i, nitems)
    pos_p, pos_s = pos_flat[:mp * TOP_K], pos_flat[mp * TOP_K:]
    xs = _dispatch(hp_p, pos_p, tail_start, rows, 512)
    xs = _dispatch(hp_s, pos_s, tail_start, rows, 512, xs_prev=xs)
    ys = _moe(xs, item_expert, item_start, item_nsub,
              w_gate[l], b_gate[l].reshape(N_EXPERTS, 1, D_FF), w_up[l], b_up[l].reshape(N_EXPERTS, 1, D_FF),
              w_down[l], b_down[l].reshape(N_EXPERTS, 1, D_MODEL))
    fnorm = final_norm.reshape(1, D_MODEL)
    y_p = _combine(x2_p, route_p, fnorm, ys, pos_p, 256).reshape(bp, sp, D_MODEL)
    y_s = _combine(x2_s, route_s, fnorm, ys, pos_s, 256).reshape(bs, ss, D_MODEL)

    def kv_window(proj3, col, past=None):
        new = proj3[:, :, col * LANES:(col + 1) * LANES]
        full = new if past is None else jnp.concatenate([past, new], axis=1)
        win = full[:, -WIN_PAST:]
        return win.reshape(win.shape[0], WIN_PAST, SWA_KV_HEADS, SWA_DH)[None]

    return (y_p, y_s,
            conv_p[None], st_p[None], kv_window(proj_p, COL_KB), kv_window(proj_p, COL_VB),
            mk_p.reshape(bp, MEM_LEN, MEM_HEADS, MEM_DH)[None], mv_p.reshape(bp, MEM_LEN, MEM_HEADS, MEM_DH)[None],
            conv_s[None], st_s[None], kv_window(proj_s, COL_KB, k_past), kv_window(proj_s, COL_VB, v_past))
```

```python
import functools
import math

import numpy as np
import jax
import jax.numpy as jnp
from jax import lax
from jax.experimental import pallas as pl
from jax.experimental.pallas import tpu as pltpu

F32 = jnp.float32
BF16 = jnp.bfloat16
HIGHEST = lax.Precision.HIGHEST

D_MODEL = 2048
CHUNK = 64
GDN_HEADS = 8
GDN_D = 128
CONV_W = 4
SWA_HEADS = 16
SWA_KV_HEADS = 2
SWA_GROUP = SWA_HEADS // SWA_KV_HEADS
SWA_DH = 64
WIN_PAST = 128
REL_BUCKETS = 32
REL_MAX_DIST = 128
MEM_LEN = 256
MEM_HEADS = 4
MEM_DH = 128
N_EXPERTS = 32
TOP_K = 4
D_FF = D_MODEL
SWIGLU_LIMIT = 7.0
SWIGLU_ALPHA = 1.702
EPS = 1e-6

QK_A = GDN_HEADS * GDN_D
V_A = GDN_HEADS * GDN_D
CONV_CH = 2 * QK_A + V_A
Q_B = SWA_HEADS * SWA_DH
KV_B = SWA_KV_HEADS * SWA_DH
MEM_W = MEM_HEADS * MEM_DH

LANES = 128
SUBLANES = 8
VMEM_LIMIT_BYTES = 58 * 1024 * 1024
MOE_VMEM_LIMIT_BYTES = 60 * 1024 * 1024

PROJ_COLS = 5632
PROJ_TN = 1408
MEMKV_TN = 512
COL_Z = CONV_CH // V_A
COL_QB = (CONV_CH + V_A) // Q_B
COL_KB = (CONV_CH + V_A + Q_B) // LANES
COL_VB = COL_KB + 1
COL_AB = COL_KB + 2

MOE_SUB = 256
MOE_ROWS = 1536
MOE_STAGE = 128
MOE_OUT_SLOTS = 4
MOE_TF = 512
MOE_NF = D_FF // MOE_TF
HALF = D_MODEL // 2
NORM_ROWS = 256
SWA_CHUNKS_PER_STEP = 4
CROSS_ROWS = 256
GDN_SEQS_PER_STEP = 4
XROW_TILES = HALF // LANES
YROW_TILES = HALF // LANES


def _cparams(sem):
    return pltpu.CompilerParams(dimension_semantics=sem, vmem_limit_bytes=VMEM_LIMIT_BYTES)


def _rms(x, gain):
    return x * lax.rsqrt(jnp.mean(x * x, axis=-1, keepdims=True) + EPS) * gain


def _pack_halves(lo, hi):
    lo_bits = pltpu.bitcast(lo.astype(BF16).astype(F32), jnp.uint32)
    hi_bits = pltpu.bitcast(hi.astype(BF16).astype(F32), jnp.uint32)
    return (lo_bits >> 16) | (hi_bits & jnp.uint32(0xFFFF0000))


def _unpack_halves(u):
    return pltpu.bitcast(u << 16, F32), pltpu.bitcast(u & jnp.uint32(0xFFFF0000), F32)


def _norm_matmul_kernel(x_ref, g_ref, w_ref, o_ref, h_ref):
    @pl.when(pl.program_id(1) == 0)
    def _():
        def body(r, carry):
            rows = pl.ds(pl.multiple_of(r * NORM_ROWS, NORM_ROWS), NORM_ROWS)
            h_ref[rows, :] = _rms(x_ref[rows, :], g_ref[...]).astype(BF16)
            return carry

        lax.fori_loop(0, x_ref.shape[0] // NORM_ROWS, body, 0)

    o_ref[...] = jnp.dot(h_ref[...], w_ref[...], preferred_element_type=F32)


def _norm_matmul(x, gain, w, tm, tn):
    m, k = x.shape
    n = w.shape[1]
    return pl.pallas_call(
        _norm_matmul_kernel,
        out_shape=jax.ShapeDtypeStruct((m, n), F32),
        grid=(m // tm, n // tn),
        in_specs=[pl.BlockSpec((tm, k), lambda i, j: (i, 0)),
                  pl.BlockSpec((1, k), lambda i, j: (0, 0)),
                  pl.BlockSpec((k, tn), lambda i, j: (0, j))],
        out_specs=pl.BlockSpec((tm, tn), lambda i, j: (i, j)),
        scratch_shapes=[pltpu.VMEM((tm, k), BF16)],
        compiler_params=_cparams(("arbitrary", "arbitrary")),
        name="norm_matmul",
    )(x, gain, w)


def _tri_inverse(lows, c):
    row = lax.broadcasted_iota(jnp.int32, (c, c), 0)
    col = lax.broadcasted_iota(jnp.int32, (c, c), 1)
    eye = jnp.where(row == col, 1.0, 0.0).astype(F32)
    ps = [eye - low for low in lows]
    ms = list(lows)
    span = 1
    while 2 * span < c:
        mbs = [m.astype(BF16) for m in ms]
        ms = [jnp.dot(mb, mb, preferred_element_type=F32) for mb in mbs]
        ps = [p + jnp.dot(p.astype(BF16), m.astype(BF16), preferred_element_type=F32) for p, m in zip(ps, ms)]
        span *= 2
    return ps


def _gdn_kernel(qkv_ref, z_ref, ab_ref, hist_ref, s0_ref, convw_ref, gp_ref, norm_ref,
                o_ref, convnew_ref, s_ref, ext_ref, *, c):
    step = pl.program_id(1)
    nb = qkv_ref.shape[0]

    @pl.when(step == 0)
    def _():
        ext_ref[:, 0:SUBLANES, :] = hist_ref[...]
        s_ref[...] = s0_ref[...]

    row = lax.broadcasted_iota(jnp.int32, (c, c), 0)
    col = lax.broadcasted_iota(jnp.int32, (c, c), 1)
    incl = row >= col
    strict = row > col
    tril = jnp.where(incl, 1.0, 0.0).astype(F32)
    gp = gp_ref[...]
    scale = GDN_D ** -0.5
    heads = [(bi, h) for bi in range(nb) for h in range(GDN_HEADS)]
    s_old = [s_ref[bi, h] for bi, h in heads]

    acts, gc_alls, gc_ts, beta_alls = [], [], [], []
    for bi in range(nb):
        ext_ref[bi, SUBLANES:SUBLANES + c, :] = qkv_ref[bi]
        base = SUBLANES - (CONV_W - 1)
        conv = ext_ref[bi, base:base + c, :] * convw_ref[0:1, :]
        for j in range(1, CONV_W):
            conv = conv + ext_ref[bi, base + j:base + j + c, :] * convw_ref[j:j + 1, :]
        tail = ext_ref[bi, c:c + SUBLANES, :]
        convnew_ref[bi] = tail
        ext_ref[bi, 0:SUBLANES, :] = tail
        acts.append(conv * jax.nn.sigmoid(conv))
        ab = ab_ref[bi]
        g = -jnp.exp(gp[0:1, :]) * jax.nn.softplus(ab + gp[1:2, :])
        beta_alls.append(jax.nn.sigmoid(ab))
        gc_all = jnp.dot(tril, g, precision=HIGHEST, preferred_element_type=F32)
        gc_pad = jnp.concatenate([gc_all, jnp.zeros((LANES - c, LANES), F32)], axis=0) if c < LANES else gc_all
        gc_alls.append(gc_all)
        gc_ts.append(gc_pad.T)

    q, k, v = [], [], []
    for bi, h in heads:
        lo = h * GDN_D
        qh = acts[bi][:, lo:lo + GDN_D]
        kh = acts[bi][:, QK_A + lo:QK_A + lo + GDN_D]
        q.append(qh * lax.rsqrt(jnp.sum(qh * qh, axis=-1, keepdims=True) + EPS) * scale)
        k.append(kh * lax.rsqrt(jnp.sum(kh * kh, axis=-1, keepdims=True) + EPS))
        v.append(acts[bi][:, 2 * QK_A + lo:2 * QK_A + lo + GDN_D])
    beta = [beta_alls[bi][:, GDN_HEADS + h:GDN_HEADS + h + 1] for bi, h in heads]
    gc = [gc_alls[bi][:, h:h + 1] for bi, h in heads]
    gc_last = [gc_alls[bi][c - 1:c, h:h + 1] for bi, h in heads]
    heads = range(len(heads))
    decay = [jnp.exp(jnp.where(incl, gc[h] - gc_ts[h // GDN_HEADS][h % GDN_HEADS:h % GDN_HEADS + 1, 0:c],
                               -jnp.inf)) for h in heads]
    egc = [jnp.exp(gc[h]) for h in heads]
    kb = [k[h] * beta[h] for h in heads]
    qk = [lax.dot_general(jnp.concatenate([q[h], kb[h]], axis=0).astype(BF16), k[h].astype(BF16),
                          (((1,), (1,)), ((), ())), preferred_element_type=F32) for h in heads]
    a_qk = [jnp.where(incl, qk[h][0:c] * decay[h], 0.0) for h in heads]
    tinv = _tri_inverse([jnp.where(strict, qk[h][c:2 * c] * decay[h], 0.0) for h in heads], c)
    uw = [jnp.dot(tinv[h].astype(BF16),
                  jnp.concatenate([v[h] * beta[h], kb[h] * egc[h]], axis=1).astype(BF16),
                  preferred_element_type=F32) for h in heads]
    ws_qs = [jnp.dot(jnp.concatenate([uw[h][:, GDN_D:2 * GDN_D], q[h] * egc[h]], axis=0).astype(BF16),
                     s_old[h].astype(BF16), preferred_element_type=F32) for h in heads]
    v_new = [(uw[h][:, 0:GDN_D] - ws_qs[h][0:c]).astype(BF16) for h in heads]
    o = [ws_qs[h][c:2 * c] + jnp.dot(a_qk[h].astype(BF16), v_new[h], preferred_element_type=F32)
         for h in heads]
    s_new = [s_old[h] * jnp.exp(gc_last[h]) + lax.dot_general(
        (k[h] * jnp.exp(gc_last[h] - gc[h])).astype(BF16), v_new[h], (((0,), (0,)), ((), ())),
        preferred_element_type=F32) for h in heads]
    for bi in range(nb):
        z_all = z_ref[bi]
        gate = z_all * jax.nn.sigmoid(z_all)
        o_ref[bi] = jnp.concatenate([_rms(o[bi * GDN_HEADS + h], norm_ref[...]) for h in range(GDN_HEADS)],
                                    axis=1) * gate
    for h in heads:
        s_ref[h // GDN_HEADS, h % GDN_HEADS] = s_new[h]


def _gdn(proj3, hist8, s0, conv_w, gate_par, gdn_norm, c, nb):
    b, t, _ = proj3.shape
    nchunks = t // c
    kern = functools.partial(_gdn_kernel, c=c)
    return pl.pallas_call(
        kern,
        out_shape=(jax.ShapeDtypeStruct((b, t, V_A), F32),
                   jax.ShapeDtypeStruct((b, SUBLANES, CONV_CH), F32),
                   jax.ShapeDtypeStruct((b, GDN_HEADS, GDN_D, GDN_D), F32)),
        grid=(b // nb, nchunks),
        in_specs=[pl.BlockSpec((nb, c, CONV_CH), lambda i, j: (i, j, 0)),
                  pl.BlockSpec((nb, c, V_A), lambda i, j: (i, j, COL_Z)),
                  pl.BlockSpec((nb, c, LANES), lambda i, j: (i, j, COL_AB)),
                  pl.BlockSpec((nb, SUBLANES, CONV_CH), lambda i, j: (i, 0, 0)),
                  pl.BlockSpec((nb, GDN_HEADS, GDN_D, GDN_D), lambda i, j: (i, 0, 0, 0)),
                  pl.BlockSpec((CONV_W, CONV_CH), lambda i, j: (0, 0)),
                  pl.BlockSpec((SUBLANES, LANES), lambda i, j: (0, 0)),
                  pl.BlockSpec((1, GDN_D), lambda i, j: (0, 0))],
        out_specs=(pl.BlockSpec((nb, c, V_A), lambda i, j: (i, j, 0)),
                   pl.BlockSpec((nb, SUBLANES, CONV_CH), lambda i, j: (i, 0, 0)),
                   pl.BlockSpec((nb, GDN_HEADS, GDN_D, GDN_D), lambda i, j: (i, 0, 0, 0))),
        scratch_shapes=[pltpu.VMEM((nb, SUBLANES + c, CONV_CH), F32)],
        compiler_params=_cparams(("arbitrary", "arbitrary")),
        name="gdn_mixer",
    )(proj3, proj3, proj3, hist8, s0, conv_w, gate_par, gdn_norm)


def _swa_kernel(*refs, nq, nsub, piece_rows, masked):
    npieces = len(piece_rows)
    q_ref = refs[0]
    k_refs = refs[1:1 + npieces]
    v_refs = refs[1 + npieces:1 + 2 * npieces]
    bias_ref, sink_ref, o_ref = refs[1 + 2 * npieces:]
    nk = WIN_PAST + nq
    kcat = jnp.concatenate([r[...] for r in k_refs], axis=0).astype(BF16)
    vcat = jnp.concatenate([r[...] for r in v_refs], axis=0).astype(BF16)
    q = q_ref[...]
    kidx = lax.broadcasted_iota(jnp.int32, (SWA_GROUP * nq, nk), 1)
    blocks = [(i, kv) for i in range(nsub) for kv in range(SWA_KV_HEADS)]
    scores = []
    for i, kv in blocks:
        qs = jnp.concatenate(
            [q[i * nq:(i + 1) * nq, (kv * SWA_GROUP + g) * SWA_DH:(kv * SWA_GROUP + g + 1) * SWA_DH]
             for g in range(SWA_GROUP)], axis=0).astype(BF16)
        kh = kcat[i * nq:i * nq + nk, kv * SWA_DH:(kv + 1) * SWA_DH]
        s = lax.dot_general(qs, kh, (((1,), (1,)), ((), ())), preferred_element_type=F32)
        s = s * SWA_DH ** -0.5 + bias_ref[kv]
        if masked:
            first = (pl.program_id(1) * nsub + i) * nq - WIN_PAST
            s = jnp.where(kidx + first >= 0, s, -jnp.inf)
        scores.append(s)
    sinks = [sink_ref[kv] for _, kv in blocks]
    maxes = [jnp.maximum(jnp.max(s, axis=-1, keepdims=True), sk) for s, sk in zip(scores, sinks)]
    probs = [jnp.exp(s - m) for s, m in zip(scores, maxes)]
    dens = [jnp.sum(p, axis=-1, keepdims=True) + jnp.exp(sk - m) for p, sk, m in zip(probs, sinks, maxes)]
    outs = [jnp.dot((p / den).astype(BF16), vcat[i * nq:i * nq + nk, kv * SWA_DH:(kv + 1) * SWA_DH],
                    preferred_element_type=F32) for (i, kv), p, den in zip(blocks, probs, dens)]
    out_rows = []
    for i in range(nsub):
        heads = []
        for kv in range(SWA_KV_HEADS):
            o = outs[i * SWA_KV_HEADS + kv]
            heads += [o[g * nq:(g + 1) * nq, :] for g in range(SWA_GROUP)]
        out_rows.append(jnp.concatenate(heads, axis=1))
    o_ref[...] = out_rows[0] if nsub == 1 else jnp.concatenate(out_rows, axis=0)


def _rel_bucket(nq, nk):
    rel = jnp.arange(nk)[None, :] - WIN_PAST - jnp.arange(nq)[:, None]
    nb = REL_BUCKETS // 2
    max_exact = nb // 2
    n = jnp.abs(rel)
    large = max_exact + (jnp.log(jnp.maximum(n, 1).astype(F32) / max_exact)
                         / math.log(REL_MAX_DIST / max_exact) * (nb - max_exact)).astype(jnp.int32)
    large = jnp.minimum(large, nb - 1)
    return jnp.where(rel > 0, nb, 0) + jnp.where(n < max_exact, n, large)


def _swa_tables(table, sinks, nq, nk):
    onehot = (_rel_bucket(nq, nk)[:, :, None] == jnp.arange(REL_BUCKETS)[None, None, :]).astype(F32)
    bias = jnp.einsum("qkb,bh->qkh", onehot, table.astype(F32), precision=HIGHEST)
    bias = jnp.transpose(bias, (2, 0, 1)).astype(F32)
    bias = bias.reshape(SWA_KV_HEADS, SWA_GROUP * nq, nk)
    sk = jnp.broadcast_to(sinks.astype(F32).reshape(SWA_KV_HEADS, SWA_GROUP, 1, 1),
                          (SWA_KV_HEADS, SWA_GROUP, nq, 1)).reshape(SWA_KV_HEADS, SWA_GROUP * nq, 1)
    return bias, sk


def _swa_prompt(proj3, table, sinks):
    b, t, _ = proj3.shape
    nq = CHUNK
    nsub = SWA_CHUNKS_PER_STEP
    rows = nsub * nq
    assert rows % WIN_PAST == 0 and t % rows == 0
    past_per_step = rows // WIN_PAST
    bias, sk = _swa_tables(table, sinks, nq, WIN_PAST + nq)
    kern = functools.partial(_swa_kernel, nq=nq, nsub=nsub, piece_rows=(WIN_PAST, rows), masked=True)

    def past_spec(colblk):
        return pl.BlockSpec((None, WIN_PAST, LANES),
                            lambda i, j: (i, jnp.maximum(j * past_per_step - 1, 0), colblk))

    def cur_spec(colblk):
        return pl.BlockSpec((None, rows, LANES), lambda i, j: (i, j, colblk))

    in_specs = [pl.BlockSpec((None, rows, Q_B), lambda i, j: (i, j, COL_QB)),
                past_spec(COL_KB), cur_spec(COL_KB), past_spec(COL_VB), cur_spec(COL_VB),
                pl.BlockSpec(bias.shape, lambda i, j: (0, 0, 0)),
                pl.BlockSpec(sk.shape, lambda i, j: (0, 0, 0))]
    return pl.pallas_call(
        kern,
        out_shape=jax.ShapeDtypeStruct((b, t, Q_B), F32),
        grid=(b, t // rows),
        in_specs=in_specs,
        out_specs=pl.BlockSpec((None, rows, Q_B), lambda i, j: (i, j, 0)),
        compiler_params=_cparams(("arbitrary", "arbitrary")),
        name="swa_prompt",
    )(proj3, proj3, proj3, proj3, proj3, bias, sk)


def _swa_sample(proj3, k_past, v_past, table, sinks):
    b, t, _ = proj3.shape
    bias, sk = _swa_tables(table, sinks, t, WIN_PAST + t)
    kern = functools.partial(_swa_kernel, nq=t, nsub=1, piece_rows=(WIN_PAST, t), masked=False)
    in_specs = [pl.BlockSpec((None, t, Q_B), lambda i, j: (i, 0, COL_QB)),
                pl.BlockSpec((None, WIN_PAST, LANES), lambda i, j: (i, 0, 0)),
                pl.BlockSpec((None, t, LANES), lambda i, j: (i, 0, COL_KB)),
                pl.BlockSpec((None, WIN_PAST, LANES), lambda i, j: (i, 0, 0)),
                pl.BlockSpec((None, t, LANES), lambda i, j: (i, 0, COL_VB)),
                pl.BlockSpec(bias.shape, lambda i, j: (0, 0, 0)),
                pl.BlockSpec(sk.shape, lambda i, j: (0, 0, 0))]
    return pl.pallas_call(
        kern,
        out_shape=jax.ShapeDtypeStruct((b, t, Q_B), F32),
        grid=(b, 1),
        in_specs=in_specs,
        out_specs=pl.BlockSpec((None, t, Q_B), lambda i, j: (i, 0, 0)),
        compiler_params=_cparams(("arbitrary", "arbitrary")),
        name="swa_sample",
    )(proj3, k_past, proj3, v_past, proj3, bias, sk)


def _outproj_kernel(x_ref, oa_ref, ob_ref, wa_ref, wb_ref, g_ref, wq_ref, x1_ref, q_ref):
    x1 = (x_ref[...]
          + jnp.dot(oa_ref[...].astype(BF16), wa_ref[...], preferred_element_type=F32)
          + jnp.dot(ob_ref[...].astype(BF16), wb_ref[...], preferred_element_type=F32))
    x1_ref[...] = x1
    hc = _rms(x1, g_ref[...]).astype(BF16)
    q_ref[...] = jnp.dot(hc, wq_ref[...], preferred_element_type=F32)


def _outproj(x, o_a, o_b, w_out, norm_cross, w_mq, tm):
    m = x.shape[0]
    return pl.pallas_call(
        _outproj_kernel,
        out_shape=(jax.ShapeDtypeStruct((m, D_MODEL), F32), jax.ShapeDtypeStruct((m, MEM_W), F32)),
        grid=(m // tm,),
        in_specs=[pl.BlockSpec((tm, D_MODEL), lambda i: (i, 0)),
                  pl.BlockSpec((tm, V_A), lambda i: (i, 0)),
                  pl.BlockSpec((tm, Q_B), lambda i: (i, 0)),
                  pl.BlockSpec((V_A, D_MODEL), lambda i: (0, 0)),
                  pl.BlockSpec((Q_B, D_MODEL), lambda i: (1, 0)),
                  pl.BlockSpec((1, D_MODEL), lambda i: (0, 0)),
                  pl.BlockSpec((D_MODEL, MEM_W), lambda i: (0, 0))],
        out_specs=(pl.BlockSpec((tm, D_MODEL), lambda i: (i, 0)),
                   pl.BlockSpec((tm, MEM_W), lambda i: (i, 0))),
        compiler_params=_cparams(("arbitrary",)),
        name="outproj",
    )(x, o_a, o_b, w_out, w_out, norm_cross, w_mq)


def _cross_router_kernel(x1_ref, q_ref, mk_ref, mv_ref, wo_ref, g_ref, wr_ref, br_ref,
                         x2_ref, hp_ref, route_ref):
    nb, tm = x1_ref.shape[0], x1_ref.shape[1]
    rows = []
    for bi in range(nb):
        q = q_ref[bi]
        outs = []
        for h in range(MEM_HEADS):
            sl = slice(h * MEM_DH, (h + 1) * MEM_DH)
            mk = mk_ref[bi, pl.ds(h, MEM_LEN, stride=MEM_HEADS), :].astype(BF16)
            mv = mv_ref[bi, pl.ds(h, MEM_LEN, stride=MEM_HEADS), :].astype(BF16)
            s = lax.dot_general(q[:, sl].astype(BF16), mk, (((1,), (1,)), ((), ())),
                                preferred_element_type=F32) * MEM_DH ** -0.5
            m = jnp.max(s, axis=-1, keepdims=True)
            p = jnp.exp(s - m)
            p = (p / jnp.sum(p, axis=-1, keepdims=True)).astype(BF16)
            outs.append(jnp.dot(p, mv, preferred_element_type=F32))
        rows.append(jnp.concatenate(outs, axis=1))
    att = (rows[0] if nb == 1 else jnp.concatenate(rows, axis=0)).astype(BF16)
    x2 = x1_ref[...].reshape(nb * tm, D_MODEL) + jnp.dot(att, wo_ref[...], preferred_element_type=F32)
    x2_ref[...] = x2.reshape(nb, tm, D_MODEL)
    hf = _rms(x2, g_ref[...])
    packed = _pack_halves(hf[:, 0:HALF], hf[:, HALF:D_MODEL])
    for bi in range(nb):
        for j in range(XROW_TILES):
            hp_ref[bi, pl.ds(j, tm, stride=XROW_TILES), :] = packed[bi * tm:(bi + 1) * tm, j * LANES:(j + 1) * LANES]
    hf_hi = hf.astype(BF16)
    hf_lo = (hf - hf_hi.astype(F32)).astype(BF16)
    wr = wr_ref[...]
    r_hi = jnp.dot(hf_hi, wr, preferred_element_type=F32)
    r_lo = jnp.dot(hf_lo, wr[:, 0:LANES], preferred_element_type=F32)
    logits = r_hi[:, 0:LANES] + r_hi[:, LANES:2 * LANES] + r_lo + br_ref[...]
    lane = lax.broadcasted_iota(jnp.int32, logits.shape, 1)
    lanef = lane.astype(F32)
    l = jnp.where(lane < N_EXPERTS, logits, -jnp.inf)
    vals, idxs = [], []
    for _ in range(TOP_K):
        m = jnp.max(l, axis=-1, keepdims=True)
        idx = jnp.min(jnp.where(l == m, lanef, float(LANES)), axis=-1, keepdims=True)
        vals.append(m)
        idxs.append(idx)
        l = jnp.where(lanef == idx, -jnp.inf, l)
    es = [jnp.exp(v - vals[0]) for v in vals]
    den = es[0] + es[1] + es[2] + es[3]
    route = jnp.zeros(logits.shape, F32)
    for k in range(TOP_K):
        route = jnp.where(lane == k, idxs[k], route)
        route = jnp.where(lane == TOP_K + k, es[k] / den, route)
    route_ref[...] = route.reshape(nb, tm, LANES)


def _split_bf16(w):
    w_hi = w.astype(BF16)
    w_lo = (w - w_hi.astype(F32)).astype(BF16)
    return jnp.concatenate([w_hi, w_lo], axis=1)


def _cross_router(x1, qm, mk, mv, w_mo, norm_ffn, w_router, b_router, nb, tm):
    b, t, _ = x1.shape
    nt = t // tm
    return pl.pallas_call(
        _cross_router_kernel,
        out_shape=(jax.ShapeDtypeStruct((b, t, D_MODEL), F32),
                   jax.ShapeDtypeStruct((b, t * XROW_TILES, LANES), jnp.uint32),
                   jax.ShapeDtypeStruct((b, t, LANES), F32)),
        grid=(b // nb, nt),
        in_specs=[pl.BlockSpec((nb, tm, D_MODEL), lambda i, j: (i, j, 0)),
                  pl.BlockSpec((nb, tm, MEM_W), lambda i, j: (i, j, 0)),
                  pl.BlockSpec((nb, MEM_LEN * MEM_HEADS, MEM_DH), lambda i, j: (i, 0, 0)),
                  pl.BlockSpec((nb, MEM_LEN * MEM_HEADS, MEM_DH), lambda i, j: (i, 0, 0)),
                  pl.BlockSpec((MEM_W, D_MODEL), lambda i, j: (0, 0)),
                  pl.BlockSpec((1, D_MODEL), lambda i, j: (0, 0)),
                  pl.BlockSpec((D_MODEL, 2 * LANES), lambda i, j: (0, 0)),
                  pl.BlockSpec((1, LANES), lambda i, j: (0, 0))],
        out_specs=(pl.BlockSpec((nb, tm, D_MODEL), lambda i, j: (i, j, 0)),
                   pl.BlockSpec((nb, tm * XROW_TILES, LANES), lambda i, j: (i, j, 0)),
                   pl.BlockSpec((nb, tm, LANES), lambda i, j: (i, j, 0))),
        compiler_params=_cparams(("arbitrary", "arbitrary")),
        name="cross_router",
    )(x1, qm, mk, mv, w_mo, norm_ffn, w_router, b_router)


def _row_tile(idx, tiles):
    return pl.ds(pl.multiple_of(idx * tiles, tiles), tiles)


def _dispatch_kernel(*refs, tokens, first_group):
    if first_group:
        pos_ref, tail_ref, hp_ref, xs_hbm, zero_ref, sem = refs
    else:
        pos_ref, tail_ref, hp_ref, _, xs_hbm, zero_ref, sem = refs

    def zero_tails():
        zero_ref[...] = jnp.zeros(zero_ref.shape, zero_ref.dtype)

        def tail_copy(e):
            dst = xs_hbm.at[pl.ds(pl.multiple_of(tail_ref[e] * XROW_TILES, XROW_TILES), MOE_STAGE * XROW_TILES)]
            return pltpu.make_async_copy(zero_ref, dst, sem.at[0])

        for e in range(N_EXPERTS):
            pl.when(tail_ref[e] >= 0)(lambda e=e: tail_copy(e).start())
        for e in range(N_EXPERTS):
            pl.when(tail_ref[e] >= 0)(lambda e=e: tail_copy(e).wait())

    if first_group:
        pl.when(pl.program_id(0) == 0)(zero_tails)

    def issue(t, carry):
        src = hp_ref.at[_row_tile(t, XROW_TILES)]
        for k in range(TOP_K):
            dst = xs_hbm.at[_row_tile(pos_ref[t * TOP_K + k], XROW_TILES)]
            pltpu.make_async_copy(src, dst, sem.at[1]).start(priority=k % 2)
        return carry

    lax.fori_loop(0, tokens, issue, 0, unroll=4)

    for k in range(TOP_K):
        pltpu.make_async_copy(hp_ref, xs_hbm.at[pl.ds(0, tokens * XROW_TILES)], sem.at[1]).wait()


def _dispatch(hp, pos_flat, tail_start, rows, tokens, xs_prev=None):
    t = hp.shape[0] // XROW_TILES
    first_group = xs_prev is None
    kern = functools.partial(_dispatch_kernel, tokens=tokens, first_group=first_group)
    in_specs = [pl.BlockSpec((tokens * TOP_K,), lambda i: (i,), memory_space=pltpu.SMEM),
                pl.BlockSpec((N_EXPERTS,), lambda i: (0,), memory_space=pltpu.SMEM),
                pl.BlockSpec((tokens * XROW_TILES, LANES), lambda i: (i, 0))]
    args = [pos_flat, tail_start, hp]
    if not first_group:
        in_specs.append(pl.BlockSpec(memory_space=pl.ANY))
        args.append(xs_prev)
    return pl.pallas_call(
        kern,
        out_shape=jax.ShapeDtypeStruct((rows * XROW_TILES, LANES), jnp.uint32),
        grid=(t // tokens,),
        in_specs=in_specs,
        out_specs=pl.BlockSpec(memory_space=pl.ANY),
        scratch_shapes=[pltpu.VMEM((MOE_STAGE * XROW_TILES, LANES), jnp.uint32),
                        pltpu.SemaphoreType.DMA((2,))],
        input_output_aliases={} if first_group else {3: 0},
        compiler_params=pltpu.CompilerParams(dimension_semantics=("arbitrary",),
                                             vmem_limit_bytes=VMEM_LIMIT_BYTES, has_side_effects=True),
        name="moe_dispatch",
    )(*args)


def _moe_kernel(ie_ref, is_ref, ns_ref, xs_hbm, wg_ref, bg_ref, wu_ref, bu_ref, wd_ref, bd_ref,
                ys_hbm, xu_ref, acc_ref, ystage_ref, in_sem, out_sem):
    w = pl.program_id(0)
    f = pl.program_id(1)
    nstage = ns_ref[w]
    nsub = nstage // (MOE_SUB // MOE_STAGE)
    start = is_ref[w]

    def stage_rows(s):
        return pl.ds(pl.multiple_of(s * MOE_STAGE, MOE_STAGE), MOE_STAGE)

    def hbm_rows(first, s, tiles):
        return pl.ds(pl.multiple_of((first + s * MOE_STAGE) * tiles, MOE_STAGE * tiles), MOE_STAGE * tiles)

    def load(first, s):
        dst = xu_ref.at[pl.ds(pl.multiple_of(s * MOE_STAGE * XROW_TILES, MOE_STAGE * XROW_TILES),
                              MOE_STAGE * XROW_TILES)]
        return pltpu.make_async_copy(xs_hbm.at[hbm_rows(first, s, XROW_TILES)], dst, in_sem.at[0])

    def fetch_item(first, stages):
        def issue(s, carry):
            load(first, s).start()
            return carry

        lax.fori_loop(0, stages, issue, 0)

    pl.when(jnp.logical_and(jnp.logical_and(w == 0, f == 0), nstage > 0))(lambda: fetch_item(start, nstage))

    @pl.when(jnp.logical_and(f == 0, nstage > 0))
    def _():
        def landed(s, carry):
            load(start, s).wait()
            return carry

        lax.fori_loop(0, nstage, landed, 0)

        def init(s, carry):
            acc_ref[stage_rows(s), :] = jnp.broadcast_to(bd_ref[...], (MOE_STAGE, D_MODEL))
            return carry

        lax.fori_loop(0, nstage, init, 0)

    def x_rows(row0, nrows):
        base = pl.multiple_of(row0 * XROW_TILES, MOE_STAGE * XROW_TILES)
        lows, highs = [], []
        for j in range(XROW_TILES):
            lo, hi = _unpack_halves(xu_ref[pl.ds(base + j, nrows, stride=XROW_TILES), :])
            lows.append(lo.astype(BF16))
            highs.append(hi.astype(BF16))
        return jnp.concatenate(lows + highs, axis=1)

    @pl.when(nstage > 0)
    def _():
        bg = bg_ref[...]
        bu = bu_ref[...]

        def expert_rows(row0, nrows):
            x = x_rows(row0, nrows)
            gt = jnp.dot(x, wg_ref[...].astype(BF16), preferred_element_type=F32) + bg
            up = jnp.dot(x, wu_ref[...].astype(BF16), preferred_element_type=F32) + bu
            gt = jnp.minimum(gt, SWIGLU_LIMIT)
            up = jnp.clip(up, -SWIGLU_LIMIT, SWIGLU_LIMIT)
            a = (gt * jax.nn.sigmoid(SWIGLU_ALPHA * gt) * (up + 1.0)).astype(BF16)
            rows = pl.ds(pl.multiple_of(row0, MOE_STAGE), nrows)
            acc_ref[rows, :] += jnp.dot(a, wd_ref[...].astype(BF16), preferred_element_type=F32)

        def body(s, carry):
            expert_rows(s * MOE_SUB, MOE_SUB)
            return carry

        lax.fori_loop(0, nsub, body, 0)
        pl.when(nstage % (MOE_SUB // MOE_STAGE) == 1)(lambda: expert_rows(nsub * MOE_SUB, MOE_STAGE))

    @pl.when(jnp.logical_and(f == MOE_NF - 1, nstage > 0))
    def _():
        nxt = jnp.minimum(w + 1, pl.num_programs(0) - 1)
        nxt_stages = jnp.where(w + 1 < pl.num_programs(0), ns_ref[nxt], 0)
        fetch_item(is_ref[nxt], nxt_stages)

        def store(s):
            slot = s % MOE_OUT_SLOTS
            return pltpu.make_async_copy(ystage_ref.at[slot], ys_hbm.at[hbm_rows(start, s, YROW_TILES)],
                                         out_sem.at[slot])

        def write(s, carry):
            pl.when(s >= MOE_OUT_SLOTS)(lambda: store(s - MOE_OUT_SLOTS).wait())
            for j in range(YROW_TILES):
                lo = acc_ref[stage_rows(s), j * LANES:(j + 1) * LANES]
                hi = acc_ref[stage_rows(s), HALF + j * LANES:HALF + (j + 1) * LANES]
                ystage_ref[s % MOE_OUT_SLOTS, pl.ds(j, MOE_STAGE, stride=YROW_TILES), :] = _pack_halves(lo, hi)
            store(s).start()
            return carry

        lax.fori_loop(0, nstage, write, 0)
        for back in range(1, MOE_OUT_SLOTS + 1):
            pl.when(nstage >= back)(lambda back=back: store(nstage - back).wait())


def _moe(xs, item_expert, item_start, item_nsub, w_gate, b_gate, w_up, b_up, w_down, b_down):
    rows = xs.shape[0] // XROW_TILES
    nitems = item_expert.shape[0]

    def fcol(w, f, ie, st, ns):
        return jnp.where(ns[w] > 0, f, MOE_NF - 1)

    grid_spec = pltpu.PrefetchScalarGridSpec(
        num_scalar_prefetch=3,
        grid=(nitems, MOE_NF),
        in_specs=[pl.BlockSpec(memory_space=pl.ANY),
                  pl.BlockSpec((None, D_MODEL, MOE_TF), lambda w, f, ie, st, ns: (ie[w], 0, fcol(w, f, ie, st, ns))),
                  pl.BlockSpec((None, 1, MOE_TF), lambda w, f, ie, st, ns: (ie[w], 0, fcol(w, f, ie, st, ns))),
                  pl.BlockSpec((None, D_MODEL, MOE_TF), lambda w, f, ie, st, ns: (ie[w], 0, fcol(w, f, ie, st, ns))),
                  pl.BlockSpec((None, 1, MOE_TF), lambda w, f, ie, st, ns: (ie[w], 0, fcol(w, f, ie, st, ns))),
                  pl.BlockSpec((None, MOE_TF, D_MODEL), lambda w, f, ie, st, ns: (ie[w], fcol(w, f, ie, st, ns), 0)),
                  pl.BlockSpec((None, 1, D_MODEL), lambda w, f, ie, st, ns: (ie[w], 0, 0))],
        out_specs=pl.BlockSpec(memory_space=pl.ANY),
        scratch_shapes=[pltpu.VMEM((MOE_ROWS * XROW_TILES, LANES), jnp.uint32),
                        pltpu.VMEM((MOE_ROWS, D_MODEL), F32),
                        pltpu.VMEM((MOE_OUT_SLOTS, MOE_STAGE * YROW_TILES, LANES), jnp.uint32),
                        pltpu.SemaphoreType.DMA((1,)),
                        pltpu.SemaphoreType.DMA((MOE_OUT_SLOTS,))])
    return pl.pallas_call(
        _moe_kernel,
        out_shape=jax.ShapeDtypeStruct((rows * YROW_TILES, LANES), jnp.uint32),
        grid_spec=grid_spec,
        compiler_params=pltpu.CompilerParams(dimension_semantics=("arbitrary", "arbitrary"),
                                             vmem_limit_bytes=MOE_VMEM_LIMIT_BYTES, has_side_effects=True),
        name="moe_experts",
    )(item_expert, item_start, item_nsub, xs, w_gate, b_gate, w_up, b_up, w_down, b_down)


def _combine_kernel(pos_ref, nxt_ref, x2_ref, route_ref, g_ref, ys_hbm, o_ref, buf_ref, sem, *, tokens):
    i = pl.program_id(0)
    slot = i % 2

    def gather(p_ref, dst_slot):
        def issue(t, carry):
            for k in range(TOP_K):
                pltpu.make_async_copy(ys_hbm.at[_row_tile(p_ref[t * TOP_K + k], YROW_TILES)],
                                      buf_ref.at[dst_slot, k, _row_tile(t, YROW_TILES)],
                                      sem.at[dst_slot]).start(priority=k % 2)
            return carry

        lax.fori_loop(0, tokens, issue, 0, unroll=4)

    pl.when(i == 0)(lambda: gather(pos_ref, 0))
    pl.when(i + 1 < pl.num_programs(0))(lambda: gather(nxt_ref, 1 - slot))

    for k in range(TOP_K):
        pltpu.make_async_copy(ys_hbm.at[pl.ds(0, tokens * YROW_TILES)], buf_ref.at[slot, k], sem.at[slot]).wait()

    route = route_ref[...]
    gates = [route[:, TOP_K + k:TOP_K + k + 1] for k in range(TOP_K)]
    lows, highs = [], []
    for j in range(YROW_TILES):
        y_lo = x2_ref[:, j * LANES:(j + 1) * LANES]
        y_hi = x2_ref[:, HALF + j * LANES:HALF + (j + 1) * LANES]
        for k in range(TOP_K):
            lo, hi = _unpack_halves(buf_ref[slot, k, pl.ds(j, tokens, stride=YROW_TILES), :])
            y_lo = y_lo + gates[k] * lo
            y_hi = y_hi + gates[k] * hi
        lows.append(y_lo)
        highs.append(y_hi)
    o_ref[...] = _rms(jnp.concatenate(lows + highs, axis=1), g_ref[...])


def _combine(x2, route, final_norm, ys, pos_flat, tokens):
    t = x2.shape[0]
    nsteps = t // tokens
    kern = functools.partial(_combine_kernel, tokens=tokens)
    return pl.pallas_call(
        kern,
        out_shape=jax.ShapeDtypeStruct((t, D_MODEL), F32),
        grid=(nsteps,),
        in_specs=[pl.BlockSpec((tokens * TOP_K,), lambda i: (i,), memory_space=pltpu.SMEM),
                  pl.BlockSpec((tokens * TOP_K,), lambda i: (jnp.minimum(i + 1, nsteps - 1),),
                               memory_space=pltpu.SMEM),
                  pl.BlockSpec((tokens, D_MODEL), lambda i: (i, 0)),
                  pl.BlockSpec((tokens, LANES), lambda i: (i, 0)),
                  pl.BlockSpec((1, D_MODEL), lambda i: (0, 0)),
                  pl.BlockSpec(memory_space=pl.ANY)],
        out_specs=pl.BlockSpec((tokens, D_MODEL), lambda i: (i, 0)),
        scratch_shapes=[pltpu.VMEM((2, TOP_K, tokens * YROW_TILES, LANES), jnp.uint32),
                        pltpu.SemaphoreType.DMA((2,))],
        compiler_params=_cparams(("arbitrary",)),
        name="moe_combine",
    )(pos_flat, pos_flat, x2, route, final_norm, ys)


def _routing_tables(top_i, nitems):
    t = top_i.shape[0]
    sel = jnp.sum((top_i[:, :, None] == jnp.arange(N_EXPERTS, dtype=jnp.int32)[None, None, :]).astype(jnp.int32),
                  axis=1)
    cnt = jnp.sum(sel, axis=0)
    rank = jnp.cumsum(sel, axis=0) - sel
    cnt_pad = ((cnt + MOE_STAGE - 1) // MOE_STAGE) * MOE_STAGE
    off = jnp.cumsum(cnt_pad) - cnt_pad
    pos = jnp.take_along_axis(off[None, :] + rank, top_i, axis=1)
    tail_start = jnp.where(cnt > 0, off + cnt_pad - MOE_STAGE, -1)
    items_per = (cnt + MOE_ROWS - 1) // MOE_ROWS
    item_end = jnp.cumsum(items_per)
    total = item_end[-1]
    widx = jnp.arange(nitems, dtype=jnp.int32)
    e_of = jnp.minimum(jnp.searchsorted(item_end, widx, side="right"), N_EXPERTS - 1).astype(jnp.int32)
    j_of = widx - (item_end - items_per)[e_of]
    valid = widx < total
    e_last = e_of[jnp.maximum(total - 1, 0)]
    item_expert = jnp.where(valid, e_of, e_last).astype(jnp.int32)
    item_start = jnp.where(valid, off[e_of] + j_of * MOE_ROWS, 0).astype(jnp.int32)
    rows_left = cnt_pad[e_of] - j_of * MOE_ROWS
    item_nsub = jnp.where(valid, jnp.minimum(rows_left, MOE_ROWS) // MOE_STAGE, 0).astype(jnp.int32)
    return pos.reshape(t * TOP_K).astype(jnp.int32), tail_start.astype(jnp.int32), item_expert, item_start, item_nsub


def _trunk(x, conv_hist, s0, swa_fn, mk, mv, gdn_chunk, p):
    b, t, _ = x.shape
    m = b * t
    xf = x.reshape(m, D_MODEL)
    proj = _norm_matmul(xf, p["norm_mix"], p["w_in"], min(m, 1024), PROJ_TN)
    proj3 = proj.reshape(b, t, PROJ_COLS)
    hist8 = jnp.pad(conv_hist, ((0, 0), (SUBLANES - (CONV_W - 1), 0), (0, 0)))
    o_a, conv8, s_new = _gdn(proj3, hist8, s0, p["conv_w"], p["gate_par"], p["gdn_norm"], gdn_chunk,
                             GDN_SEQS_PER_STEP)
    o_b = swa_fn(proj3)
    x1, qm = _outproj(xf, o_a.reshape(m, V_A), o_b.reshape(m, Q_B), p["w_out"], p["norm_cross"], p["w_mq"], 256)
    tm = min(t, CROSS_ROWS)
    x2, hp, route = _cross_router(x1.reshape(b, t, D_MODEL), qm.reshape(b, t, MEM_W), mk, mv,
                                  p["w_mo"], p["norm_ffn"], p["w_router"], p["b_router"], CROSS_ROWS // tm, tm)
    return proj3, conv8[:, SUBLANES - (CONV_W - 1):], s_new, x2.reshape(m, D_MODEL), \
        hp.reshape(m * XROW_TILES, LANES), route.reshape(m, LANES)


def kernel(x_prompt, x_sample, cache_conv, state_gdn, cache_swa_k, cache_swa_v, cache_mem_k, cache_mem_v, mem_prompt, norm_mix, w_in, conv_w, gdn_a_log, gdn_dt_bias, gdn_norm, swa_sinks, rel_bias_table, w_out, norm_cross, norm_mem, w_mq, w_mk, w_mv, w_mo, norm_ffn, w_router, b_router, w_gate, b_gate, w_up, b_up, w_down, b_down, final_norm):
    depth = norm_mix.shape[0]
    assert depth == 1, "kernel is written for the single-layer trunk"
    bp, sp, _ = x_prompt.shape
    bs, ss, _ = x_sample.shape
    l = 0
    w = w_in[l]
    n_ab = 2 * GDN_HEADS
    w_perm = jnp.concatenate(
        [w[:, :CONV_CH + V_A], w[:, CONV_CH + V_A + n_ab:], w[:, CONV_CH + V_A:CONV_CH + V_A + n_ab],
         jnp.zeros((D_MODEL, PROJ_COLS - w.shape[1]), w.dtype)], axis=1).astype(BF16)
    gate_par = jnp.zeros((SUBLANES, LANES), F32)
    gate_par = gate_par.at[0, :GDN_HEADS].set(gdn_a_log[l]).at[1, :GDN_HEADS].set(gdn_dt_bias[l])
    p = dict(
        norm_mix=norm_mix[l].reshape(1, D_MODEL), w_in=w_perm, conv_w=conv_w[l], gate_par=gate_par,
        gdn_norm=gdn_norm[l].reshape(1, GDN_D), w_out=w_out[l].astype(BF16),
        norm_cross=norm_cross[l].reshape(1, D_MODEL), w_mq=w_mq[l].astype(BF16), w_mo=w_mo[l].astype(BF16),
        norm_ffn=norm_ffn[l].reshape(1, D_MODEL),
        w_router=_split_bf16(jnp.pad(w_router[l], ((0, 0), (0, LANES - N_EXPERTS)))),
        b_router=jnp.pad(b_router[l], (0, LANES - N_EXPERTS)).reshape(1, LANES))

    w_mkv = jnp.concatenate([w_mk[l], w_mv[l]], axis=1).astype(BF16)
    mkv = _norm_matmul(mem_prompt.reshape(bp * MEM_LEN, D_MODEL), norm_mem[l].reshape(1, D_MODEL), w_mkv,
                       min(bp * MEM_LEN, 1024), MEMKV_TN)
    mk_p = mkv[:, :MEM_W].reshape(bp, MEM_LEN * MEM_HEADS, MEM_DH)
    mv_p = mkv[:, MEM_W:].reshape(bp, MEM_LEN * MEM_HEADS, MEM_DH)

    zero_hist = jnp.zeros((bp, CONV_W - 1, CONV_CH), F32)
    zero_state = jnp.zeros((bp, GDN_HEADS, GDN_D, GDN_D), F32)
    swa_p = functools.partial(_swa_prompt, table=rel_bias_table, sinks=swa_sinks[l])
    proj_p, conv_p, st_p, x2_p, hp_p, route_p = _trunk(x_prompt, zero_hist, zero_state, swa_p, mk_p, mv_p, CHUNK, p)

    k_past = cache_swa_k[l].reshape(bs, WIN_PAST, KV_B)
    v_past = cache_swa_v[l].reshape(bs, WIN_PAST, KV_B)
    swa_s = functools.partial(_swa_sample, k_past=k_past, v_past=v_past, table=rel_bias_table, sinks=swa_sinks[l])
    proj_s, conv_s, st_s, x2_s, hp_s, route_s = _trunk(
        x_sample, cache_conv[l], state_gdn[l], swa_s,
        cache_mem_k[l].reshape(bs, MEM_LEN * MEM_HEADS, MEM_DH),
        cache_mem_v[l].reshape(bs, MEM_LEN * MEM_HEADS, MEM_DH), ss, p)

    mp, ms = bp * sp, bs * ss
    ntok = mp + ms
    top_i = jnp.concatenate([route_p[:, :TOP_K], route_s[:, :TOP_K]], axis=0).astype(jnp.int32)
    rows = ntok * TOP_K + N_EXPERTS * MOE_STAGE
    nitems = (ntok * TOP_K) // MOE_ROWS + N_EXPERTS
    pos_flat, tail_start, item_expert, item_start, item_nsub = _routing_tables(top_i, nitems)
    pos_p, pos_s = pos_flat[:mp * TOP_K], pos_flat[mp * TOP_K:]
    xs = _dispatch(hp_p, pos_p, tail_start, rows, 512)
    xs = _dispatch(hp_s, pos_s, tail_start, rows, 512, xs_prev=xs)
    ys = _moe(xs, item_expert, item_start, item_nsub,
              w_gate[l], b_gate[l].reshape(N_EXPERTS, 1, D_FF), w_up[l], b_up[l].reshape(N_EXPERTS, 1, D_FF),
              w_down[l], b_down[l].reshape(N_EXPERTS, 1, D_MODEL))
    fnorm = final_norm.reshape(1, D_MODEL)
    y_p = _combine(x2_p, route_p, fnorm, ys, pos_p, 256).reshape(bp, sp, D_MODEL)
    y_s = _combine(x2_s, route_s, fnorm, ys, pos_s, 256).reshape(bs, ss, D_MODEL)

    def kv_window(proj3, col, past=None):
        new = proj3[:, :, col * LANES:(col + 1) * LANES]
        full = new if past is None else jnp.concatenate([past, new], axis=1)
        win = full[:, -WIN_PAST:]
        return win.reshape(win.shape[0], WIN_PAST, SWA_KV_HEADS, SWA_DH)[None]

    return (y_p, y_s,
            conv_p[None], st_p[None], kv_window(proj_p, COL_KB), kv_window(proj_p, COL_VB),
            mk_p.reshape(bp, MEM_LEN, MEM_HEADS, MEM_DH)[None], mv_p.reshape(bp, MEM_LEN, MEM_HEADS, MEM_DH)[None],
            conv_s[None], st_s[None], kv_window(proj_s, COL_KB, k_past), kv_window(proj_s, COL_VB, v_past))
```

```python
import functools
import math

import numpy as np
import jax
import jax.numpy as jnp
from jax import lax
from jax.experimental import pallas as pl
from jax.experimental.pallas import tpu as pltpu

F32 = jnp.float32
BF16 = jnp.bfloat16
HIGHEST = lax.Precision.HIGHEST

D_MODEL = 2048
CHUNK = 64
GDN_HEADS = 8
GDN_D = 128
CONV_W = 4
SWA_HEADS = 16
SWA_KV_HEADS = 2
SWA_GROUP = SWA_HEADS // SWA_KV_HEADS
SWA_DH = 64
WIN_PAST = 128
REL_BUCKETS = 32
REL_MAX_DIST = 128
MEM_LEN = 256
MEM_HEADS = 4
MEM_DH = 128
N_EXPERTS = 32
TOP_K = 4
D_FF = D_MODEL
SWIGLU_LIMIT = 7.0
SWIGLU_ALPHA = 1.702
EPS = 1e-6

QK_A = GDN_HEADS * GDN_D
V_A = GDN_HEADS * GDN_D
CONV_CH = 2 * QK_A + V_A
Q_B = SWA_HEADS * SWA_DH
KV_B = SWA_KV_HEADS * SWA_DH
MEM_W = MEM_HEADS * MEM_DH

LANES = 128
SUBLANES = 8
VMEM_LIMIT_BYTES = 58 * 1024 * 1024
MOE_VMEM_LIMIT_BYTES = 60 * 1024 * 1024

PROJ_COLS = 5632
PROJ_TN = 1408
MEMKV_TN = 512
COL_Z = CONV_CH // V_A
COL_QB = (CONV_CH + V_A) // Q_B
COL_KB = (CONV_CH + V_A + Q_B) // LANES
COL_VB = COL_KB + 1
COL_AB = COL_KB + 2

MOE_SUB = 512
MOE_ROWS = 1536
MOE_STAGE = 128
MOE_OUT_SLOTS = 4
MOE_TF = 512
MOE_NF = D_FF // MOE_TF
HALF = D_MODEL // 2
NORM_ROWS = 256
SWA_CHUNKS_PER_STEP = 4
CROSS_ROWS = 256
GDN_SEQS_PER_STEP = 4
XROW_TILES = HALF // LANES
YROW_TILES = HALF // LANES


def _cparams(sem):
    return pltpu.CompilerParams(dimension_semantics=sem, vmem_limit_bytes=VMEM_LIMIT_BYTES)


def _rms(x, gain):
    return x * lax.rsqrt(jnp.mean(x * x, axis=-1, keepdims=True) + EPS) * gain


def _pack_halves(lo, hi):
    lo_bits = pltpu.bitcast(lo.astype(BF16).astype(F32), jnp.uint32)
    hi_bits = pltpu.bitcast(hi.astype(BF16).astype(F32), jnp.uint32)
    return (lo_bits >> 16) | (hi_bits & jnp.uint32(0xFFFF0000))


def _unpack_halves(u):
    return pltpu.bitcast(u << 16, F32), pltpu.bitcast(u & jnp.uint32(0xFFFF0000), F32)


def _norm_matmul_kernel(x_ref, g_ref, w_ref, o_ref, h_ref):
    @pl.when(pl.program_id(1) == 0)
    def _():
        def body(r, carry):
            rows = pl.ds(pl.multiple_of(r * NORM_ROWS, NORM_ROWS), NORM_ROWS)
            h_ref[rows, :] = _rms(x_ref[rows, :], g_ref[...]).astype(BF16)
            return carry

        lax.fori_loop(0, x_ref.shape[0] // NORM_ROWS, body, 0)

    o_ref[...] = jnp.dot(h_ref[...], w_ref[...], preferred_element_type=F32)


def _norm_matmul(x, gain, w, tm, tn):
    m, k = x.shape
    n = w.shape[1]
    return pl.pallas_call(
        _norm_matmul_kernel,
        out_shape=jax.ShapeDtypeStruct((m, n), F32),
        grid=(m // tm, n // tn),
        in_specs=[pl.BlockSpec((tm, k), lambda i, j: (i, 0)),
                  pl.BlockSpec((1, k), lambda i, j: (0, 0)),
                  pl.BlockSpec((k, tn), lambda i, j: (0, j))],
        out_specs=pl.BlockSpec((tm, tn), lambda i, j: (i, j)),
        scratch_shapes=[pltpu.VMEM((tm, k), BF16)],
        compiler_params=_cparams(("arbitrary", "arbitrary")),
        name="norm_matmul",
    )(x, gain, w)


def _tri_inverse(lows, c):
    row = lax.broadcasted_iota(jnp.int32, (c, c), 0)
    col = lax.broadcasted_iota(jnp.int32, (c, c), 1)
    eye = jnp.where(row == col, 1.0, 0.0).astype(F32)
    ps = [eye - low for low in lows]
    ms = list(lows)
    span = 1
    while 2 * span < c:
        mbs = [m.astype(BF16) for m in ms]
        ms = [jnp.dot(mb, mb, preferred_element_type=F32) for mb in mbs]
        ps = [p + jnp.dot(p.astype(BF16), m.astype(BF16), preferred_element_type=F32) for p, m in zip(ps, ms)]
        span *= 2
    return ps


def _gdn_kernel(qkv_ref, z_ref, ab_ref, hist_ref, s0_ref, convw_ref, gp_ref, norm_ref,
                o_ref, convnew_ref, s_ref, ext_ref, *, c):
    step = pl.program_id(1)
    nb = qkv_ref.shape[0]

    @pl.when(step == 0)
    def _():
        ext_ref[:, 0:SUBLANES, :] = hist_ref[...]
        s_ref[...] = s0_ref[...]

    row = lax.broadcasted_iota(jnp.int32, (c, c), 0)
    col = lax.broadcasted_iota(jnp.int32, (c, c), 1)
    incl = row >= col
    strict = row > col
    tril = jnp.where(incl, 1.0, 0.0).astype(F32)
    gp = gp_ref[...]
    scale = GDN_D ** -0.5
    heads = [(bi, h) for bi in range(nb) for h in range(GDN_HEADS)]
    s_old = [s_ref[bi, h] for bi, h in heads]

    acts, gc_alls, gc_ts, beta_alls = [], [], [], []
    for bi in range(nb):
        ext_ref[bi, SUBLANES:SUBLANES + c, :] = qkv_ref[bi]
        base = SUBLANES - (CONV_W - 1)
        conv = ext_ref[bi, base:base + c, :] * convw_ref[0:1, :]
        for j in range(1, CONV_W):
            conv = conv + ext_ref[bi, base + j:base + j + c, :] * convw_ref[j:j + 1, :]
        tail = ext_ref[bi, c:c + SUBLANES, :]
        convnew_ref[bi] = tail
        ext_ref[bi, 0:SUBLANES, :] = tail
        acts.append(conv * jax.nn.sigmoid(conv))
        ab = ab_ref[bi]
        g = -jnp.exp(gp[0:1, :]) * jax.nn.softplus(ab + gp[1:2, :])
        beta_alls.append(jax.nn.sigmoid(ab))
        gc_all = jnp.dot(tril, g, precision=HIGHEST, preferred_element_type=F32)
        gc_pad = jnp.concatenate([gc_all, jnp.zeros((LANES - c, LANES), F32)], axis=0) if c < LANES else gc_all
        gc_alls.append(gc_all)
        gc_ts.append(gc_pad.T)

    q, k, v = [], [], []
    for bi, h in heads:
        lo = h * GDN_D
        qh = acts[bi][:, lo:lo + GDN_D]
        kh = acts[bi][:, QK_A + lo:QK_A + lo + GDN_D]
        q.append(qh * lax.rsqrt(jnp.sum(qh * qh, axis=-1, keepdims=True) + EPS) * scale)
        k.append(kh * lax.rsqrt(jnp.sum(kh * kh, axis=-1, keepdims=True) + EPS))
        v.append(acts[bi][:, 2 * QK_A + lo:2 * QK_A + lo + GDN_D])
    beta = [beta_alls[bi][:, GDN_HEADS + h:GDN_HEADS + h + 1] for bi, h in heads]
    gc = [gc_alls[bi][:, h:h + 1] for bi, h in heads]
    gc_last = [gc_alls[bi][c - 1:c, h:h + 1] for bi, h in heads]
    heads = range(len(heads))
    decay = [jnp.exp(jnp.where(incl, gc[h] - gc_ts[h // GDN_HEADS][h % GDN_HEADS:h % GDN_HEADS + 1, 0:c],
                               -jnp.inf)) for h in heads]
    egc = [jnp.exp(gc[h]) for h in heads]
    kb = [k[h] * beta[h] for h in heads]
    qk = [lax.dot_general(jnp.concatenate([q[h], kb[h]], axis=0).astype(BF16), k[h].astype(BF16),
                          (((1,), (1,)), ((), ())), preferred_element_type=F32) for h in heads]
    a_qk = [jnp.where(incl, qk[h][0:c] * decay[h], 0.0) for h in heads]
    tinv = _tri_inverse([jnp.where(strict, qk[h][c:2 * c] * decay[h], 0.0) for h in heads], c)
    uw = [jnp.dot(tinv[h].astype(BF16),
                  jnp.concatenate([v[h] * beta[h], kb[h] * egc[h]], axis=1).astype(BF16),
                  preferred_element_type=F32) for h in heads]
    ws_qs = [jnp.dot(jnp.concatenate([uw[h][:, GDN_D:2 * GDN_D], q[h] * egc[h]], axis=0).astype(BF16),
                     s_old[h].astype(BF16), preferred_element_type=F32) for h in heads]
    v_new = [(uw[h][:, 0:GDN_D] - ws_qs[h][0:c]).astype(BF16) for h in heads]
    o = [ws_qs[h][c:2 * c] + jnp.dot(a_qk[h].astype(BF16), v_new[h], preferred_element_type=F32)
         for h in heads]
    s_new = [s_old[h] * jnp.exp(gc_last[h]) + lax.dot_general(
        (k[h] * jnp.exp(gc_last[h] - gc[h])).astype(BF16), v_new[h], (((0,), (0,)), ((), ())),
        preferred_element_type=F32) for h in heads]
    for bi in range(nb):
        z_all = z_ref[bi]
        gate = z_all * jax.nn.sigmoid(z_all)
        o_ref[bi] = jnp.concatenate([_rms(o[bi * GDN_HEADS + h], norm_ref[...]) for h in range(GDN_HEADS)],
                                    axis=1) * gate
    for h in heads:
        s_ref[h // GDN_HEADS, h % GDN_HEADS] = s_new[h]


def _gdn(proj3, hist8, s0, conv_w, gate_par, gdn_norm, c, nb):
    b, t, _ = proj3.shape
    nchunks = t // c
    kern = functools.partial(_gdn_kernel, c=c)
    return pl.pallas_call(
        kern,
        out_shape=(jax.ShapeDtypeStruct((b, t, V_A), F32),
                   jax.ShapeDtypeStruct((b, SUBLANES, CONV_CH), F32),
                   jax.ShapeDtypeStruct((b, GDN_HEADS, GDN_D, GDN_D), F32)),
        grid=(b // nb, nchunks),
        in_specs=[pl.BlockSpec((nb, c, CONV_CH), lambda i, j: (i, j, 0)),
                  pl.BlockSpec((nb, c, V_A), lambda i, j: (i, j, COL_Z)),
                  pl.BlockSpec((nb, c, LANES), lambda i, j: (i, j, COL_AB)),
                  pl.BlockSpec((nb, SUBLANES, CONV_CH), lambda i, j: (i, 0, 0)),
                  pl.BlockSpec((nb, GDN_HEADS, GDN_D, GDN_D), lambda i, j: (i, 0, 0, 0)),
                  pl.BlockSpec((CONV_W, CONV_CH), lambda i, j: (0, 0)),
                  pl.BlockSpec((SUBLANES, LANES), lambda i, j: (0, 0)),
                  pl.BlockSpec((1, GDN_D), lambda i, j: (0, 0))],
        out_specs=(pl.BlockSpec((nb, c, V_A), lambda i, j: (i, j, 0)),
                   pl.BlockSpec((nb, SUBLANES, CONV_CH), lambda i, j: (i, 0, 0)),
                   pl.BlockSpec((nb, GDN_HEADS, GDN_D, GDN_D), lambda i, j: (i, 0, 0, 0))),
        scratch_shapes=[pltpu.VMEM((nb, SUBLANES + c, CONV_CH), F32)],
        compiler_params=_cparams(("arbitrary", "arbitrary")),
        name="gdn_mixer",
    )(proj3, proj3, proj3, hist8, s0, conv_w, gate_par, gdn_norm)


def _swa_kernel(*refs, nq, nsub, piece_rows, masked):
    npieces = len(piece_rows)
    q_ref = refs[0]
    k_refs = refs[1:1 + npieces]
    v_refs = refs[1 + npieces:1 + 2 * npieces]
    bias_ref, sink_ref, o_ref = refs[1 + 2 * npieces:]
    nk = WIN_PAST + nq
    kcat = jnp.concatenate([r[...] for r in k_refs], axis=0).astype(BF16)
    vcat = jnp.concatenate([r[...] for r in v_refs], axis=0).astype(BF16)
    q = q_ref[...]
    kidx = lax.broadcasted_iota(jnp.int32, (SWA_GROUP * nq, nk), 1)
    blocks = [(i, kv) for i in range(nsub) for kv in range(SWA_KV_HEADS)]
    scores = []
    for i, kv in blocks:
        qs = jnp.concatenate(
            [q[i * nq:(i + 1) * nq, (kv * SWA_GROUP + g) * SWA_DH:(kv * SWA_GROUP + g + 1) * SWA_DH]
             for g in range(SWA_GROUP)], axis=0).astype(BF16)
        kh = kcat[i * nq:i * nq + nk, kv * SWA_DH:(kv + 1) * SWA_DH]
        s = lax.dot_general(qs, kh, (((1,), (1,)), ((), ())), preferred_element_type=F32)
        s = s * SWA_DH ** -0.5 + bias_ref[kv]
        if masked:
            first = (pl.program_id(1) * nsub + i) * nq - WIN_PAST
            s = jnp.where(kidx + first >= 0, s, -jnp.inf)
        scores.append(s)
    sinks = [sink_ref[kv] for _, kv in blocks]
    maxes = [jnp.maximum(jnp.max(s, axis=-1, keepdims=True), sk) for s, sk in zip(scores, sinks)]
    probs = [jnp.exp(s - m) for s, m in zip(scores, maxes)]
    dens = [jnp.sum(p, axis=-1, keepdims=True) + jnp.exp(sk - m) for p, sk, m in zip(probs, sinks, maxes)]
    outs = [jnp.dot((p / den).astype(BF16), vcat[i * nq:i * nq + nk, kv * SWA_DH:(kv + 1) * SWA_DH],
                    preferred_element_type=F32) for (i, kv), p, den in zip(blocks, probs, dens)]
    out_rows = []
    for i in range(nsub):
        heads = []
        for kv in range(SWA_KV_HEADS):
            o = outs[i * SWA_KV_HEADS + kv]
            heads += [o[g * nq:(g + 1) * nq, :] for g in range(SWA_GROUP)]
        out_rows.append(jnp.concatenate(heads, axis=1))
    o_ref[...] = out_rows[0] if nsub == 1 else jnp.concatenate(out_rows, axis=0)


def _rel_bucket(nq, nk):
    rel = jnp.arange(nk)[None, :] - WIN_PAST - jnp.arange(nq)[:, None]
    nb = REL_BUCKETS // 2
    max_exact = nb // 2
    n = jnp.abs(rel)
    large = max_exact + (jnp.log(jnp.maximum(n, 1).astype(F32) / max_exact)
                         / math.log(REL_MAX_DIST / max_exact) * (nb - max_exact)).astype(jnp.int32)
    large = jnp.minimum(large, nb - 1)
    return jnp.where(rel > 0, nb, 0) + jnp.where(n < max_exact, n, large)


def _swa_tables(table, sinks, nq, nk):
    onehot = (_rel_bucket(nq, nk)[:, :, None] == jnp.arange(REL_BUCKETS)[None, None, :]).astype(F32)
    bias = jnp.einsum("qkb,bh->qkh", onehot, table.astype(F32), precision=HIGHEST)
    bias = jnp.transpose(bias, (2, 0, 1)).astype(F32)
    bias = bias.reshape(SWA_KV_HEADS, SWA_GROUP * nq, nk)
    sk = jnp.broadcast_to(sinks.astype(F32).reshape(SWA_KV_HEADS, SWA_GROUP, 1, 1),
                          (SWA_KV_HEADS, SWA_GROUP, nq, 1)).reshape(SWA_KV_HEADS, SWA_GROUP * nq, 1)
    return bias, sk


def _swa_prompt(proj3, table, sinks):
    b, t, _ = proj3.shape
    nq = CHUNK
    nsub = SWA_CHUNKS_PER_STEP
    rows = nsub * nq
    assert rows % WIN_PAST == 0 and t % rows == 0
    past_per_step = rows // WIN_PAST
    bias, sk = _swa_tables(table, sinks, nq, WIN_PAST + nq)
    kern = functools.partial(_swa_kernel, nq=nq, nsub=nsub, piece_rows=(WIN_PAST, rows), masked=True)

    def past_spec(colblk):
        return pl.BlockSpec((None, WIN_PAST, LANES),
                            lambda i, j: (i, jnp.maximum(j * past_per_step - 1, 0), colblk))

    def cur_spec(colblk):
        return pl.BlockSpec((None, rows, LANES), lambda i, j: (i, j, colblk))

    in_specs = [pl.BlockSpec((None, rows, Q_B), lambda i, j: (i, j, COL_QB)),
                past_spec(COL_KB), cur_spec(COL_KB), past_spec(COL_VB), cur_spec(COL_VB),
                pl.BlockSpec(bias.shape, lambda i, j: (0, 0, 0)),
                pl.BlockSpec(sk.shape, lambda i, j: (0, 0, 0))]
    return pl.pallas_call(
        kern,
        out_shape=jax.ShapeDtypeStruct((b, t, Q_B), F32),
        grid=(b, t // rows),
        in_specs=in_specs,
        out_specs=pl.BlockSpec((None, rows, Q_B), lambda i, j: (i, j, 0)),
        compiler_params=_cparams(("arbitrary", "arbitrary")),
        name="swa_prompt",
    )(proj3, proj3, proj3, proj3, proj3, bias, sk)


def _swa_sample(proj3, k_past, v_past, table, sinks):
    b, t, _ = proj3.shape
    bias, sk = _swa_tables(table, sinks, t, WIN_PAST + t)
    kern = functools.partial(_swa_kernel, nq=t, nsub=1, piece_rows=(WIN_PAST, t), masked=False)
    in_specs = [pl.BlockSpec((None, t, Q_B), lambda i, j: (i, 0, COL_QB)),
                pl.BlockSpec((None, WIN_PAST, LANES), lambda i, j: (i, 0, 0)),
                pl.BlockSpec((None, t, LANES), lambda i, j: (i, 0, COL_KB)),
                pl.BlockSpec((None, WIN_PAST, LANES), lambda i, j: (i, 0, 0)),
                pl.BlockSpec((None, t, LANES), lambda i, j: (i, 0, COL_VB)),
                pl.BlockSpec(bias.shape, lambda i, j: (0, 0, 0)),
                pl.BlockSpec(sk.shape, lambda i, j: (0, 0, 0))]
    return pl.pallas_call(
        kern,
        out_shape=jax.ShapeDtypeStruct((b, t, Q_B), F32),
        grid=(b, 1),
        in_specs=in_specs,
        out_specs=pl.BlockSpec((None, t, Q_B), lambda i, j: (i, 0, 0)),
        compiler_params=_cparams(("arbitrary", "arbitrary")),
        name="swa_sample",
    )(proj3, k_past, proj3, v_past, proj3, bias, sk)


def _outproj_kernel(x_ref, oa_ref, ob_ref, wa_ref, wb_ref, g_ref, wq_ref, x1_ref, q_ref):
    x1 = (x_ref[...]
          + jnp.dot(oa_ref[...].astype(BF16), wa_ref[...], preferred_element_type=F32)
          + jnp.dot(ob_ref[...].astype(BF16), wb_ref[...], preferred_element_type=F32))
    x1_ref[...] = x1
    hc = _rms(x1, g_ref[...]).astype(BF16)
    q_ref[...] = jnp.dot(hc, wq_ref[...], preferred_element_type=F32)


def _outproj(x, o_a, o_b, w_out, norm_cross, w_mq, tm):
    m = x.shape[0]
    return pl.pallas_call(
        _outproj_kernel,
        out_shape=(jax.ShapeDtypeStruct((m, D_MODEL), F32), jax.ShapeDtypeStruct((m, MEM_W), F32)),
        grid=(m // tm,),
        in_specs=[pl.BlockSpec((tm, D_MODEL), lambda i: (i, 0)),
                  pl.BlockSpec((tm, V_A), lambda i: (i, 0)),
                  pl.BlockSpec((tm, Q_B), lambda i: (i, 0)),
                  pl.BlockSpec((V_A, D_MODEL), lambda i: (0, 0)),
                  pl.BlockSpec((Q_B, D_MODEL), lambda i: (1, 0)),
                  pl.BlockSpec((1, D_MODEL), lambda i: (0, 0)),
                  pl.BlockSpec((D_MODEL, MEM_W), lambda i: (0, 0))],
        out_specs=(pl.BlockSpec((tm, D_MODEL), lambda i: (i, 0)),
                   pl.BlockSpec((tm, MEM_W), lambda i: (i, 0))),
        compiler_params=_cparams(("arbitrary",)),
        name="outproj",
    )(x, o_a, o_b, w_out, w_out, norm_cross, w_mq)


def _cross_router_kernel(x1_ref, q_ref, mk_ref, mv_ref, wo_ref, g_ref, wr_ref, br_ref,
                         x2_ref, hp_ref, route_ref):
    nb, tm = x1_ref.shape[0], x1_ref.shape[1]
    rows = []
    for bi in range(nb):
        q = q_ref[bi]
        outs = []
        for h in range(MEM_HEADS):
            sl = slice(h * MEM_DH, (h + 1) * MEM_DH)
            mk = mk_ref[bi, pl.ds(h, MEM_LEN, stride=MEM_HEADS), :].astype(BF16)
            mv = mv_ref[bi, pl.ds(h, MEM_LEN, stride=MEM_HEADS), :].astype(BF16)
            s = lax.dot_general(q[:, sl].astype(BF16), mk, (((1,), (1,)), ((), ())),
                                preferred_element_type=F32) * MEM_DH ** -0.5
            m = jnp.max(s, axis=-1, keepdims=True)
            p = jnp.exp(s - m)
            p = (p / jnp.sum(p, axis=-1, keepdims=True)).astype(BF16)
            outs.append(jnp.dot(p, mv, preferred_element_type=F32))
        rows.append(jnp.concatenate(outs, axis=1))
    att = (rows[0] if nb == 1 else jnp.concatenate(rows, axis=0)).astype(BF16)
    x2 = x1_ref[...].reshape(nb * tm, D_MODEL) + jnp.dot(att, wo_ref[...], preferred_element_type=F32)
    x2_ref[...] = x2.reshape(nb, tm, D_MODEL)
    hf = _rms(x2, g_ref[...])
    packed = _pack_halves(hf[:, 0:HALF], hf[:, HALF:D_MODEL])
    for bi in range(nb):
        for j in range(XROW_TILES):
            hp_ref[bi, pl.ds(j, tm, stride=XROW_TILES), :] = packed[bi * tm:(bi + 1) * tm, j * LANES:(j + 1) * LANES]
    hf_hi = hf.astype(BF16)
    hf_lo = (hf - hf_hi.astype(F32)).astype(BF16)
    wr = wr_ref[...]
    r_hi = jnp.dot(hf_hi, wr, preferred_element_type=F32)
    r_lo = jnp.dot(hf_lo, wr[:, 0:LANES], preferred_element_type=F32)
    logits = r_hi[:, 0:LANES] + r_hi[:, LANES:2 * LANES] + r_lo + br_ref[...]
    lane = lax.broadcasted_iota(jnp.int32, logits.shape, 1)
    lanef = lane.astype(F32)
    l = jnp.where(lane < N_EXPERTS, logits, -jnp.inf)
    vals, idxs = [], []
    for _ in range(TOP_K):
        m = jnp.max(l, axis=-1, keepdims=True)
        idx = jnp.min(jnp.where(l == m, lanef, float(LANES)), axis=-1, keepdims=True)
        vals.append(m)
        idxs.append(idx)
        l = jnp.where(lanef == idx, -jnp.inf, l)
    es = [jnp.exp(v - vals[0]) for v in vals]
    den = es[0] + es[1] + es[2] + es[3]
    route = jnp.zeros(logits.shape, F32)
    for k in range(TOP_K):
        route = jnp.where(lane == k, idxs[k], route)
        route = jnp.where(lane == TOP_K + k, es[k] / den, route)
    route_ref[...] = route.reshape(nb, tm, LANES)


def _split_bf16(w):
    w_hi = w.astype(BF16)
    w_lo = (w - w_hi.astype(F32)).astype(BF16)
    return jnp.concatenate([w_hi, w_lo], axis=1)


def _cross_router(x1, qm, mk, mv, w_mo, norm_ffn, w_router, b_router, nb, tm):
    b, t, _ = x1.shape
    nt = t // tm
    return pl.pallas_call(
        _cross_router_kernel,
        out_shape=(jax.ShapeDtypeStruct((b, t, D_MODEL), F32),
                   jax.ShapeDtypeStruct((b, t * XROW_TILES, LANES), jnp.uint32),
                   jax.ShapeDtypeStruct((b, t, LANES), F32)),
        grid=(b // nb, nt),
        in_specs=[pl.BlockSpec((nb, tm, D_MODEL), lambda i, j: (i, j, 0)),
                  pl.BlockSpec((nb, tm, MEM_W), lambda i, j: (i, j, 0)),
                  pl.BlockSpec((nb, MEM_LEN * MEM_HEADS, MEM_DH), lambda i, j: (i, 0, 0)),
                  pl.BlockSpec((nb, MEM_LEN * MEM_HEADS, MEM_DH), lambda i, j: (i, 0, 0)),
                  pl.BlockSpec((MEM_W, D_MODEL), lambda i, j: (0, 0)),
                  pl.BlockSpec((1, D_MODEL), lambda i, j: (0, 0)),
                  pl.BlockSpec((D_MODEL, 2 * LANES), lambda i, j: (0, 0)),
                  pl.BlockSpec((1, LANES), lambda i, j: (0, 0))],
        out_specs=(pl.BlockSpec((nb, tm, D_MODEL), lambda i, j: (i, j, 0)),
                   pl.BlockSpec((nb, tm * XROW_TILES, LANES), lambda i, j: (i, j, 0)),
                   pl.BlockSpec((nb, tm, LANES), lambda i, j: (i, j, 0))),
        compiler_params=_cparams(("arbitrary", "arbitrary")),
        name="cross_router",
    )(x1, qm, mk, mv, w_mo, norm_ffn, w_router, b_router)


def _row_tile(idx, tiles):
    return pl.ds(pl.multiple_of(idx * tiles, tiles), tiles)


def _dispatch_kernel(*refs, tokens, first_group):
    if first_group:
        pos_ref, tail_ref, hp_ref, xs_hbm, zero_ref, sem = refs
    else:
        pos_ref, tail_ref, hp_ref, _, xs_hbm, zero_ref, sem = refs

    def zero_tails():
        zero_ref[...] = jnp.zeros(zero_ref.shape, zero_ref.dtype)

        def tail_copy(e):
            dst = xs_hbm.at[pl.ds(pl.multiple_of(tail_ref[e] * XROW_TILES, XROW_TILES), MOE_STAGE * XROW_TILES)]
            return pltpu.make_async_copy(zero_ref, dst, sem.at[0])

        for e in range(N_EXPERTS):
            pl.when(tail_ref[e] >= 0)(lambda e=e: tail_copy(e).start())
        for e in range(N_EXPERTS):
            pl.when(tail_ref[e] >= 0)(lambda e=e: tail_copy(e).wait())

    if first_group:
        pl.when(pl.program_id(0) == 0)(zero_tails)

    def issue(t, carry):
        src = hp_ref.at[_row_tile(t, XROW_TILES)]
        for k in range(TOP_K):
            dst = xs_hbm.at[_row_tile(pos_ref[t * TOP_K + k], XROW_TILES)]
            pltpu.make_async_copy(src, dst, sem.at[1]).start(priority=k % 2)
        return carry

    lax.fori_loop(0, tokens, issue, 0, unroll=4)

    for k in range(TOP_K):
        pltpu.make_async_copy(hp_ref, xs_hbm.at[pl.ds(0, tokens * XROW_TILES)], sem.at[1]).wait()


def _dispatch(hp, pos_flat, tail_start, rows, tokens, xs_prev=None):
    t = hp.shape[0] // XROW_TILES
    first_group = xs_prev is None
    kern = functools.partial(_dispatch_kernel, tokens=tokens, first_group=first_group)
    in_specs = [pl.BlockSpec((tokens * TOP_K,), lambda i: (i,), memory_space=pltpu.SMEM),
                pl.BlockSpec((N_EXPERTS,), lambda i: (0,), memory_space=pltpu.SMEM),
                pl.BlockSpec((tokens * XROW_TILES, LANES), lambda i: (i, 0))]
    args = [pos_flat, tail_start, hp]
    if not first_group:
        in_specs.append(pl.BlockSpec(memory_space=pl.ANY))
        args.append(xs_prev)
    return pl.pallas_call(
        kern,
        out_shape=jax.ShapeDtypeStruct((rows * XROW_TILES, LANES), jnp.uint32),
        grid=(t // tokens,),
        in_specs=in_specs,
        out_specs=pl.BlockSpec(memory_space=pl.ANY),
        scratch_shapes=[pltpu.VMEM((MOE_STAGE * XROW_TILES, LANES), jnp.uint32),
                        pltpu.SemaphoreType.DMA((2,))],
        input_output_aliases={} if first_group else {3: 0},
        compiler_params=pltpu.CompilerParams(dimension_semantics=("arbitrary",),
                                             vmem_limit_bytes=VMEM_LIMIT_BYTES, has_side_effects=True),
        name="moe_dispatch",
    )(*args)


def _moe_kernel(ie_ref, is_ref, ns_ref, xs_hbm, wg_ref, bg_ref, wu_ref, bu_ref, wd_ref, bd_ref,
                ys_hbm, xu_ref, acc_ref, ystage_ref, in_sem, out_sem):
    w = pl.program_id(0)
    f = pl.program_id(1)
    nstage = ns_ref[w]
    nsub = nstage // (MOE_SUB // MOE_STAGE)
    start = is_ref[w]

    def stage_rows(s):
        return pl.ds(pl.multiple_of(s * MOE_STAGE, MOE_STAGE), MOE_STAGE)

    def hbm_rows(first, s, tiles):
        return pl.ds(pl.multiple_of((first + s * MOE_STAGE) * tiles, MOE_STAGE * tiles), MOE_STAGE * tiles)

    def load(first, s):
        dst = xu_ref.at[pl.ds(pl.multiple_of(s * MOE_STAGE * XROW_TILES, MOE_STAGE * XROW_TILES),
                              MOE_STAGE * XROW_TILES)]
        return pltpu.make_async_copy(xs_hbm.at[hbm_rows(first, s, XROW_TILES)], dst, in_sem.at[0])

    def fetch_item(first, stages):
        def issue(s, carry):
            load(first, s).start()
            return carry

        lax.fori_loop(0, stages, issue, 0)

    pl.when(jnp.logical_and(jnp.logical_and(w == 0, f == 0), nstage > 0))(lambda: fetch_item(start, nstage))

    @pl.when(jnp.logical_and(f == 0, nstage > 0))
    def _():
        def landed(s, carry):
            load(start, s).wait()
            return carry

        lax.fori_loop(0, nstage, landed, 0)

        def init(s, carry):
            acc_ref[stage_rows(s), :] = jnp.broadcast_to(bd_ref[...], (MOE_STAGE, D_MODEL))
            return carry

        lax.fori_loop(0, nstage, init, 0)

    def x_rows(row0, nrows):
        base = pl.multiple_of(row0 * XROW_TILES, MOE_STAGE * XROW_TILES)
        lows, highs = [], []
        for j in range(XROW_TILES):
            lo, hi = _unpack_halves(xu_ref[pl.ds(base + j, nrows, stride=XROW_TILES), :])
            lows.append(lo.astype(BF16))
            highs.append(hi.astype(BF16))
        return jnp.concatenate(lows + highs, axis=1)

    @pl.when(nstage > 0)
    def _():
        bg = bg_ref[...]
        bu = bu_ref[...]

        def expert_rows(row0, nrows):
            x = x_rows(row0, nrows)
            gt = jnp.dot(x, wg_ref[...].astype(BF16), preferred_element_type=F32) + bg
            up = jnp.dot(x, wu_ref[...].astype(BF16), preferred_element_type=F32) + bu
            gt = jnp.minimum(gt, SWIGLU_LIMIT)
            up = jnp.clip(up, -SWIGLU_LIMIT, SWIGLU_LIMIT)
            a = (gt * jax.nn.sigmoid(SWIGLU_ALPHA * gt) * (up + 1.0)).astype(BF16)
            rows = pl.ds(pl.multiple_of(row0, MOE_STAGE), nrows)
            acc_ref[rows, :] += jnp.dot(a, wd_ref[...].astype(BF16), preferred_element_type=F32)

        def body(s, carry):
            expert_rows(s * MOE_SUB, MOE_SUB)
            return carry

        lax.fori_loop(0, nsub, body, 0)
        done = nsub * MOE_SUB
        left = nstage - nsub * (MOE_SUB // MOE_STAGE)
        size = MOE_SUB // 2
        while size >= MOE_STAGE:
            here = left * MOE_STAGE >= size
            pl.when(here)(lambda done=done, size=size: expert_rows(done, size))
            done = done + jnp.where(here, size, 0)
            left = left - jnp.where(here, size // MOE_STAGE, 0)
            size //= 2

    @pl.when(jnp.logical_and(f == MOE_NF - 1, nstage > 0))
    def _():
        nxt = jnp.minimum(w + 1, pl.num_programs(0) - 1)
        nxt_stages = jnp.where(w + 1 < pl.num_programs(0), ns_ref[nxt], 0)
        fetch_item(is_ref[nxt], nxt_stages)

        def store(s):
            slot = s % MOE_OUT_SLOTS
            return pltpu.make_async_copy(ystage_ref.at[slot], ys_hbm.at[hbm_rows(start, s, YROW_TILES)],
                                         out_sem.at[slot])

        def write(s, carry):
            pl.when(s >= MOE_OUT_SLOTS)(lambda: store(s - MOE_OUT_SLOTS).wait())
            for j in range(YROW_TILES):
                lo = acc_ref[stage_rows(s), j * LANES:(j + 1) * LANES]
                hi = acc_ref[stage_rows(s), HALF + j * LANES:HALF + (j + 1) * LANES]
                ystage_ref[s % MOE_OUT_SLOTS, pl.ds(j, MOE_STAGE, stride=YROW_TILES), :] = _pack_halves(lo, hi)
            store(s).start()
            return carry

        lax.fori_loop(0, nstage, write, 0)
        for back in range(1, MOE_OUT_SLOTS + 1):
            pl.when(nstage >= back)(lambda back=back: store(nstage - back).wait())


def _moe(xs, item_expert, item_start, item_nsub, w_gate, b_gate, w_up, b_up, w_down, b_down):
    rows = xs.shape[0] // XROW_TILES
    nitems = item_expert.shape[0]

    def fcol(w, f, ie, st, ns):
        return jnp.where(ns[w] > 0, f, MOE_NF - 1)

    grid_spec = pltpu.PrefetchScalarGridSpec(
        num_scalar_prefetch=3,
        grid=(nitems, MOE_NF),
        in_specs=[pl.BlockSpec(memory_space=pl.ANY),
                  pl.BlockSpec((None, D_MODEL, MOE_TF), lambda w, f, ie, st, ns: (ie[w], 0, fcol(w, f, ie, st, ns))),
                  pl.BlockSpec((None, 1, MOE_TF), lambda w, f, ie, st, ns: (ie[w], 0, fcol(w, f, ie, st, ns))),
                  pl.BlockSpec((None, D_MODEL, MOE_TF), lambda w, f, ie, st, ns: (ie[w], 0, fcol(w, f, ie, st, ns))),
                  pl.BlockSpec((None, 1, MOE_TF), lambda w, f, ie, st, ns: (ie[w], 0, fcol(w, f, ie, st, ns))),
                  pl.BlockSpec((None, MOE_TF, D_MODEL), lambda w, f, ie, st, ns: (ie[w], fcol(w, f, ie, st, ns), 0)),
                  pl.BlockSpec((None, 1, D_MODEL), lambda w, f, ie, st, ns: (ie[w], 0, 0))],
        out_specs=pl.BlockSpec(memory_space=pl.ANY),
        scratch_shapes=[pltpu.VMEM((MOE_ROWS * XROW_TILES, LANES), jnp.uint32),
                        pltpu.VMEM((MOE_ROWS, D_MODEL), F32),
                        pltpu.VMEM((MOE_OUT_SLOTS, MOE_STAGE * YROW_TILES, LANES), jnp.uint32),
                        pltpu.SemaphoreType.DMA((1,)),
                        pltpu.SemaphoreType.DMA((MOE_OUT_SLOTS,))])
    return pl.pallas_call(
        _moe_kernel,
        out_shape=jax.ShapeDtypeStruct((rows * YROW_TILES, LANES), jnp.uint32),
        grid_spec=grid_spec,
        compiler_params=pltpu.CompilerParams(dimension_semantics=("arbitrary", "arbitrary"),
                                             vmem_limit_bytes=MOE_VMEM_LIMIT_BYTES, has_side_effects=True),
        name="moe_experts",
    )(item_expert, item_start, item_nsub, xs, w_gate, b_gate, w_up, b_up, w_down, b_down)


def _combine_kernel(pos_ref, nxt_ref, x2_ref, route_ref, g_ref, ys_hbm, o_ref, buf_ref, sem, *, tokens):
    i = pl.program_id(0)
    slot = i % 2

    def gather(p_ref, dst_slot):
        def issue(t, carry):
            for k in range(TOP_K):
                pltpu.make_async_copy(ys_hbm.at[_row_tile(p_ref[t * TOP_K + k], YROW_TILES)],
                                      buf_ref.at[dst_slot, k, _row_tile(t, YROW_TILES)],
                                      sem.at[dst_slot]).start(priority=k % 2)
            return carry

        lax.fori_loop(0, tokens, issue, 0, unroll=4)

    pl.when(i == 0)(lambda: gather(pos_ref, 0))
    pl.when(i + 1 < pl.num_programs(0))(lambda: gather(nxt_ref, 1 - slot))

    for k in range(TOP_K):
        pltpu.make_async_copy(ys_hbm.at[pl.ds(0, tokens * YROW_TILES)], buf_ref.at[slot, k], sem.at[slot]).wait()

    route = route_ref[...]
    gates = [route[:, TOP_K + k:TOP_K + k + 1] for k in range(TOP_K)]
    lows, highs = [], []
    for j in range(YROW_TILES):
        y_lo = x2_ref[:, j * LANES:(j + 1) * LANES]
        y_hi = x2_ref[:, HALF + j * LANES:HALF + (j + 1) * LANES]
        for k in range(TOP_K):
            lo, hi = _unpack_halves(buf_ref[slot, k, pl.ds(j, tokens, stride=YROW_TILES), :])
            y_lo = y_lo + gates[k] * lo
            y_hi = y_hi + gates[k] * hi
        lows.append(y_lo)
        highs.append(y_hi)
    o_ref[...] = _rms(jnp.concatenate(lows + highs, axis=1), g_ref[...])


def _combine(x2, route, final_norm, ys, pos_flat, tokens):
    t = x2.shape[0]
    nsteps = t // tokens
    kern = functools.partial(_combine_kernel, tokens=tokens)
    return pl.pallas_call(
        kern,
        out_shape=jax.ShapeDtypeStruct((t, D_MODEL), F32),
        grid=(nsteps,),
        in_specs=[pl.BlockSpec((tokens * TOP_K,), lambda i: (i,), memory_space=pltpu.SMEM),
                  pl.BlockSpec((tokens * TOP_K,), lambda i: (jnp.minimum(i + 1, nsteps - 1),),
                               memory_space=pltpu.SMEM),
                  pl.BlockSpec((tokens, D_MODEL), lambda i: (i, 0)),
                  pl.BlockSpec((tokens, LANES), lambda i: (i, 0)),
                  pl.BlockSpec((1, D_MODEL), lambda i: (0, 0)),
                  pl.BlockSpec(memory_space=pl.ANY)],
        out_specs=pl.BlockSpec((tokens, D_MODEL), lambda i: (i, 0)),
        scratch_shapes=[pltpu.VMEM((2, TOP_K, tokens * YROW_TILES, LANES), jnp.uint32),
                        pltpu.SemaphoreType.DMA((2,))],
        compiler_params=_cparams(("arbitrary",)),
        name="moe_combine",
    )(pos_flat, pos_flat, x2, route, final_norm, ys)


def _routing_tables(top_i, nitems):
    t = top_i.shape[0]
    sel = jnp.sum((top_i[:, :, None] == jnp.arange(N_EXPERTS, dtype=jnp.int32)[None, None, :]).astype(jnp.int32),
                  axis=1)
    cnt = jnp.sum(sel, axis=0)
    rank = jnp.cumsum(sel, axis=0) - sel
    cnt_pad = ((cnt + MOE_STAGE - 1) // MOE_STAGE) * MOE_STAGE
    off = jnp.cumsum(cnt_pad) - cnt_pad
    pos = jnp.take_along_axis(off[None, :] + rank, top_i, axis=1)
    tail_start = jnp.where(cnt > 0, off + cnt_pad - MOE_STAGE, -1)
    items_per = (cnt + MOE_ROWS - 1) // MOE_ROWS
    item_end = jnp.cumsum(items_per)
    total = item_end[-1]
    widx = jnp.arange(nitems, dtype=jnp.int32)
    e_of = jnp.minimum(jnp.searchsorted(item_end, widx, side="right"), N_EXPERTS - 1).astype(jnp.int32)
    j_of = widx - (item_end - items_per)[e_of]
    valid = widx < total
    e_last = e_of[jnp.maximum(total - 1, 0)]
    item_expert = jnp.where(valid, e_of, e_last).astype(jnp.int32)
    item_start = jnp.where(valid, off[e_of] + j_of * MOE_ROWS, 0).astype(jnp.int32)
    rows_left = cnt_pad[e_of] - j_of * MOE_ROWS
    item_nsub = jnp.where(valid, jnp.minimum(rows_left, MOE_ROWS) // MOE_STAGE, 0).astype(jnp.int32)
    return pos.reshape(t * TOP_K).astype(jnp.int32), tail_start.astype(jnp.int32), item_expert, item_start, item_nsub


def _trunk(x, conv_hist, s0, swa_fn, mk, mv, gdn_chunk, p):
    b, t, _ = x.shape
    m = b * t
    xf = x.reshape(m, D_MODEL)
    proj = _norm_matmul(xf, p["norm_mix"], p["w_in"], min(m, 1024), PROJ_TN)
    proj3 = proj.reshape(b, t, PROJ_COLS)
    hist8 = jnp.pad(conv_hist, ((0, 0), (SUBLANES - (CONV_W - 1), 0), (0, 0)))
    o_a, conv8, s_new = _gdn(proj3, hist8, s0, p["conv_w"], p["gate_par"], p["gdn_norm"], gdn_chunk,
                             GDN_SEQS_PER_STEP)
    o_b = swa_fn(proj3)
    x1, qm = _outproj(xf, o_a.reshape(m, V_A), o_b.reshape(m, Q_B), p["w_out"], p["norm_cross"], p["w_mq"], 256)
    tm = min(t, CROSS_ROWS)
    x2, hp, route = _cross_router(x1.reshape(b, t, D_MODEL), qm.reshape(b, t, MEM_W), mk, mv,
                                  p["w_mo"], p["norm_ffn"], p["w_router"], p["b_router"], CROSS_ROWS // tm, tm)
    return proj3, conv8[:, SUBLANES - (CONV_W - 1):], s_new, x2.reshape(m, D_MODEL), \
        hp.reshape(m * XROW_TILES, LANES), route.reshape(m, LANES)


def kernel(x_prompt, x_sample, cache_conv, state_gdn, cache_swa_k, cache_swa_v, cache_mem_k, cache_mem_v, mem_prompt, norm_mix, w_in, conv_w, gdn_a_log, gdn_dt_bias, gdn_norm, swa_sinks, rel_bias_table, w_out, norm_cross, norm_mem, w_mq, w_mk, w_mv, w_mo, norm_ffn, w_router, b_router, w_gate, b_gate, w_up, b_up, w_down, b_down, final_norm):
    depth = norm_mix.shape[0]
    assert depth == 1, "kernel is written for the single-layer trunk"
    bp, sp, _ = x_prompt.shape
    bs, ss, _ = x_sample.shape
    l = 0
    w = w_in[l]
    n_ab = 2 * GDN_HEADS
    w_perm = jnp.concatenate(
        [w[:, :CONV_CH + V_A], w[:, CONV_CH + V_A + n_ab:], w[:, CONV_CH + V_A:CONV_CH + V_A + n_ab],
         jnp.zeros((D_MODEL, PROJ_COLS - w.shape[1]), w.dtype)], axis=1).astype(BF16)
    gate_par = jnp.zeros((SUBLANES, LANES), F32)
    gate_par = gate_par.at[0, :GDN_HEADS].set(gdn_a_log[l]).at[1, :GDN_HEADS].set(gdn_dt_bias[l])
    p = dict(
        norm_mix=norm_mix[l].reshape(1, D_MODEL), w_in=w_perm, conv_w=conv_w[l], gate_par=gate_par,
        gdn_norm=gdn_norm[l].reshape(1, GDN_D), w_out=w_out[l].astype(BF16),
        norm_cross=norm_cross[l].reshape(1, D_MODEL), w_mq=w_mq[l].astype(BF16), w_mo=w_mo[l].astype(BF16),
        norm_ffn=norm_ffn[l].reshape(1, D_MODEL),
        w_router=_split_bf16(jnp.pad(w_router[l], ((0, 0), (0, LANES - N_EXPERTS)))),
        b_router=jnp.pad(b_router[l], (0, LANES - N_EXPERTS)).reshape(1, LANES))

    w_mkv = jnp.concatenate([w_mk[l], w_mv[l]], axis=1).astype(BF16)
    mkv = _norm_matmul(mem_prompt.reshape(bp * MEM_LEN, D_MODEL), norm_mem[l].reshape(1, D_MODEL), w_mkv,
                       min(bp * MEM_LEN, 1024), MEMKV_TN)
    mk_p = mkv[:, :MEM_W].reshape(bp, MEM_LEN * MEM_HEADS, MEM_DH)
    mv_p = mkv[:, MEM_W:].reshape(bp, MEM_LEN * MEM_HEADS, MEM_DH)

    zero_hist = jnp.zeros((bp, CONV_W - 1, CONV_CH), F32)
    zero_state = jnp.zeros((bp, GDN_HEADS, GDN_D, GDN_D), F32)
    swa_p = functools.partial(_swa_prompt, table=rel_bias_table, sinks=swa_sinks[l])
    proj_p, conv_p, st_p, x2_p, hp_p, route_p = _trunk(x_prompt, zero_hist, zero_state, swa_p, mk_p, mv_p, CHUNK, p)

    k_past = cache_swa_k[l].reshape(bs, WIN_PAST, KV_B)
    v_past = cache_swa_v[l].reshape(bs, WIN_PAST, KV_B)
    swa_s = functools.partial(_swa_sample, k_past=k_past, v_past=v_past, table=rel_bias_table, sinks=swa_sinks[l])
    proj_s, conv_s, st_s, x2_s, hp_s, route_s = _trunk(
        x_sample, cache_conv[l], state_gdn[l], swa_s,
        cache_mem_k[l].reshape(bs, MEM_LEN * MEM_HEADS, MEM_DH),
        cache_mem_v[l].reshape(bs, MEM_LEN * MEM_HEADS, MEM_DH), ss, p)

    mp, ms = bp * sp, bs * ss
    ntok = mp + ms
    top_i = jnp.concatenate([route_p[:, :TOP_K], route_s[:, :TOP_K]], axis=0).astype(jnp.int32)
    rows = ntok * TOP_K + N_EXPERTS * MOE_STAGE
    nitems = (ntok * TOP_K) // MOE_ROWS + N_EXPERTS
    pos_flat, tail_start, item_expert, item_start, item_nsub = _routing_tables(top_i, nitems)
    pos_p, pos_s = pos_flat[:mp * TOP_K], pos_flat[mp * TOP_K:]
    xs = _dispatch(hp_p, pos_p, tail_start, rows, 512)
    xs = _dispatch(hp_s, pos_s, tail_start, rows, 512, xs_prev=xs)
    ys = _moe(xs, item_expert, item_start, item_nsub,
              w_gate[l], b_gate[l].reshape(N_EXPERTS, 1, D_FF), w_up[l], b_up[l].reshape(N_EXPERTS, 1, D_FF),
              w_down[l], b_down[l].reshape(N_EXPERTS, 1, D_MODEL))
    fnorm = final_norm.reshape(1, D_MODEL)
    y_p = _combine(x2_p, route_p, fnorm, ys, pos_p, 256).reshape(bp, sp, D_MODEL)
    y_s = _combine(x2_s, route_s, fnorm, ys, pos_s, 256).reshape(bs, ss, D_MODEL)

    def kv_window(proj3, col, past=None):
        new = proj3[:, :, col * LANES:(col + 1) * LANES]
        full = new if past is None else jnp.concatenate([past, new], axis=1)
        win = full[:, -WIN_PAST:]
        return win.reshape(win.shape[0], WIN_PAST, SWA_KV_HEADS, SWA_DH)[None]

    return (y_p, y_s,
            conv_p[None], st_p[None], kv_window(proj_p, COL_KB), kv_window(proj_p, COL_VB),
            mk_p.reshape(bp, MEM_LEN, MEM_HEADS, MEM_DH)[None], mv_p.reshape(bp, MEM_LEN, MEM_HEADS, MEM_DH)[None],
            conv_s[None], st_s[None], kv_window(proj_s, COL_KB, k_past), kv_window(proj_s, COL_VB, v_past))
```

```python
import functools
import math

import numpy as np
import jax
import jax.numpy as jnp
from jax import lax
from jax.experimental import pallas as pl
from jax.experimental.pallas import tpu as pltpu

F32 = jnp.float32
BF16 = jnp.bfloat16
HIGHEST = lax.Precision.HIGHEST

D_MODEL = 2048
CHUNK = 64
GDN_HEADS = 8
GDN_D = 128
CONV_W = 4
SWA_HEADS = 16
SWA_KV_HEADS = 2
SWA_GROUP = SWA_HEADS // SWA_KV_HEADS
SWA_DH = 64
WIN_PAST = 128
REL_BUCKETS = 32
REL_MAX_DIST = 128
MEM_LEN = 256
MEM_HEADS = 4
MEM_DH = 128
N_EXPERTS = 32
TOP_K = 4
D_FF = D_MODEL
SWIGLU_LIMIT = 7.0
SWIGLU_ALPHA = 1.702
EPS = 1e-6

QK_A = GDN_HEADS * GDN_D
V_A = GDN_HEADS * GDN_D
CONV_CH = 2 * QK_A + V_A
Q_B = SWA_HEADS * SWA_DH
KV_B = SWA_KV_HEADS * SWA_DH
MEM_W = MEM_HEADS * MEM_DH

LANES = 128
SUBLANES = 8
VMEM_LIMIT_BYTES = 58 * 1024 * 1024
MOE_VMEM_LIMIT_BYTES = 60 * 1024 * 1024

PROJ_COLS = 5632
PROJ_TN = 1408
MEMKV_TN = 512
COL_Z = CONV_CH // V_A
COL_QB = (CONV_CH + V_A) // Q_B
COL_KB = (CONV_CH + V_A + Q_B) // LANES
COL_VB = COL_KB + 1
COL_AB = COL_KB + 2

MOE_SUB = 512
MOE_ROWS = 1536
MOE_STAGE = 128
MOE_OUT_SLOTS = 4
MOE_TF = 512
MOE_NF = D_FF // MOE_TF
HALF = D_MODEL // 2
NORM_ROWS = 256
SWA_CHUNKS_PER_STEP = 4
CROSS_ROWS = 256
DMA_ISSUE_UNROLL = 8
OUTPROJ_ROWS = 512
GDN_SEQS_PER_STEP = 4
XROW_TILES = HALF // LANES
YROW_TILES = HALF // LANES


def _cparams(sem):
    return pltpu.CompilerParams(dimension_semantics=sem, vmem_limit_bytes=VMEM_LIMIT_BYTES)


def _rms(x, gain):
    return x * lax.rsqrt(jnp.mean(x * x, axis=-1, keepdims=True) + EPS) * gain


def _pack_halves(lo, hi):
    lo_bits = pltpu.bitcast(lo.astype(BF16).astype(F32), jnp.uint32)
    hi_bits = pltpu.bitcast(hi.astype(BF16).astype(F32), jnp.uint32)
    return (lo_bits >> 16) | (hi_bits & jnp.uint32(0xFFFF0000))


def _unpack_halves(u):
    return pltpu.bitcast(u << 16, F32), pltpu.bitcast(u & jnp.uint32(0xFFFF0000), F32)


def _norm_matmul_kernel(x_ref, g_ref, w_ref, o_ref, h_ref):
    @pl.when(pl.program_id(1) == 0)
    def _():
        def body(r, carry):
            rows = pl.ds(pl.multiple_of(r * NORM_ROWS, NORM_ROWS), NORM_ROWS)
            h_ref[rows, :] = _rms(x_ref[rows, :], g_ref[...]).astype(BF16)
            return carry

        lax.fori_loop(0, x_ref.shape[0] // NORM_ROWS, body, 0)

    o_ref[...] = jnp.dot(h_ref[...], w_ref[...], preferred_element_type=F32)


def _norm_matmul(x, gain, w, tm, tn):
    m, k = x.shape
    n = w.shape[1]
    return pl.pallas_call(
        _norm_matmul_kernel,
        out_shape=jax.ShapeDtypeStruct((m, n), F32),
        grid=(m // tm, n // tn),
        in_specs=[pl.BlockSpec((tm, k), lambda i, j: (i, 0)),
                  pl.BlockSpec((1, k), lambda i, j: (0, 0)),
                  pl.BlockSpec((k, tn), lambda i, j: (0, j))],
        out_specs=pl.BlockSpec((tm, tn), lambda i, j: (i, j)),
        scratch_shapes=[pltpu.VMEM((tm, k), BF16)],
        compiler_params=_cparams(("arbitrary", "arbitrary")),
        name="norm_matmul",
    )(x, gain, w)


def _tri_inverse(lows, c):
    row = lax.broadcasted_iota(jnp.int32, (c, c), 0)
    col = lax.broadcasted_iota(jnp.int32, (c, c), 1)
    eye = jnp.where(row == col, 1.0, 0.0).astype(F32)
    ps = [eye - low for low in lows]
    ms = list(lows)
    span = 1
    while 2 * span < c:
        mbs = [m.astype(BF16) for m in ms]
        ms = [jnp.dot(mb, mb, preferred_element_type=F32) for mb in mbs]
        ps = [p + jnp.dot(p.astype(BF16), m.astype(BF16), preferred_element_type=F32) for p, m in zip(ps, ms)]
        span *= 2
    return ps


def _gdn_kernel(qkv_ref, z_ref, ab_ref, hist_ref, s0_ref, convw_ref, gp_ref, norm_ref,
                o_ref, convnew_ref, s_ref, ext_ref, *, c):
    step = pl.program_id(1)
    nb = qkv_ref.shape[0]

    @pl.when(step == 0)
    def _():
        ext_ref[:, 0:SUBLANES, :] = hist_ref[...]
        s_ref[...] = s0_ref[...]

    row = lax.broadcasted_iota(jnp.int32, (c, c), 0)
    col = lax.broadcasted_iota(jnp.int32, (c, c), 1)
    incl = row >= col
    strict = row > col
    tril = jnp.where(incl, 1.0, 0.0).astype(F32)
    gp = gp_ref[...]
    scale = GDN_D ** -0.5
    heads = [(bi, h) for bi in range(nb) for h in range(GDN_HEADS)]
    s_old = [s_ref[bi, h] for bi, h in heads]

    acts, gc_alls, gc_ts, beta_alls = [], [], [], []
    for bi in range(nb):
        ext_ref[bi, SUBLANES:SUBLANES + c, :] = qkv_ref[bi]
        base = SUBLANES - (CONV_W - 1)
        conv = ext_ref[bi, base:base + c, :] * convw_ref[0:1, :]
        for j in range(1, CONV_W):
            conv = conv + ext_ref[bi, base + j:base + j + c, :] * convw_ref[j:j + 1, :]
        tail = ext_ref[bi, c:c + SUBLANES, :]
        convnew_ref[bi] = tail
        ext_ref[bi, 0:SUBLANES, :] = tail
        acts.append(conv * jax.nn.sigmoid(conv))
        ab = ab_ref[bi]
        g = -jnp.exp(gp[0:1, :]) * jax.nn.softplus(ab + gp[1:2, :])
        beta_alls.append(jax.nn.sigmoid(ab))
        gc_all = jnp.dot(tril, g, precision=HIGHEST, preferred_element_type=F32)
        gc_pad = jnp.concatenate([gc_all, jnp.zeros((LANES - c, LANES), F32)], axis=0) if c < LANES else gc_all
        gc_alls.append(gc_all)
        gc_ts.append(gc_pad.T)

    q, k, v = [], [], []
    for bi, h in heads:
        lo = h * GDN_D
        qh = acts[bi][:, lo:lo + GDN_D]
        kh = acts[bi][:, QK_A + lo:QK_A + lo + GDN_D]
        q.append(qh * lax.rsqrt(jnp.sum(qh * qh, axis=-1, keepdims=True) + EPS) * scale)
        k.append(kh * lax.rsqrt(jnp.sum(kh * kh, axis=-1, keepdims=True) + EPS))
        v.append(acts[bi][:, 2 * QK_A + lo:2 * QK_A + lo + GDN_D])
    beta = [beta_alls[bi][:, GDN_HEADS + h:GDN_HEADS + h + 1] for bi, h in heads]
    gc = [gc_alls[bi][:, h:h + 1] for bi, h in heads]
    gc_last = [gc_alls[bi][c - 1:c, h:h + 1] for bi, h in heads]
    heads = range(len(heads))
    decay = [jnp.exp(jnp.where(incl, gc[h] - gc_ts[h // GDN_HEADS][h % GDN_HEADS:h % GDN_HEADS + 1, 0:c],
                               -jnp.inf)) for h in heads]
    egc = [jnp.exp(gc[h]) for h in heads]
    kb = [k[h] * beta[h] for h in heads]
    qk = [lax.dot_general(jnp.concatenate([q[h], kb[h]], axis=0).astype(BF16), k[h].astype(BF16),
                          (((1,), (1,)), ((), ())), preferred_element_type=F32) for h in heads]
    a_qk = [jnp.where(incl, qk[h][0:c] * decay[h], 0.0) for h in heads]
    tinv = _tri_inverse([jnp.where(strict, qk[h][c:2 * c] * decay[h], 0.0) for h in heads], c)
    uw = [jnp.dot(tinv[h].astype(BF16),
                  jnp.concatenate([v[h] * beta[h], kb[h] * egc[h]], axis=1).astype(BF16),
                  preferred_element_type=F32) for h in heads]
    ws_qs = [jnp.dot(jnp.concatenate([uw[h][:, GDN_D:2 * GDN_D], q[h] * egc[h]], axis=0).astype(BF16),
                     s_old[h].astype(BF16), preferred_element_type=F32) for h in heads]
    v_new = [(uw[h][:, 0:GDN_D] - ws_qs[h][0:c]).astype(BF16) for h in heads]
    o = [ws_qs[h][c:2 * c] + jnp.dot(a_qk[h].astype(BF16), v_new[h], preferred_element_type=F32)
         for h in heads]
    s_new = [s_old[h] * jnp.exp(gc_last[h]) + lax.dot_general(
        (k[h] * jnp.exp(gc_last[h] - gc[h])).astype(BF16), v_new[h], (((0,), (0,)), ((), ())),
        preferred_element_type=F32) for h in heads]
    for bi in range(nb):
        z_all = z_ref[bi]
        gate = z_all * jax.nn.sigmoid(z_all)
        o_ref[bi] = jnp.concatenate([_rms(o[bi * GDN_HEADS + h], norm_ref[...]) for h in range(GDN_HEADS)],
                                    axis=1) * gate
    for h in heads:
        s_ref[h // GDN_HEADS, h % GDN_HEADS] = s_new[h]


def _gdn(proj3, hist8, s0, conv_w, gate_par, gdn_norm, c, nb):
    b, t, _ = proj3.shape
    nchunks = t // c
    kern = functools.partial(_gdn_kernel, c=c)
    return pl.pallas_call(
        kern,
        out_shape=(jax.ShapeDtypeStruct((b, t, V_A), F32),
                   jax.ShapeDtypeStruct((b, SUBLANES, CONV_CH), F32),
                   jax.ShapeDtypeStruct((b, GDN_HEADS, GDN_D, GDN_D), F32)),
        grid=(b // nb, nchunks),
        in_specs=[pl.BlockSpec((nb, c, CONV_CH), lambda i, j: (i, j, 0)),
                  pl.BlockSpec((nb, c, V_A), lambda i, j: (i, j, COL_Z)),
                  pl.BlockSpec((nb, c, LANES), lambda i, j: (i, j, COL_AB)),
                  pl.BlockSpec((nb, SUBLANES, CONV_CH), lambda i, j: (i, 0, 0)),
                  pl.BlockSpec((nb, GDN_HEADS, GDN_D, GDN_D), lambda i, j: (i, 0, 0, 0)),
                  pl.BlockSpec((CONV_W, CONV_CH), lambda i, j: (0, 0)),
                  pl.BlockSpec((SUBLANES, LANES), lambda i, j: (0, 0)),
                  pl.BlockSpec((1, GDN_D), lambda i, j: (0, 0))],
        out_specs=(pl.BlockSpec((nb, c, V_A), lambda i, j: (i, j, 0)),
                   pl.BlockSpec((nb, SUBLANES, CONV_CH), lambda i, j: (i, 0, 0)),
                   pl.BlockSpec((nb, GDN_HEADS, GDN_D, GDN_D), lambda i, j: (i, 0, 0, 0))),
        scratch_shapes=[pltpu.VMEM((nb, SUBLANES + c, CONV_CH), F32)],
        compiler_params=_cparams(("arbitrary", "arbitrary")),
        name="gdn_mixer",
    )(proj3, proj3, proj3, hist8, s0, conv_w, gate_par, gdn_norm)


def _swa_kernel(*refs, nq, nsub, piece_rows, masked):
    npieces = len(piece_rows)
    q_ref = refs[0]
    k_refs = refs[1:1 + npieces]
    v_refs = refs[1 + npieces:1 + 2 * npieces]
    bias_ref, sink_ref, o_ref = refs[1 + 2 * npieces:]
    nk = WIN_PAST + nq
    kcat = jnp.concatenate([r[...] for r in k_refs], axis=0).astype(BF16)
    vcat = jnp.concatenate([r[...] for r in v_refs], axis=0).astype(BF16)
    q = q_ref[...]
    kidx = lax.broadcasted_iota(jnp.int32, (SWA_GROUP * nq, nk), 1)
    blocks = [(i, kv) for i in range(nsub) for kv in range(SWA_KV_HEADS)]
    scores = []
    for i, kv in blocks:
        qs = jnp.concatenate(
            [q[i * nq:(i + 1) * nq, (kv * SWA_GROUP + g) * SWA_DH:(kv * SWA_GROUP + g + 1) * SWA_DH]
             for g in range(SWA_GROUP)], axis=0).astype(BF16)
        kh = kcat[i * nq:i * nq + nk, kv * SWA_DH:(kv + 1) * SWA_DH]
        s = lax.dot_general(qs, kh, (((1,), (1,)), ((), ())), preferred_element_type=F32)
        s = s * SWA_DH ** -0.5 + bias_ref[kv]
        if masked:
            first = (pl.program_id(1) * nsub + i) * nq - WIN_PAST
            s = jnp.where(kidx + first >= 0, s, -jnp.inf)
        scores.append(s)
    sinks = [sink_ref[kv] for _, kv in blocks]
    maxes = [jnp.maximum(jnp.max(s, axis=-1, keepdims=True), sk) for s, sk in zip(scores, sinks)]
    probs = [jnp.exp(s - m) for s, m in zip(scores, maxes)]
    dens = [jnp.sum(p, axis=-1, keepdims=True) + jnp.exp(sk - m) for p, sk, m in zip(probs, sinks, maxes)]
    outs = [jnp.dot((p / den).astype(BF16), vcat[i * nq:i * nq + nk, kv * SWA_DH:(kv + 1) * SWA_DH],
                    preferred_element_type=F32) for (i, kv), p, den in zip(blocks, probs, dens)]
    out_rows = []
    for i in range(nsub):
        heads = []
        for kv in range(SWA_KV_HEADS):
            o = outs[i * SWA_KV_HEADS + kv]
            heads += [o[g * nq:(g + 1) * nq, :] for g in range(SWA_GROUP)]
        out_rows.append(jnp.concatenate(heads, axis=1))
    o_ref[...] = out_rows[0] if nsub == 1 else jnp.concatenate(out_rows, axis=0)


def _rel_bucket(nq, nk):
    rel = jnp.arange(nk)[None, :] - WIN_PAST - jnp.arange(nq)[:, None]
    nb = REL_BUCKETS // 2
    max_exact = nb // 2
    n = jnp.abs(rel)
    large = max_exact + (jnp.log(jnp.maximum(n, 1).astype(F32) / max_exact)
                         / math.log(REL_MAX_DIST / max_exact) * (nb - max_exact)).astype(jnp.int32)
    large = jnp.minimum(large, nb - 1)
    return jnp.where(rel > 0, nb, 0) + jnp.where(n < max_exact, n, large)


def _swa_tables(table, sinks, nq, nk):
    onehot = (_rel_bucket(nq, nk)[:, :, None] == jnp.arange(REL_BUCKETS)[None, None, :]).astype(F32)
    bias = jnp.einsum("qkb,bh->qkh", onehot, table.astype(F32), precision=HIGHEST)
    bias = jnp.transpose(bias, (2, 0, 1)).astype(F32)
    bias = bias.reshape(SWA_KV_HEADS, SWA_GROUP * nq, nk)
    sk = jnp.broadcast_to(sinks.astype(F32).reshape(SWA_KV_HEADS, SWA_GROUP, 1, 1),
                          (SWA_KV_HEADS, SWA_GROUP, nq, 1)).reshape(SWA_KV_HEADS, SWA_GROUP * nq, 1)
    return bias, sk


def _swa_prompt(proj3, table, sinks):
    b, t, _ = proj3.shape
    nq = CHUNK
    nsub = SWA_CHUNKS_PER_STEP
    rows = nsub * nq
    assert rows % WIN_PAST == 0 and t % rows == 0
    past_per_step = rows // WIN_PAST
    bias, sk = _swa_tables(table, sinks, nq, WIN_PAST + nq)
    kern = functools.partial(_swa_kernel, nq=nq, nsub=nsub, piece_rows=(WIN_PAST, rows), masked=True)

    def past_spec(colblk):
        return pl.BlockSpec((None, WIN_PAST, LANES),
                            lambda i, j: (i, jnp.maximum(j * past_per_step - 1, 0), colblk))

    def cur_spec(colblk):
        return pl.BlockSpec((None, rows, LANES), lambda i, j: (i, j, colblk))

    in_specs = [pl.BlockSpec((None, rows, Q_B), lambda i, j: (i, j, COL_QB)),
                past_spec(COL_KB), cur_spec(COL_KB), past_spec(COL_VB), cur_spec(COL_VB),
                pl.BlockSpec(bias.shape, lambda i, j: (0, 0, 0)),
                pl.BlockSpec(sk.shape, lambda i, j: (0, 0, 0))]
    return pl.pallas_call(
        kern,
        out_shape=jax.ShapeDtypeStruct((b, t, Q_B), F32),
        grid=(b, t // rows),
        in_specs=in_specs,
        out_specs=pl.BlockSpec((None, rows, Q_B), lambda i, j: (i, j, 0)),
        compiler_params=_cparams(("arbitrary", "arbitrary")),
        name="swa_prompt",
    )(proj3, proj3, proj3, proj3, proj3, bias, sk)


def _swa_sample(proj3, k_past, v_past, table, sinks):
    b, t, _ = proj3.shape
    bias, sk = _swa_tables(table, sinks, t, WIN_PAST + t)
    kern = functools.partial(_swa_kernel, nq=t, nsub=1, piece_rows=(WIN_PAST, t), masked=False)
    in_specs = [pl.BlockSpec((None, t, Q_B), lambda i, j: (i, 0, COL_QB)),
                pl.BlockSpec((None, WIN_PAST, LANES), lambda i, j: (i, 0, 0)),
                pl.BlockSpec((None, t, LANES), lambda i, j: (i, 0, COL_KB)),
                pl.BlockSpec((None, WIN_PAST, LANES), lambda i, j: (i, 0, 0)),
                pl.BlockSpec((None, t, LANES), lambda i, j: (i, 0, COL_VB)),
                pl.BlockSpec(bias.shape, lambda i, j: (0, 0, 0)),
                pl.BlockSpec(sk.shape, lambda i, j: (0, 0, 0))]
    return pl.pallas_call(
        kern,
        out_shape=jax.ShapeDtypeStruct((b, t, Q_B), F32),
        grid=(b, 1),
        in_specs=in_specs,
        out_specs=pl.BlockSpec((None, t, Q_B), lambda i, j: (i, 0, 0)),
        compiler_params=_cparams(("arbitrary", "arbitrary")),
        name="swa_sample",
    )(proj3, k_past, proj3, v_past, proj3, bias, sk)


def _outproj_kernel(x_ref, oa_ref, ob_ref, wa_ref, wb_ref, g_ref, wq_ref, x1_ref, q_ref):
    x1 = (x_ref[...]
          + jnp.dot(oa_ref[...].astype(BF16), wa_ref[...], preferred_element_type=F32)
          + jnp.dot(ob_ref[...].astype(BF16), wb_ref[...], preferred_element_type=F32))
    x1_ref[...] = x1
    hc = _rms(x1, g_ref[...]).astype(BF16)
    q_ref[...] = jnp.dot(hc, wq_ref[...], preferred_element_type=F32)


def _outproj(x, o_a, o_b, w_out, norm_cross, w_mq, tm):
    m = x.shape[0]
    return pl.pallas_call(
        _outproj_kernel,
        out_shape=(jax.ShapeDtypeStruct((m, D_MODEL), F32), jax.ShapeDtypeStruct((m, MEM_W), F32)),
        grid=(m // tm,),
        in_specs=[pl.BlockSpec((tm, D_MODEL), lambda i: (i, 0)),
                  pl.BlockSpec((tm, V_A), lambda i: (i, 0)),
                  pl.BlockSpec((tm, Q_B), lambda i: (i, 0)),
                  pl.BlockSpec((V_A, D_MODEL), lambda i: (0, 0)),
                  pl.BlockSpec((Q_B, D_MODEL), lambda i: (1, 0)),
                  pl.BlockSpec((1, D_MODEL), lambda i: (0, 0)),
                  pl.BlockSpec((D_MODEL, MEM_W), lambda i: (0, 0))],
        out_specs=(pl.BlockSpec((tm, D_MODEL), lambda i: (i, 0)),
                   pl.BlockSpec((tm, MEM_W), lambda i: (i, 0))),
        compiler_params=_cparams(("arbitrary",)),
        name="outproj",
    )(x, o_a, o_b, w_out, w_out, norm_cross, w_mq)


def _cross_router_kernel(x1_ref, q_ref, mk_ref, mv_ref, wo_ref, g_ref, wr_ref, br_ref,
                         x2_ref, hp_ref, route_ref):
    nb, tm = x1_ref.shape[0], x1_ref.shape[1]
    rows = []
    for bi in range(nb):
        q = q_ref[bi]
        outs = []
        for h in range(MEM_HEADS):
            sl = slice(h * MEM_DH, (h + 1) * MEM_DH)
            mk = mk_ref[bi, pl.ds(h, MEM_LEN, stride=MEM_HEADS), :].astype(BF16)
            mv = mv_ref[bi, pl.ds(h, MEM_LEN, stride=MEM_HEADS), :].astype(BF16)
            s = lax.dot_general(q[:, sl].astype(BF16), mk, (((1,), (1,)), ((), ())),
                                preferred_element_type=F32) * MEM_DH ** -0.5
            m = jnp.max(s, axis=-1, keepdims=True)
            p = jnp.exp(s - m)
            p = (p / jnp.sum(p, axis=-1, keepdims=True)).astype(BF16)
            outs.append(jnp.dot(p, mv, preferred_element_type=F32))
        rows.append(jnp.concatenate(outs, axis=1))
    att = (rows[0] if nb == 1 else jnp.concatenate(rows, axis=0)).astype(BF16)
    x2 = x1_ref[...].reshape(nb * tm, D_MODEL) + jnp.dot(att, wo_ref[...], preferred_element_type=F32)
    x2_ref[...] = x2.reshape(nb, tm, D_MODEL)
    hf = _rms(x2, g_ref[...])
    packed = _pack_halves(hf[:, 0:HALF], hf[:, HALF:D_MODEL])
    for bi in range(nb):
        for j in range(XROW_TILES):
            hp_ref[bi, pl.ds(j, tm, stride=XROW_TILES), :] = packed[bi * tm:(bi + 1) * tm, j * LANES:(j + 1) * LANES]
    hf_hi = hf.astype(BF16)
    hf_lo = (hf - hf_hi.astype(F32)).astype(BF16)
    wr = wr_ref[...]
    r_hi = jnp.dot(hf_hi, wr, preferred_element_type=F32)
    r_lo = jnp.dot(hf_lo, wr[:, 0:LANES], preferred_element_type=F32)
    logits = r_hi[:, 0:LANES] + r_hi[:, LANES:2 * LANES] + r_lo + br_ref[...]
    lane = lax.broadcasted_iota(jnp.int32, logits.shape, 1)
    lanef = lane.astype(F32)
    l = jnp.where(lane < N_EXPERTS, logits, -jnp.inf)
    vals, idxs = [], []
    for _ in range(TOP_K):
        m = jnp.max(l, axis=-1, keepdims=True)
        idx = jnp.min(jnp.where(l == m, lanef, float(LANES)), axis=-1, keepdims=True)
        vals.append(m)
        idxs.append(idx)
        l = jnp.where(lanef == idx, -jnp.inf, l)
    es = [jnp.exp(v - vals[0]) for v in vals]
    den = es[0] + es[1] + es[2] + es[3]
    route = jnp.zeros(logits.shape, F32)
    for k in range(TOP_K):
        route = jnp.where(lane == k, idxs[k], route)
        route = jnp.where(lane == TOP_K + k, es[k] / den, route)
    route_ref[...] = route.reshape(nb, tm, LANES)


def _split_bf16(w):
    w_hi = w.astype(BF16)
    w_lo = (w - w_hi.astype(F32)).astype(BF16)
    return jnp.concatenate([w_hi, w_lo], axis=1)


def _cross_router(x1, qm, mk, mv, w_mo, norm_ffn, w_router, b_router, nb, tm):
    b, t, _ = x1.shape
    nt = t // tm
    return pl.pallas_call(
        _cross_router_kernel,
        out_shape=(jax.ShapeDtypeStruct((b, t, D_MODEL), F32),
                   jax.ShapeDtypeStruct((b, t * XROW_TILES, LANES), jnp.uint32),
                   jax.ShapeDtypeStruct((b, t, LANES), F32)),
        grid=(b // nb, nt),
        in_specs=[pl.BlockSpec((nb, tm, D_MODEL), lambda i, j: (i, j, 0)),
                  pl.BlockSpec((nb, tm, MEM_W), lambda i, j: (i, j, 0)),
                  pl.BlockSpec((nb, MEM_LEN * MEM_HEADS, MEM_DH), lambda i, j: (i, 0, 0)),
                  pl.BlockSpec((nb, MEM_LEN * MEM_HEADS, MEM_DH), lambda i, j: (i, 0, 0)),
                  pl.BlockSpec((MEM_W, D_MODEL), lambda i, j: (0, 0)),
                  pl.BlockSpec((1, D_MODEL), lambda i, j: (0, 0)),
                  pl.BlockSpec((D_MODEL, 2 * LANES), lambda i, j: (0, 0)),
                  pl.BlockSpec((1, LANES), lambda i, j: (0, 0))],
        out_specs=(pl.BlockSpec((nb, tm, D_MODEL), lambda i, j: (i, j, 0)),
                   pl.BlockSpec((nb, tm * XROW_TILES, LANES), lambda i, j: (i, j, 0)),
                   pl.BlockSpec((nb, tm, LANES), lambda i, j: (i, j, 0))),
        compiler_params=_cparams(("arbitrary", "arbitrary")),
        name="cross_router",
    )(x1, qm, mk, mv, w_mo, norm_ffn, w_router, b_router)


def _row_tile(idx, tiles):
    return pl.ds(pl.multiple_of(idx * tiles, tiles), tiles)


def _dispatch_kernel(*refs, tokens, first_group):
    if first_group:
        pos_ref, tail_ref, hp_ref, xs_hbm, zero_ref, sem = refs
    else:
        pos_ref, tail_ref, hp_ref, _, xs_hbm, zero_ref, sem = refs

    def zero_tails():
        zero_ref[...] = jnp.zeros(zero_ref.shape, zero_ref.dtype)

        def tail_copy(e):
            dst = xs_hbm.at[pl.ds(pl.multiple_of(tail_ref[e] * XROW_TILES, XROW_TILES), MOE_STAGE * XROW_TILES)]
            return pltpu.make_async_copy(zero_ref, dst, sem.at[0])

        for e in range(N_EXPERTS):
            pl.when(tail_ref[e] >= 0)(lambda e=e: tail_copy(e).start())
        for e in range(N_EXPERTS):
            pl.when(tail_ref[e] >= 0)(lambda e=e: tail_copy(e).wait())

    if first_group:
        pl.when(pl.program_id(0) == 0)(zero_tails)

    def issue(t, carry):
        src = hp_ref.at[_row_tile(t, XROW_TILES)]
        for k in range(TOP_K):
            dst = xs_hbm.at[_row_tile(pos_ref[t * TOP_K + k], XROW_TILES)]
            pltpu.make_async_copy(src, dst, sem.at[1]).start(priority=k % 2)
        return carry

    lax.fori_loop(0, tokens, issue, 0, unroll=DMA_ISSUE_UNROLL)

    for k in range(TOP_K):
        pltpu.make_async_copy(hp_ref, xs_hbm.at[pl.ds(0, tokens * XROW_TILES)], sem.at[1]).wait()


def _dispatch(hp, pos_flat, tail_start, rows, tokens, xs_prev=None):
    t = hp.shape[0] // XROW_TILES
    first_group = xs_prev is None
    kern = functools.partial(_dispatch_kernel, tokens=tokens, first_group=first_group)
    in_specs = [pl.BlockSpec((tokens * TOP_K,), lambda i: (i,), memory_space=pltpu.SMEM),
                pl.BlockSpec((N_EXPERTS,), lambda i: (0,), memory_space=pltpu.SMEM),
                pl.BlockSpec((tokens * XROW_TILES, LANES), lambda i: (i, 0))]
    args = [pos_flat, tail_start, hp]
    if not first_group:
        in_specs.append(pl.BlockSpec(memory_space=pl.ANY))
        args.append(xs_prev)
    return pl.pallas_call(
        kern,
        out_shape=jax.ShapeDtypeStruct((rows * XROW_TILES, LANES), jnp.uint32),
        grid=(t // tokens,),
        in_specs=in_specs,
        out_specs=pl.BlockSpec(memory_space=pl.ANY),
        scratch_shapes=[pltpu.VMEM((MOE_STAGE * XROW_TILES, LANES), jnp.uint32),
                        pltpu.SemaphoreType.DMA((2,))],
        input_output_aliases={} if first_group else {3: 0},
        compiler_params=pltpu.CompilerParams(dimension_semantics=("arbitrary",),
                                             vmem_limit_bytes=VMEM_LIMIT_BYTES, has_side_effects=True),
        name="moe_dispatch",
    )(*args)


def _moe_kernel(ie_ref, is_ref, ns_ref, xs_hbm, wg_ref, bg_ref, wu_ref, bu_ref, wd_ref, bd_ref,
                ys_hbm, xu_ref, acc_ref, ystage_ref, in_sem, out_sem):
    w = pl.program_id(0)
    f = pl.program_id(1)
    nstage = ns_ref[w]
    nsub = nstage // (MOE_SUB // MOE_STAGE)
    start = is_ref[w]

    def stage_rows(s):
        return pl.ds(pl.multiple_of(s * MOE_STAGE, MOE_STAGE), MOE_STAGE)

    def hbm_rows(first, s, tiles):
        return pl.ds(pl.multiple_of((first + s * MOE_STAGE) * tiles, MOE_STAGE * tiles), MOE_STAGE * tiles)

    def load(first, s):
        dst = xu_ref.at[pl.ds(pl.multiple_of(s * MOE_STAGE * XROW_TILES, MOE_STAGE * XROW_TILES),
                              MOE_STAGE * XROW_TILES)]
        return pltpu.make_async_copy(xs_hbm.at[hbm_rows(first, s, XROW_TILES)], dst, in_sem.at[0])

    def fetch_item(first, stages):
        def issue(s, carry):
            load(first, s).start()
            return carry

        lax.fori_loop(0, stages, issue, 0)

    pl.when(jnp.logical_and(jnp.logical_and(w == 0, f == 0), nstage > 0))(lambda: fetch_item(start, nstage))

    @pl.when(jnp.logical_and(f == 0, nstage > 0))
    def _():
        def landed(s, carry):
            load(start, s).wait()
            return carry

        lax.fori_loop(0, nstage, landed, 0)

        def init(s, carry):
            acc_ref[stage_rows(s), :] = jnp.broadcast_to(bd_ref[...], (MOE_STAGE, D_MODEL))
            return carry

        lax.fori_loop(0, nstage, init, 0)

    def x_rows(row0, nrows):
        base = pl.multiple_of(row0 * XROW_TILES, MOE_STAGE * XROW_TILES)
        lows, highs = [], []
        for j in range(XROW_TILES):
            lo, hi = _unpack_halves(xu_ref[pl.ds(base + j, nrows, stride=XROW_TILES), :])
            lows.append(lo.astype(BF16))
            highs.append(hi.astype(BF16))
        return jnp.concatenate(lows + highs, axis=1)

    @pl.when(nstage > 0)
    def _():
        bg = bg_ref[...]
        bu = bu_ref[...]

        def expert_rows(row0, nrows):
            x = x_rows(row0, nrows)
            gt = jnp.dot(x, wg_ref[...].astype(BF16), preferred_element_type=F32) + bg
            up = jnp.dot(x, wu_ref[...].astype(BF16), preferred_element_type=F32) + bu
            gt = jnp.minimum(gt, SWIGLU_LIMIT)
            up = jnp.clip(up, -SWIGLU_LIMIT, SWIGLU_LIMIT)
            a = (gt * jax.nn.sigmoid(SWIGLU_ALPHA * gt) * (up + 1.0)).astype(BF16)
            rows = pl.ds(pl.multiple_of(row0, MOE_STAGE), nrows)
            acc_ref[rows, :] += jnp.dot(a, wd_ref[...].astype(BF16), preferred_element_type=F32)

        def body(s, carry):
            expert_rows(s * MOE_SUB, MOE_SUB)
            return carry

        lax.fori_loop(0, nsub, body, 0)
        done = nsub * MOE_SUB
        left = nstage - nsub * (MOE_SUB // MOE_STAGE)
        size = MOE_SUB // 2
        while size >= MOE_STAGE:
            here = left * MOE_STAGE >= size
            pl.when(here)(lambda done=done, size=size: expert_rows(done, size))
            done = done + jnp.where(here, size, 0)
            left = left - jnp.where(here, size // MOE_STAGE, 0)
            size //= 2

    @pl.when(jnp.logical_and(f == MOE_NF - 1, nstage > 0))
    def _():
        nxt = jnp.minimum(w + 1, pl.num_programs(0) - 1)
        nxt_stages = jnp.where(w + 1 < pl.num_programs(0), ns_ref[nxt], 0)
        fetch_item(is_ref[nxt], nxt_stages)

        def store(s):
            slot = s % MOE_OUT_SLOTS
            return pltpu.make_async_copy(ystage_ref.at[slot], ys_hbm.at[hbm_rows(start, s, YROW_TILES)],
                                         out_sem.at[slot])

        def write(s, carry):
            pl.when(s >= MOE_OUT_SLOTS)(lambda: store(s - MOE_OUT_SLOTS).wait())
            for j in range(YROW_TILES):
                lo = acc_ref[stage_rows(s), j * LANES:(j + 1) * LANES]
                hi = acc_ref[stage_rows(s), HALF + j * LANES:HALF + (j + 1) * LANES]
                ystage_ref[s % MOE_OUT_SLOTS, pl.ds(j, MOE_STAGE, stride=YROW_TILES), :] = _pack_halves(lo, hi)
            store(s).start()
            return carry

        lax.fori_loop(0, nstage, write, 0)
        for back in range(1, MOE_OUT_SLOTS + 1):
            pl.when(nstage >= back)(lambda back=back: store(nstage - back).wait())


def _moe(xs, item_expert, item_start, item_nsub, w_gate, b_gate, w_up, b_up, w_down, b_down):
    rows = xs.shape[0] // XROW_TILES
    nitems = item_expert.shape[0]

    def fcol(w, f, ie, st, ns):
        return jnp.where(ns[w] > 0, f, MOE_NF - 1)

    grid_spec = pltpu.PrefetchScalarGridSpec(
        num_scalar_prefetch=3,
        grid=(nitems, MOE_NF),
        in_specs=[pl.BlockSpec(memory_space=pl.ANY),
                  pl.BlockSpec((None, D_MODEL, MOE_TF), lambda w, f, ie, st, ns: (ie[w], 0, fcol(w, f, ie, st, ns))),
                  pl.BlockSpec((None, 1, MOE_TF), lambda w, f, ie, st, ns: (ie[w], 0, fcol(w, f, ie, st, ns))),
                  pl.BlockSpec((None, D_MODEL, MOE_TF), lambda w, f, ie, st, ns: (ie[w], 0, fcol(w, f, ie, st, ns))),
                  pl.BlockSpec((None, 1, MOE_TF), lambda w, f, ie, st, ns: (ie[w], 0, fcol(w, f, ie, st, ns))),
                  pl.BlockSpec((None, MOE_TF, D_MODEL), lambda w, f, ie, st, ns: (ie[w], fcol(w, f, ie, st, ns), 0)),
                  pl.BlockSpec((None, 1, D_MODEL), lambda w, f, ie, st, ns: (ie[w], 0, 0))],
        out_specs=pl.BlockSpec(memory_space=pl.ANY),
        scratch_shapes=[pltpu.VMEM((MOE_ROWS * XROW_TILES, LANES), jnp.uint32),
                        pltpu.VMEM((MOE_ROWS, D_MODEL), F32),
                        pltpu.VMEM((MOE_OUT_SLOTS, MOE_STAGE * YROW_TILES, LANES), jnp.uint32),
                        pltpu.SemaphoreType.DMA((1,)),
                        pltpu.SemaphoreType.DMA((MOE_OUT_SLOTS,))])
    return pl.pallas_call(
        _moe_kernel,
        out_shape=jax.ShapeDtypeStruct((rows * YROW_TILES, LANES), jnp.uint32),
        grid_spec=grid_spec,
        compiler_params=pltpu.CompilerParams(dimension_semantics=("arbitrary", "arbitrary"),
                                             vmem_limit_bytes=MOE_VMEM_LIMIT_BYTES, has_side_effects=True),
        name="moe_experts",
    )(item_expert, item_start, item_nsub, xs, w_gate, b_gate, w_up, b_up, w_down, b_down)


def _combine_kernel(pos_ref, nxt_ref, x2_ref, route_ref, g_ref, ys_hbm, o_ref, buf_ref, sem, *, tokens):
    i = pl.program_id(0)
    slot = i % 2

    def gather(p_ref, dst_slot):
        def issue(t, carry):
            for k in range(TOP_K):
                pltpu.make_async_copy(ys_hbm.at[_row_tile(p_ref[t * TOP_K + k], YROW_TILES)],
                                      buf_ref.at[dst_slot, k, _row_tile(t, YROW_TILES)],
                                      sem.at[dst_slot]).start(priority=k % 2)
            return carry

        lax.fori_loop(0, tokens, issue, 0, unroll=DMA_ISSUE_UNROLL)

    pl.when(i == 0)(lambda: gather(pos_ref, 0))
    pl.when(i + 1 < pl.num_programs(0))(lambda: gather(nxt_ref, 1 - slot))

    for k in range(TOP_K):
        pltpu.make_async_copy(ys_hbm.at[pl.ds(0, tokens * YROW_TILES)], buf_ref.at[slot, k], sem.at[slot]).wait()

    route = route_ref[...]
    gates = [route[:, TOP_K + k:TOP_K + k + 1] for k in range(TOP_K)]
    lows, highs = [], []
    for j in range(YROW_TILES):
        y_lo = x2_ref[:, j * LANES:(j + 1) * LANES]
        y_hi = x2_ref[:, HALF + j * LANES:HALF + (j + 1) * LANES]
        for k in range(TOP_K):
            lo, hi = _unpack_halves(buf_ref[slot, k, pl.ds(j, tokens, stride=YROW_TILES), :])
            y_lo = y_lo + gates[k] * lo
            y_hi = y_hi + gates[k] * hi
        lows.append(y_lo)
        highs.append(y_hi)
    o_ref[...] = _rms(jnp.concatenate(lows + highs, axis=1), g_ref[...])


def _combine(x2, route, final_norm, ys, pos_flat, tokens):
    t = x2.shape[0]
    nsteps = t // tokens
    kern = functools.partial(_combine_kernel, tokens=tokens)
    return pl.pallas_call(
        kern,
        out_shape=jax.ShapeDtypeStruct((t, D_MODEL), F32),
        grid=(nsteps,),
        in_specs=[pl.BlockSpec((tokens * TOP_K,), lambda i: (i,), memory_space=pltpu.SMEM),
                  pl.BlockSpec((tokens * TOP_K,), lambda i: (jnp.minimum(i + 1, nsteps - 1),),
                               memory_space=pltpu.SMEM),
                  pl.BlockSpec((tokens, D_MODEL), lambda i: (i, 0)),
                  pl.BlockSpec((tokens, LANES), lambda i: (i, 0)),
                  pl.BlockSpec((1, D_MODEL), lambda i: (0, 0)),
                  pl.BlockSpec(memory_space=pl.ANY)],
        out_specs=pl.BlockSpec((tokens, D_MODEL), lambda i: (i, 0)),
        scratch_shapes=[pltpu.VMEM((2, TOP_K, tokens * YROW_TILES, LANES), jnp.uint32),
                        pltpu.SemaphoreType.DMA((2,))],
        compiler_params=_cparams(("arbitrary",)),
        name="moe_combine",
    )(pos_flat, pos_flat, x2, route, final_norm, ys)


def _routing_tables(top_i, nitems):
    t = top_i.shape[0]
    sel = jnp.sum((top_i[:, :, None] == jnp.arange(N_EXPERTS, dtype=jnp.int32)[None, None, :]).astype(jnp.int32),
                  axis=1)
    cnt = jnp.sum(sel, axis=0)
    rank = jnp.cumsum(sel, axis=0) - sel
    cnt_pad = ((cnt + MOE_STAGE - 1) // MOE_STAGE) * MOE_STAGE
    off = jnp.cumsum(cnt_pad) - cnt_pad
    pos = jnp.take_along_axis(off[None, :] + rank, top_i, axis=1)
    tail_start = jnp.where(cnt > 0, off + cnt_pad - MOE_STAGE, -1)
    items_per = (cnt + MOE_ROWS - 1) // MOE_ROWS
    item_end = jnp.cumsum(items_per)
    total = item_end[-1]
    widx = jnp.arange(nitems, dtype=jnp.int32)
    e_of = jnp.minimum(jnp.searchsorted(item_end, widx, side="right"), N_EXPERTS - 1).astype(jnp.int32)
    j_of = widx - (item_end - items_per)[e_of]
    valid = widx < total
    e_last = e_of[jnp.maximum(total - 1, 0)]
    item_expert = jnp.where(valid, e_of, e_last).astype(jnp.int32)
    item_start = jnp.where(valid, off[e_of] + j_of * MOE_ROWS, 0).astype(jnp.int32)
    rows_left = cnt_pad[e_of] - j_of * MOE_ROWS
    item_nsub = jnp.where(valid, jnp.minimum(rows_left, MOE_ROWS) // MOE_STAGE, 0).astype(jnp.int32)
    return pos.reshape(t * TOP_K).astype(jnp.int32), tail_start.astype(jnp.int32), item_expert, item_start, item_nsub


def _trunk(x, conv_hist, s0, swa_fn, mk, mv, gdn_chunk, p):
    b, t, _ = x.shape
    m = b * t
    xf = x.reshape(m, D_MODEL)
    proj = _norm_matmul(xf, p["norm_mix"], p["w_in"], min(m, 1024), PROJ_TN)
    proj3 = proj.reshape(b, t, PROJ_COLS)
    hist8 = jnp.pad(conv_hist, ((0, 0), (SUBLANES - (CONV_W - 1), 0), (0, 0)))
    o_a, conv8, s_new = _gdn(proj3, hist8, s0, p["conv_w"], p["gate_par"], p["gdn_norm"], gdn_chunk,
                             GDN_SEQS_PER_STEP)
    o_b = swa_fn(proj3)
    x1, qm = _outproj(xf, o_a.reshape(m, V_A), o_b.reshape(m, Q_B), p["w_out"], p["norm_cross"], p["w_mq"], OUTPROJ_ROWS)
    tm = min(t, CROSS_ROWS)
    x2, hp, route = _cross_router(x1.reshape(b, t, D_MODEL), qm.reshape(b, t, MEM_W), mk, mv,
                                  p["w_mo"], p["norm_ffn"], p["w_router"], p["b_router"], CROSS_ROWS // tm, tm)
    return proj3, conv8[:, SUBLANES - (CONV_W - 1):], s_new, x2.reshape(m, D_MODEL), \
        hp.reshape(m * XROW_TILES, LANES), route.reshape(m, LANES)


def kernel(x_prompt, x_sample, cache_conv, state_gdn, cache_swa_k, cache_swa_v, cache_mem_k, cache_mem_v, mem_prompt, norm_mix, w_in, conv_w, gdn_a_log, gdn_dt_bias, gdn_norm, swa_sinks, rel_bias_table, w_out, norm_cross, norm_mem, w_mq, w_mk, w_mv, w_mo, norm_ffn, w_router, b_router, w_gate, b_gate, w_up, b_up, w_down, b_down, final_norm):
    depth = norm_mix.shape[0]
    assert depth == 1, "kernel is written for the single-layer trunk"
    bp, sp, _ = x_prompt.shape
    bs, ss, _ = x_sample.shape
    l = 0
    w = w_in[l]
    n_ab = 2 * GDN_HEADS
    w_perm = jnp.concatenate(
        [w[:, :CONV_CH + V_A].astype(BF16), w[:, CONV_CH + V_A + n_ab:].astype(BF16),
         w[:, CONV_CH + V_A:CONV_CH + V_A + n_ab].astype(BF16),
         jnp.zeros((D_MODEL, PROJ_COLS - w.shape[1]), BF16)], axis=1)
    gate_par = jnp.zeros((SUBLANES, LANES), F32)
    gate_par = gate_par.at[0, :GDN_HEADS].set(gdn_a_log[l]).at[1, :GDN_HEADS].set(gdn_dt_bias[l])
    p = dict(
        norm_mix=norm_mix[l].reshape(1, D_MODEL), w_in=w_perm, conv_w=conv_w[l], gate_par=gate_par,
        gdn_norm=gdn_norm[l].reshape(1, GDN_D), w_out=w_out[l].astype(BF16),
        norm_cross=norm_cross[l].reshape(1, D_MODEL), w_mq=w_mq[l].astype(BF16), w_mo=w_mo[l].astype(BF16),
        norm_ffn=norm_ffn[l].reshape(1, D_MODEL),
        w_router=_split_bf16(jnp.pad(w_router[l], ((0, 0), (0, LANES - N_EXPERTS)))),
        b_router=jnp.pad(b_router[l], (0, LANES - N_EXPERTS)).reshape(1, LANES))

    w_mkv = jnp.concatenate([w_mk[l], w_mv[l]], axis=1).astype(BF16)
    mkv = _norm_matmul(mem_prompt.reshape(bp * MEM_LEN, D_MODEL), norm_mem[l].reshape(1, D_MODEL), w_mkv,
                       min(bp * MEM_LEN, 1024), MEMKV_TN)
    mk_p = mkv[:, :MEM_W].reshape(bp, MEM_LEN * MEM_HEADS, MEM_DH)
    mv_p = mkv[:, MEM_W:].reshape(bp, MEM_LEN * MEM_HEADS, MEM_DH)

    zero_hist = jnp.zeros((bp, CONV_W - 1, CONV_CH), F32)
    zero_state = jnp.zeros((bp, GDN_HEADS, GDN_D, GDN_D), F32)
    swa_p = functools.partial(_swa_prompt, table=rel_bias_table, sinks=swa_sinks[l])
    proj_p, conv_p, st_p, x2_p, hp_p, route_p = _trunk(x_prompt, zero_hist, zero_state, swa_p, mk_p, mv_p, CHUNK, p)

    k_past = cache_swa_k[l].reshape(bs, WIN_PAST, KV_B)
    v_past = cache_swa_v[l].reshape(bs, WIN_PAST, KV_B)
    swa_s = functools.partial(_swa_sample, k_past=k_past, v_past=v_past, table=rel_bias_table, sinks=swa_sinks[l])
    proj_s, conv_s, st_s, x2_s, hp_s, route_s = _trunk(
        x_sample, cache_conv[l], state_gdn[l], swa_s,
        cache_mem_k[l].reshape(bs, MEM_LEN * MEM_HEADS, MEM_DH),
        cache_mem_v[l].reshape(bs, MEM_LEN * MEM_HEADS, MEM_DH), ss, p)

    mp, ms = bp * sp, bs * ss
    ntok = mp + ms
    top_i = jnp.concatenate([route_p[:, :TOP_K], route_s[:, :TOP_K]], axis=0).astype(jnp.int32)
    rows = ntok * TOP_K + N_EXPERTS * MOE_STAGE
    nitems = (ntok * TOP_K) // MOE_ROWS + N_EXPERTS
    pos_flat, tail_start, item_expert, item_start, item_nsub = _routing_tables(top_i, nitems)
    pos_p, pos_s = pos_flat[:mp * TOP_K], pos_flat[mp * TOP_K:]
    xs = _dispatch(hp_p, pos_p, tail_start, rows, 512)
    xs = _dispatch(hp_s, pos_s, tail_start, rows, 512, xs_prev=xs)
    ys = _moe(xs, item_expert, item_start, item_nsub,
              w_gate[l], b_gate[l].reshape(N_EXPERTS, 1, D_FF), w_up[l], b_up[l].reshape(N_EXPERTS, 1, D_FF),
              w_down[l], b_down[l].reshape(N_EXPERTS, 1, D_MODEL))
    fnorm = final_norm.reshape(1, D_MODEL)
    y_p = _combine(x2_p, route_p, fnorm, ys, pos_p, 256).reshape(bp, sp, D_MODEL)
    y_s = _combine(x2_s, route_s, fnorm, ys, pos_s, 256).reshape(bs, ss, D_MODEL)

    def kv_window(proj3, col, past=None):
        new = proj3[:, :, col * LANES:(col + 1) * LANES]
        full = new if past is None else jnp.concatenate([past, new], axis=1)
        win = full[:, -WIN_PAST:]
        return win.reshape(win.shape[0], WIN_PAST, SWA_KV_HEADS, SWA_DH)[None]

    return (y_p, y_s,
            conv_p[None], st_p[None], kv_window(proj_p, COL_KB), kv_window(proj_p, COL_VB),
            mk_p.reshape(bp, MEM_LEN, MEM_HEADS, MEM_DH)[None], mv_p.reshape(bp, MEM_LEN, MEM_HEADS, MEM_DH)[None],
            conv_s[None], st_s[None], kv_window(proj_s, COL_KB, k_past), kv_window(proj_s, COL_VB, v_past))
```

```python
import functools
import math

import numpy as np
import jax
import jax.numpy as jnp
from jax import lax
from jax.experimental import pallas as pl
from jax.experimental.pallas import tpu as pltpu

F32 = jnp.float32
BF16 = jnp.bfloat16
HIGHEST = lax.Precision.HIGHEST

D_MODEL = 2048
CHUNK = 64
GDN_HEADS = 8
GDN_D = 128
CONV_W = 4
SWA_HEADS = 16
SWA_KV_HEADS = 2
SWA_GROUP = SWA_HEADS // SWA_KV_HEADS
SWA_DH = 64
WIN_PAST = 128
REL_BUCKETS = 32
REL_MAX_DIST = 128
MEM_LEN = 256
MEM_HEADS = 4
MEM_DH = 128
N_EXPERTS = 32
TOP_K = 4
D_FF = D_MODEL
SWIGLU_LIMIT = 7.0
SWIGLU_ALPHA = 1.702
EPS = 1e-6

QK_A = GDN_HEADS * GDN_D
V_A = GDN_HEADS * GDN_D
CONV_CH = 2 * QK_A + V_A
Q_B = SWA_HEADS * SWA_DH
KV_B = SWA_KV_HEADS * SWA_DH
MEM_W = MEM_HEADS * MEM_DH

LANES = 128
SUBLANES = 8
VMEM_LIMIT_BYTES = 58 * 1024 * 1024
MOE_VMEM_LIMIT_BYTES = 60 * 1024 * 1024

PROJ_COLS = 5632
PROJ_TN = 1408
MEMKV_TN = 512
COL_Z = CONV_CH // V_A
COL_QB = (CONV_CH + V_A) // Q_B
COL_KB = (CONV_CH + V_A + Q_B) // LANES
COL_VB = COL_KB + 1
COL_AB = COL_KB + 2

MOE_SUB = 512
MOE_ROWS = 1536
MOE_STAGE = 128
MOE_OUT_SLOTS = 4
MOE_TF = 512
MOE_NF = D_FF // MOE_TF
HALF = D_MODEL // 2
NORM_ROWS = 256
SWA_CHUNKS_PER_STEP = 4
SWA_SEQS_PER_STEP = 4
CROSS_ROWS = 256
DMA_ISSUE_UNROLL = 8
OUTPROJ_ROWS = 512
GDN_SEQS_PER_STEP = 4
XROW_TILES = HALF // LANES
YROW_TILES = HALF // LANES


def _cparams(sem):
    return pltpu.CompilerParams(dimension_semantics=sem, vmem_limit_bytes=VMEM_LIMIT_BYTES)


def _rms(x, gain):
    return x * lax.rsqrt(jnp.mean(x * x, axis=-1, keepdims=True) + EPS) * gain


def _pack_halves(lo, hi):
    lo_bits = pltpu.bitcast(lo.astype(BF16).astype(F32), jnp.uint32)
    hi_bits = pltpu.bitcast(hi.astype(BF16).astype(F32), jnp.uint32)
    return (lo_bits >> 16) | (hi_bits & jnp.uint32(0xFFFF0000))


def _unpack_halves(u):
    return pltpu.bitcast(u << 16, F32), pltpu.bitcast(u & jnp.uint32(0xFFFF0000), F32)


def _norm_matmul_kernel(x_ref, g_ref, w_ref, o_ref, h_ref):
    @pl.when(pl.program_id(1) == 0)
    def _():
        def body(r, carry):
            rows = pl.ds(pl.multiple_of(r * NORM_ROWS, NORM_ROWS), NORM_ROWS)
            h_ref[rows, :] = _rms(x_ref[rows, :], g_ref[...]).astype(BF16)
            return carry

        lax.fori_loop(0, x_ref.shape[0] // NORM_ROWS, body, 0)

    o_ref[...] = jnp.dot(h_ref[...], w_ref[...], preferred_element_type=F32)


def _norm_matmul(x, gain, w, tm, tn):
    m, k = x.shape
    n = w.shape[1]
    return pl.pallas_call(
        _norm_matmul_kernel,
        out_shape=jax.ShapeDtypeStruct((m, n), F32),
        grid=(m // tm, n // tn),
        in_specs=[pl.BlockSpec((tm, k), lambda i, j: (i, 0)),
                  pl.BlockSpec((1, k), lambda i, j: (0, 0)),
                  pl.BlockSpec((k, tn), lambda i, j: (0, j))],
        out_specs=pl.BlockSpec((tm, tn), lambda i, j: (i, j)),
        scratch_shapes=[pltpu.VMEM((tm, k), BF16)],
        compiler_params=_cparams(("arbitrary", "arbitrary")),
        name="norm_matmul",
    )(x, gain, w)


def _tri_inverse(lows, c):
    row = lax.broadcasted_iota(jnp.int32, (c, c), 0)
    col = lax.broadcasted_iota(jnp.int32, (c, c), 1)
    eye = jnp.where(row == col, 1.0, 0.0).astype(F32)
    ps = [eye - low for low in lows]
    ms = list(lows)
    span = 1
    while 2 * span < c:
        mbs = [m.astype(BF16) for m in ms]
        ms = [jnp.dot(mb, mb, preferred_element_type=F32) for mb in mbs]
        ps = [p + jnp.dot(p.astype(BF16), m.astype(BF16), preferred_element_type=F32) for p, m in zip(ps, ms)]
        span *= 2
    return ps


def _gdn_kernel(qkv_ref, z_ref, ab_ref, hist_ref, s0_ref, convw_ref, gp_ref, norm_ref,
                o_ref, convnew_ref, s_ref, ext_ref, *, c):
    step = pl.program_id(1)
    nb = qkv_ref.shape[0]

    @pl.when(step == 0)
    def _():
        ext_ref[:, 0:SUBLANES, :] = hist_ref[...]
        s_ref[...] = s0_ref[...]

    row = lax.broadcasted_iota(jnp.int32, (c, c), 0)
    col = lax.broadcasted_iota(jnp.int32, (c, c), 1)
    incl = row >= col
    strict = row > col
    tril = jnp.where(incl, 1.0, 0.0).astype(F32)
    gp = gp_ref[...]
    scale = GDN_D ** -0.5
    heads = [(bi, h) for bi in range(nb) for h in range(GDN_HEADS)]
    s_old = [s_ref[bi, h] for bi, h in heads]

    acts, gc_alls, gc_ts, beta_alls = [], [], [], []
    for bi in range(nb):
        ext_ref[bi, SUBLANES:SUBLANES + c, :] = qkv_ref[bi]
        base = SUBLANES - (CONV_W - 1)
        conv = ext_ref[bi, base:base + c, :] * convw_ref[0:1, :]
        for j in range(1, CONV_W):
            conv = conv + ext_ref[bi, base + j:base + j + c, :] * convw_ref[j:j + 1, :]
        tail = ext_ref[bi, c:c + SUBLANES, :]
        convnew_ref[bi] = tail
        ext_ref[bi, 0:SUBLANES, :] = tail
        acts.append(conv * jax.nn.sigmoid(conv))
        ab = ab_ref[bi]
        g = -jnp.exp(gp[0:1, :]) * jax.nn.softplus(ab + gp[1:2, :])
        beta_alls.append(jax.nn.sigmoid(ab))
        gc_all = jnp.dot(tril, g, precision=HIGHEST, preferred_element_type=F32)
        gc_pad = jnp.concatenate([gc_all, jnp.zeros((LANES - c, LANES), F32)], axis=0) if c < LANES else gc_all
        gc_alls.append(gc_all)
        gc_ts.append(gc_pad.T)

    q, k, v = [], [], []
    for bi, h in heads:
        lo = h * GDN_D
        qh = acts[bi][:, lo:lo + GDN_D]
        kh = acts[bi][:, QK_A + lo:QK_A + lo + GDN_D]
        q.append(qh * lax.rsqrt(jnp.sum(qh * qh, axis=-1, keepdims=True) + EPS) * scale)
        k.append(kh * lax.rsqrt(jnp.sum(kh * kh, axis=-1, keepdims=True) + EPS))
        v.append(acts[bi][:, 2 * QK_A + lo:2 * QK_A + lo + GDN_D])
    beta = [beta_alls[bi][:, GDN_HEADS + h:GDN_HEADS + h + 1] for bi, h in heads]
    gc = [gc_alls[bi][:, h:h + 1] for bi, h in heads]
    gc_last = [gc_alls[bi][c - 1:c, h:h + 1] for bi, h in heads]
    heads = range(len(heads))
    decay = [jnp.exp(jnp.where(incl, gc[h] - gc_ts[h // GDN_HEADS][h % GDN_HEADS:h % GDN_HEADS + 1, 0:c],
                               -jnp.inf)) for h in heads]
    egc = [jnp.exp(gc[h]) for h in heads]
    kb = [k[h] * beta[h] for h in heads]
    qk = [lax.dot_general(jnp.concatenate([q[h], kb[h]], axis=0).astype(BF16), k[h].astype(BF16),
                          (((1,), (1,)), ((), ())), preferred_element_type=F32) for h in heads]
    a_qk = [jnp.where(incl, qk[h][0:c] * decay[h], 0.0) for h in heads]
    tinv = _tri_inverse([jnp.where(strict, qk[h][c:2 * c] * decay[h], 0.0) for h in heads], c)
    uw = [jnp.dot(tinv[h].astype(BF16),
                  jnp.concatenate([v[h] * beta[h], kb[h] * egc[h]], axis=1).astype(BF16),
                  preferred_element_type=F32) for h in heads]
    ws_qs = [jnp.dot(jnp.concatenate([uw[h][:, GDN_D:2 * GDN_D], q[h] * egc[h]], axis=0).astype(BF16),
                     s_old[h].astype(BF16), preferred_element_type=F32) for h in heads]
    v_new = [(uw[h][:, 0:GDN_D] - ws_qs[h][0:c]).astype(BF16) for h in heads]
    o = [ws_qs[h][c:2 * c] + jnp.dot(a_qk[h].astype(BF16), v_new[h], preferred_element_type=F32)
         for h in heads]
    s_new = [s_old[h] * jnp.exp(gc_last[h]) + lax.dot_general(
        (k[h] * jnp.exp(gc_last[h] - gc[h])).astype(BF16), v_new[h], (((0,), (0,)), ((), ())),
        preferred_element_type=F32) for h in heads]
    for bi in range(nb):
        z_all = z_ref[bi]
        gate = z_all * jax.nn.sigmoid(z_all)
        o_ref[bi] = jnp.concatenate([_rms(o[bi * GDN_HEADS + h], norm_ref[...]) for h in range(GDN_HEADS)],
                                    axis=1) * gate
    for h in heads:
        s_ref[h // GDN_HEADS, h % GDN_HEADS] = s_new[h]


def _gdn(proj3, hist8, s0, conv_w, gate_par, gdn_norm, c, nb):
    b, t, _ = proj3.shape
    nchunks = t // c
    kern = functools.partial(_gdn_kernel, c=c)
    return pl.pallas_call(
        kern,
        out_shape=(jax.ShapeDtypeStruct((b, t, V_A), F32),
                   jax.ShapeDtypeStruct((b, SUBLANES, CONV_CH), F32),
                   jax.ShapeDtypeStruct((b, GDN_HEADS, GDN_D, GDN_D), F32)),
        grid=(b // nb, nchunks),
        in_specs=[pl.BlockSpec((nb, c, CONV_CH), lambda i, j: (i, j, 0)),
                  pl.BlockSpec((nb, c, V_A), lambda i, j: (i, j, COL_Z)),
                  pl.BlockSpec((nb, c, LANES), lambda i, j: (i, j, COL_AB)),
                  pl.BlockSpec((nb, SUBLANES, CONV_CH), lambda i, j: (i, 0, 0)),
                  pl.BlockSpec((nb, GDN_HEADS, GDN_D, GDN_D), lambda i, j: (i, 0, 0, 0)),
                  pl.BlockSpec((CONV_W, CONV_CH), lambda i, j: (0, 0)),
                  pl.BlockSpec((SUBLANES, LANES), lambda i, j: (0, 0)),
                  pl.BlockSpec((1, GDN_D), lambda i, j: (0, 0))],
        out_specs=(pl.BlockSpec((nb, c, V_A), lambda i, j: (i, j, 0)),
                   pl.BlockSpec((nb, SUBLANES, CONV_CH), lambda i, j: (i, 0, 0)),
                   pl.BlockSpec((nb, GDN_HEADS, GDN_D, GDN_D), lambda i, j: (i, 0, 0, 0))),
        scratch_shapes=[pltpu.VMEM((nb, SUBLANES + c, CONV_CH), F32)],
        compiler_params=_cparams(("arbitrary", "arbitrary")),
        name="gdn_mixer",
    )(proj3, proj3, proj3, hist8, s0, conv_w, gate_par, gdn_norm)


def _swa_kernel(*refs, nq, nsub, piece_rows, masked):
    npieces = len(piece_rows)
    q_ref = refs[0]
    k_refs = refs[1:1 + npieces]
    v_refs = refs[1 + npieces:1 + 2 * npieces]
    bias_ref, sink_ref, o_ref = refs[1 + 2 * npieces:]
    nk = WIN_PAST + nq
    nseq = q_ref.shape[0]
    kcat = [jnp.concatenate([r[b] for r in k_refs], axis=0).astype(BF16) for b in range(nseq)]
    vcat = [jnp.concatenate([r[b] for r in v_refs], axis=0).astype(BF16) for b in range(nseq)]
    q = [q_ref[b] for b in range(nseq)]
    kidx = lax.broadcasted_iota(jnp.int32, (SWA_GROUP * nq, nk), 1)
    blocks = [(b, i, kv) for b in range(nseq) for i in range(nsub) for kv in range(SWA_KV_HEADS)]
    scores = []
    for b, i, kv in blocks:
        qs = jnp.concatenate(
            [q[b][i * nq:(i + 1) * nq, (kv * SWA_GROUP + g) * SWA_DH:(kv * SWA_GROUP + g + 1) * SWA_DH]
             for g in range(SWA_GROUP)], axis=0).astype(BF16)
        kh = kcat[b][i * nq:i * nq + nk, kv * SWA_DH:(kv + 1) * SWA_DH]
        s = lax.dot_general(qs, kh, (((1,), (1,)), ((), ())), preferred_element_type=F32)
        s = s * SWA_DH ** -0.5 + bias_ref[kv]
        if masked:
            first = (pl.program_id(1) * nsub + i) * nq - WIN_PAST
            s = jnp.where(kidx + first >= 0, s, -jnp.inf)
        scores.append(s)
    sinks = [sink_ref[kv] for _, _, kv in blocks]
    maxes = [jnp.maximum(jnp.max(s, axis=-1, keepdims=True), sk) for s, sk in zip(scores, sinks)]
    probs = [jnp.exp(s - m) for s, m in zip(scores, maxes)]
    dens = [jnp.sum(p, axis=-1, keepdims=True) + jnp.exp(sk - m) for p, sk, m in zip(probs, sinks, maxes)]
    outs = [jnp.dot((p / den).astype(BF16), vcat[b][i * nq:i * nq + nk, kv * SWA_DH:(kv + 1) * SWA_DH],
                    preferred_element_type=F32) for (b, i, kv), p, den in zip(blocks, probs, dens)]
    for b in range(nseq):
        out_rows = []
        for i in range(nsub):
            heads = []
            for kv in range(SWA_KV_HEADS):
                o = outs[(b * nsub + i) * SWA_KV_HEADS + kv]
                heads += [o[g * nq:(g + 1) * nq, :] for g in range(SWA_GROUP)]
            out_rows.append(jnp.concatenate(heads, axis=1))
        o_ref[b] = out_rows[0] if nsub == 1 else jnp.concatenate(out_rows, axis=0)


def _rel_bucket(nq, nk):
    rel = jnp.arange(nk)[None, :] - WIN_PAST - jnp.arange(nq)[:, None]
    nb = REL_BUCKETS // 2
    max_exact = nb // 2
    n = jnp.abs(rel)
    large = max_exact + (jnp.log(jnp.maximum(n, 1).astype(F32) / max_exact)
                         / math.log(REL_MAX_DIST / max_exact) * (nb - max_exact)).astype(jnp.int32)
    large = jnp.minimum(large, nb - 1)
    return jnp.where(rel > 0, nb, 0) + jnp.where(n < max_exact, n, large)


def _swa_tables(table, sinks, nq, nk):
    onehot = (_rel_bucket(nq, nk)[:, :, None] == jnp.arange(REL_BUCKETS)[None, None, :]).astype(F32)
    bias = jnp.einsum("qkb,bh->qkh", onehot, table.astype(F32), precision=HIGHEST)
    bias = jnp.transpose(bias, (2, 0, 1)).astype(F32)
    bias = bias.reshape(SWA_KV_HEADS, SWA_GROUP * nq, nk)
    sk = jnp.broadcast_to(sinks.astype(F32).reshape(SWA_KV_HEADS, SWA_GROUP, 1, 1),
                          (SWA_KV_HEADS, SWA_GROUP, nq, 1)).reshape(SWA_KV_HEADS, SWA_GROUP * nq, 1)
    return bias, sk


def _swa_prompt(proj3, table, sinks):
    b, t, _ = proj3.shape
    nq = CHUNK
    nsub = SWA_CHUNKS_PER_STEP
    rows = nsub * nq
    assert rows % WIN_PAST == 0 and t % rows == 0
    past_per_step = rows // WIN_PAST
    bias, sk = _swa_tables(table, sinks, nq, WIN_PAST + nq)
    kern = functools.partial(_swa_kernel, nq=nq, nsub=nsub, piece_rows=(WIN_PAST, rows), masked=True)

    def past_spec(colblk):
        return pl.BlockSpec((1, WIN_PAST, LANES),
                            lambda i, j: (i, jnp.maximum(j * past_per_step - 1, 0), colblk))

    def cur_spec(colblk):
        return pl.BlockSpec((1, rows, LANES), lambda i, j: (i, j, colblk))

    in_specs = [pl.BlockSpec((1, rows, Q_B), lambda i, j: (i, j, COL_QB)),
                past_spec(COL_KB), cur_spec(COL_KB), past_spec(COL_VB), cur_spec(COL_VB),
                pl.BlockSpec(bias.shape, lambda i, j: (0, 0, 0)),
                pl.BlockSpec(sk.shape, lambda i, j: (0, 0, 0))]
    return pl.pallas_call(
        kern,
        out_shape=jax.ShapeDtypeStruct((b, t, Q_B), F32),
        grid=(b, t // rows),
        in_specs=in_specs,
        out_specs=pl.BlockSpec((1, rows, Q_B), lambda i, j: (i, j, 0)),
        compiler_params=_cparams(("arbitrary", "arbitrary")),
        name="swa_prompt",
    )(proj3, proj3, proj3, proj3, proj3, bias, sk)


def _swa_sample(proj3, k_past, v_past, table, sinks):
    b, t, _ = proj3.shape
    bias, sk = _swa_tables(table, sinks, t, WIN_PAST + t)
    kern = functools.partial(_swa_kernel, nq=t, nsub=1, piece_rows=(WIN_PAST, t), masked=False)
    ns = SWA_SEQS_PER_STEP
    in_specs = [pl.BlockSpec((ns, t, Q_B), lambda i, j: (i, 0, COL_QB)),
                pl.BlockSpec((ns, WIN_PAST, LANES), lambda i, j: (i, 0, 0)),
                pl.BlockSpec((ns, t, LANES), lambda i, j: (i, 0, COL_KB)),
                pl.BlockSpec((ns, WIN_PAST, LANES), lambda i, j: (i, 0, 0)),
                pl.BlockSpec((ns, t, LANES), lambda i, j: (i, 0, COL_VB)),
                pl.BlockSpec(bias.shape, lambda i, j: (0, 0, 0)),
                pl.BlockSpec(sk.shape, lambda i, j: (0, 0, 0))]
    return pl.pallas_call(
        kern,
        out_shape=jax.ShapeDtypeStruct((b, t, Q_B), F32),
        grid=(b // ns, 1),
        in_specs=in_specs,
        out_specs=pl.BlockSpec((ns, t, Q_B), lambda i, j: (i, 0, 0)),
        compiler_params=_cparams(("arbitrary", "arbitrary")),
        name="swa_sample",
    )(proj3, k_past, proj3, v_past, proj3, bias, sk)


def _outproj_kernel(x_ref, oa_ref, ob_ref, wa_ref, wb_ref, g_ref, wq_ref, x1_ref, q_ref):
    x1 = (x_ref[...]
          + jnp.dot(oa_ref[...].astype(BF16), wa_ref[...], preferred_element_type=F32)
          + jnp.dot(ob_ref[...].astype(BF16), wb_ref[...], preferred_element_type=F32))
    x1_ref[...] = x1
    hc = _rms(x1, g_ref[...]).astype(BF16)
    q_ref[...] = jnp.dot(hc, wq_ref[...], preferred_element_type=F32)


def _outproj(x, o_a, o_b, w_out, norm_cross, w_mq, tm):
    m = x.shape[0]
    return pl.pallas_call(
        _outproj_kernel,
        out_shape=(jax.ShapeDtypeStruct((m, D_MODEL), F32), jax.ShapeDtypeStruct((m, MEM_W), F32)),
        grid=(m // tm,),
        in_specs=[pl.BlockSpec((tm, D_MODEL), lambda i: (i, 0)),
                  pl.BlockSpec((tm, V_A), lambda i: (i, 0)),
                  pl.BlockSpec((tm, Q_B), lambda i: (i, 0)),
                  pl.BlockSpec((V_A, D_MODEL), lambda i: (0, 0)),
                  pl.BlockSpec((Q_B, D_MODEL), lambda i: (1, 0)),
                  pl.BlockSpec((1, D_MODEL), lambda i: (0, 0)),
                  pl.BlockSpec((D_MODEL, MEM_W), lambda i: (0, 0))],
        out_specs=(pl.BlockSpec((tm, D_MODEL), lambda i: (i, 0)),
                   pl.BlockSpec((tm, MEM_W), lambda i: (i, 0))),
        compiler_params=_cparams(("arbitrary",)),
        name="outproj",
    )(x, o_a, o_b, w_out, w_out, norm_cross, w_mq)


def _cross_router_kernel(x1_ref, q_ref, mk_ref, mv_ref, wo_ref, g_ref, wr_ref, br_ref,
                         x2_ref, hp_ref, route_ref):
    nb, tm = x1_ref.shape[0], x1_ref.shape[1]
    rows = []
    for bi in range(nb):
        q = q_ref[bi]
        outs = []
        for h in range(MEM_HEADS):
            sl = slice(h * MEM_DH, (h + 1) * MEM_DH)
            mk = mk_ref[bi, pl.ds(h, MEM_LEN, stride=MEM_HEADS), :].astype(BF16)
            mv = mv_ref[bi, pl.ds(h, MEM_LEN, stride=MEM_HEADS), :].astype(BF16)
            s = lax.dot_general(q[:, sl].astype(BF16), mk, (((1,), (1,)), ((), ())),
                                preferred_element_type=F32) * MEM_DH ** -0.5
            m = jnp.max(s, axis=-1, keepdims=True)
            p = jnp.exp(s - m)
            p = (p / jnp.sum(p, axis=-1, keepdims=True)).astype(BF16)
            outs.append(jnp.dot(p, mv, preferred_element_type=F32))
        rows.append(jnp.concatenate(outs, axis=1))
    att = (rows[0] if nb == 1 else jnp.concatenate(rows, axis=0)).astype(BF16)
    x2 = x1_ref[...].reshape(nb * tm, D_MODEL) + jnp.dot(att, wo_ref[...], preferred_element_type=F32)
    x2_ref[...] = x2.reshape(nb, tm, D_MODEL)
    hf = _rms(x2, g_ref[...])
    packed = _pack_halves(hf[:, 0:HALF], hf[:, HALF:D_MODEL])
    for bi in range(nb):
        for j in range(XROW_TILES):
            hp_ref[bi, pl.ds(j, tm, stride=XROW_TILES), :] = packed[bi * tm:(bi + 1) * tm, j * LANES:(j + 1) * LANES]
    hf_hi = hf.astype(BF16)
    hf_lo = (hf - hf_hi.astype(F32)).astype(BF16)
    wr = wr_ref[...]
    r_hi = jnp.dot(hf_hi, wr, preferred_element_type=F32)
    r_lo = jnp.dot(hf_lo, wr[:, 0:LANES], preferred_element_type=F32)
    logits = r_hi[:, 0:LANES] + r_hi[:, LANES:2 * LANES] + r_lo + br_ref[...]
    lane = lax.broadcasted_iota(jnp.int32, logits.shape, 1)
    lanef = lane.astype(F32)
    l = jnp.where(lane < N_EXPERTS, logits, -jnp.inf)
    vals, idxs = [], []
    for _ in range(TOP_K):
        m = jnp.max(l, axis=-1, keepdims=True)
        idx = jnp.min(jnp.where(l == m, lanef, float(LANES)), axis=-1, keepdims=True)
        vals.append(m)
        idxs.append(idx)
        l = jnp.where(lanef == idx, -jnp.inf, l)
    es = [jnp.exp(v - vals[0]) for v in vals]
    den = es[0] + es[1] + es[2] + es[3]
    route = jnp.zeros(logits.shape, F32)
    for k in range(TOP_K):
        route = jnp.where(lane == k, idxs[k], route)
        route = jnp.where(lane == TOP_K + k, es[k] / den, route)
    route_ref[...] = route.reshape(nb, tm, LANES)


def _split_bf16(w):
    w_hi = w.astype(BF16)
    w_lo = (w - w_hi.astype(F32)).astype(BF16)
    return jnp.concatenate([w_hi, w_lo], axis=1)


def _cross_router(x1, qm, mk, mv, w_mo, norm_ffn, w_router, b_router, nb, tm):
    b, t, _ = x1.shape
    nt = t // tm
    return pl.pallas_call(
        _cross_router_kernel,
        out_shape=(jax.ShapeDtypeStruct((b, t, D_MODEL), F32),
                   jax.ShapeDtypeStruct((b, t * XROW_TILES, LANES), jnp.uint32),
                   jax.ShapeDtypeStruct((b, t, LANES), F32)),
        grid=(b // nb, nt),
        in_specs=[pl.BlockSpec((nb, tm, D_MODEL), lambda i, j: (i, j, 0)),
                  pl.BlockSpec((nb, tm, MEM_W), lambda i, j: (i, j, 0)),
                  pl.BlockSpec((nb, MEM_LEN * MEM_HEADS, MEM_DH), lambda i, j: (i, 0, 0)),
                  pl.BlockSpec((nb, MEM_LEN * MEM_HEADS, MEM_DH), lambda i, j: (i, 0, 0)),
                  pl.BlockSpec((MEM_W, D_MODEL), lambda i, j: (0, 0)),
                  pl.BlockSpec((1, D_MODEL), lambda i, j: (0, 0)),
                  pl.BlockSpec((D_MODEL, 2 * LANES), lambda i, j: (0, 0)),
                  pl.BlockSpec((1, LANES), lambda i, j: (0, 0))],
        out_specs=(pl.BlockSpec((nb, tm, D_MODEL), lambda i, j: (i, j, 0)),
                   pl.BlockSpec((nb, tm * XROW_TILES, LANES), lambda i, j: (i, j, 0)),
                   pl.BlockSpec((nb, tm, LANES), lambda i, j: (i, j, 0))),
        compiler_params=_cparams(("arbitrary", "arbitrary")),
        name="cross_router",
    )(x1, qm, mk, mv, w_mo, norm_ffn, w_router, b_router)


def _row_tile(idx, tiles):
    return pl.ds(pl.multiple_of(idx * tiles, tiles), tiles)


def _dispatch_kernel(*refs, tokens, first_group):
    if first_group:
        pos_ref, tail_ref, hp_ref, xs_hbm, zero_ref, sem = refs
    else:
        pos_ref, tail_ref, hp_ref, _, xs_hbm, zero_ref, sem = refs

    def zero_tails():
        zero_ref[...] = jnp.zeros(zero_ref.shape, zero_ref.dtype)

        def tail_copy(e):
            dst = xs_hbm.at[pl.ds(pl.multiple_of(tail_ref[e] * XROW_TILES, XROW_TILES), MOE_STAGE * XROW_TILES)]
            return pltpu.make_async_copy(zero_ref, dst, sem.at[0])

        for e in range(N_EXPERTS):
            pl.when(tail_ref[e] >= 0)(lambda e=e: tail_copy(e).start())
        for e in range(N_EXPERTS):
            pl.when(tail_ref[e] >= 0)(lambda e=e: tail_copy(e).wait())

    if first_group:
        pl.when(pl.program_id(0) == 0)(zero_tails)

    def issue(t, carry):
        src = hp_ref.at[_row_tile(t, XROW_TILES)]
        for k in range(TOP_K):
            dst = xs_hbm.at[_row_tile(pos_ref[t * TOP_K + k], XROW_TILES)]
            pltpu.make_async_copy(src, dst, sem.at[1]).start(priority=k % 2)
        return carry

    lax.fori_loop(0, tokens, issue, 0, unroll=DMA_ISSUE_UNROLL)

    for k in range(TOP_K):
        pltpu.make_async_copy(hp_ref, xs_hbm.at[pl.ds(0, tokens * XROW_TILES)], sem.at[1]).wait()


def _dispatch(hp, pos_flat, tail_start, rows, tokens, xs_prev=None):
    t = hp.shape[0] // XROW_TILES
    first_group = xs_prev is None
    kern = functools.partial(_dispatch_kernel, tokens=tokens, first_group=first_group)
    in_specs = [pl.BlockSpec((tokens * TOP_K,), lambda i: (i,), memory_space=pltpu.SMEM),
                pl.BlockSpec((N_EXPERTS,), lambda i: (0,), memory_space=pltpu.SMEM),
                pl.BlockSpec((tokens * XROW_TILES, LANES), lambda i: (i, 0))]
    args = [pos_flat, tail_start, hp]
    if not first_group:
        in_specs.append(pl.BlockSpec(memory_space=pl.ANY))
        args.append(xs_prev)
    return pl.pallas_call(
        kern,
        out_shape=jax.ShapeDtypeStruct((rows * XROW_TILES, LANES), jnp.uint32),
        grid=(t // tokens,),
        in_specs=in_specs,
        out_specs=pl.BlockSpec(memory_space=pl.ANY),
        scratch_shapes=[pltpu.VMEM((MOE_STAGE * XROW_TILES, LANES), jnp.uint32),
                        pltpu.SemaphoreType.DMA((2,))],
        input_output_aliases={} if first_group else {3: 0},
        compiler_params=pltpu.CompilerParams(dimension_semantics=("arbitrary",),
                                             vmem_limit_bytes=VMEM_LIMIT_BYTES, has_side_effects=True),
        name="moe_dispatch",
    )(*args)


def _moe_kernel(ie_ref, is_ref, ns_ref, xs_hbm, wg_ref, bg_ref, wu_ref, bu_ref, wd_ref, bd_ref,
                ys_hbm, xu_ref, acc_ref, ystage_ref, in_sem, out_sem):
    w = pl.program_id(0)
    f = pl.program_id(1)
    nstage = ns_ref[w]
    nsub = nstage // (MOE_SUB // MOE_STAGE)
    start = is_ref[w]

    def stage_rows(s):
        return pl.ds(pl.multiple_of(s * MOE_STAGE, MOE_STAGE), MOE_STAGE)

    def hbm_rows(first, s, tiles):
        return pl.ds(pl.multiple_of((first + s * MOE_STAGE) * tiles, MOE_STAGE * tiles), MOE_STAGE * tiles)

    def load(first, s):
        dst = xu_ref.at[pl.ds(pl.multiple_of(s * MOE_STAGE * XROW_TILES, MOE_STAGE * XROW_TILES),
                              MOE_STAGE * XROW_TILES)]
        return pltpu.make_async_copy(xs_hbm.at[hbm_rows(first, s, XROW_TILES)], dst, in_sem.at[0])

    def fetch_item(first, stages):
        def issue(s, carry):
            load(first, s).start()
            return carry

        lax.fori_loop(0, stages, issue, 0)

    pl.when(jnp.logical_and(jnp.logical_and(w == 0, f == 0), nstage > 0))(lambda: fetch_item(start, nstage))

    @pl.when(jnp.logical_and(f == 0, nstage > 0))
    def _():
        def landed(s, carry):
            load(start, s).wait()
            return carry

        lax.fori_loop(0, nstage, landed, 0)

        def init(s, carry):
            acc_ref[stage_rows(s), :] = jnp.broadcast_to(bd_ref[...], (MOE_STAGE, D_MODEL))
            return carry

        lax.fori_loop(0, nstage, init, 0)

    def x_rows(row0, nrows):
        base = pl.multiple_of(row0 * XROW_TILES, MOE_STAGE * XROW_TILES)
        lows, highs = [], []
        for j in range(XROW_TILES):
            lo, hi = _unpack_halves(xu_ref[pl.ds(base + j, nrows, stride=XROW_TILES), :])
            lows.append(lo.astype(BF16))
            highs.append(hi.astype(BF16))
        return jnp.concatenate(lows + highs, axis=1)

    @pl.when(nstage > 0)
    def _():
        bg = bg_ref[...]
        bu = bu_ref[...]

        def expert_rows(row0, nrows):
            x = x_rows(row0, nrows)
            gt = jnp.dot(x, wg_ref[...].astype(BF16), preferred_element_type=F32) + bg
            up = jnp.dot(x, wu_ref[...].astype(BF16), preferred_element_type=F32) + bu
            gt = jnp.minimum(gt, SWIGLU_LIMIT)
            up = jnp.clip(up, -SWIGLU_LIMIT, SWIGLU_LIMIT)
            a = (gt * jax.nn.sigmoid(SWIGLU_ALPHA * gt) * (up + 1.0)).astype(BF16)
            rows = pl.ds(pl.multiple_of(row0, MOE_STAGE), nrows)
            acc_ref[rows, :] += jnp.dot(a, wd_ref[...].astype(BF16), preferred_element_type=F32)

        def body(s, carry):
            expert_rows(s * MOE_SUB, MOE_SUB)
            return carry

        lax.fori_loop(0, nsub, body, 0)
        done = nsub * MOE_SUB
        left = nstage - nsub * (MOE_SUB // MOE_STAGE)
        size = MOE_SUB // 2
        while size >= MOE_STAGE:
            here = left * MOE_STAGE >= size
            pl.when(here)(lambda done=done, size=size: expert_rows(done, size))
            done = done + jnp.where(here, size, 0)
            left = left - jnp.where(here, size // MOE_STAGE, 0)
            size //= 2

    @pl.when(jnp.logical_and(f == MOE_NF - 1, nstage > 0))
    def _():
        nxt = jnp.minimum(w + 1, pl.num_programs(0) - 1)
        nxt_stages = jnp.where(w + 1 < pl.num_programs(0), ns_ref[nxt], 0)
        fetch_item(is_ref[nxt], nxt_stages)

        def store(s):
            slot = s % MOE_OUT_SLOTS
            return pltpu.make_async_copy(ystage_ref.at[slot], ys_hbm.at[hbm_rows(start, s, YROW_TILES)],
                                         out_sem.at[slot])

        def write(s, carry):
            pl.when(s >= MOE_OUT_SLOTS)(lambda: store(s - MOE_OUT_SLOTS).wait())
            for j in range(YROW_TILES):
                lo = acc_ref[stage_rows(s), j * LANES:(j + 1) * LANES]
                hi = acc_ref[stage_rows(s), HALF + j * LANES:HALF + (j + 1) * LANES]
                ystage_ref[s % MOE_OUT_SLOTS, pl.ds(j, MOE_STAGE, stride=YROW_TILES), :] = _pack_halves(lo, hi)
            store(s).start()
            return carry

        lax.fori_loop(0, nstage, write, 0)
        for back in range(1, MOE_OUT_SLOTS + 1):
            pl.when(nstage >= back)(lambda back=back: store(nstage - back).wait())


def _moe(xs, item_expert, item_start, item_nsub, w_gate, b_gate, w_up, b_up, w_down, b_down):
    rows = xs.shape[0] // XROW_TILES
    nitems = item_expert.shape[0]

    def fcol(w, f, ie, st, ns):
        return jnp.where(ns[w] > 0, f, MOE_NF - 1)

    grid_spec = pltpu.PrefetchScalarGridSpec(
        num_scalar_prefetch=3,
        grid=(nitems, MOE_NF),
        in_specs=[pl.BlockSpec(memory_space=pl.ANY),
                  pl.BlockSpec((None, D_MODEL, MOE_TF), lambda w, f, ie, st, ns: (ie[w], 0, fcol(w, f, ie, st, ns))),
                  pl.BlockSpec((None, 1, MOE_TF), lambda w, f, ie, st, ns: (ie[w], 0, fcol(w, f, ie, st, ns))),
                  pl.BlockSpec((None, D_MODEL, MOE_TF), lambda w, f, ie, st, ns: (ie[w], 0, fcol(w, f, ie, st, ns))),
                  pl.BlockSpec((None, 1, MOE_TF), lambda w, f, ie, st, ns: (ie[w], 0, fcol(w, f, ie, st, ns))),
                  pl.BlockSpec((None, MOE_TF, D_MODEL), lambda w, f, ie, st, ns: (ie[w], fcol(w, f, ie, st, ns), 0)),
                  pl.BlockSpec((None, 1, D_MODEL), lambda w, f, ie, st, ns: (ie[w], 0, 0))],
        out_specs=pl.BlockSpec(memory_space=pl.ANY),
        scratch_shapes=[pltpu.VMEM((MOE_ROWS * XROW_TILES, LANES), jnp.uint32),
                        pltpu.VMEM((MOE_ROWS, D_MODEL), F32),
                        pltpu.VMEM((MOE_OUT_SLOTS, MOE_STAGE * YROW_TILES, LANES), jnp.uint32),
                        pltpu.SemaphoreType.DMA((1,)),
                        pltpu.SemaphoreType.DMA((MOE_OUT_SLOTS,))])
    return pl.pallas_call(
        _moe_kernel,
        out_shape=jax.ShapeDtypeStruct((rows * YROW_TILES, LANES), jnp.uint32),
        grid_spec=grid_spec,
        compiler_params=pltpu.CompilerParams(dimension_semantics=("arbitrary", "arbitrary"),
                                             vmem_limit_bytes=MOE_VMEM_LIMIT_BYTES, has_side_effects=True),
        name="moe_experts",
    )(item_expert, item_start, item_nsub, xs, w_gate, b_gate, w_up, b_up, w_down, b_down)


def _combine_kernel(pos_ref, nxt_ref, x2_ref, route_ref, g_ref, ys_hbm, o_ref, buf_ref, sem, *, tokens):
    i = pl.program_id(0)
    slot = i % 2

    def gather(p_ref, dst_slot):
        def issue(t, carry):
            for k in range(TOP_K):
                pltpu.make_async_copy(ys_hbm.at[_row_tile(p_ref[t * TOP_K + k], YROW_TILES)],
                                      buf_ref.at[dst_slot, k, _row_tile(t, YROW_TILES)],
                                      sem.at[dst_slot]).start(priority=k % 2)
            return carry

        lax.fori_loop(0, tokens, issue, 0, unroll=DMA_ISSUE_UNROLL)

    pl.when(i == 0)(lambda: gather(pos_ref, 0))
    pl.when(i + 1 < pl.num_programs(0))(lambda: gather(nxt_ref, 1 - slot))

    for k in range(TOP_K):
        pltpu.make_async_copy(ys_hbm.at[pl.ds(0, tokens * YROW_TILES)], buf_ref.at[slot, k], sem.at[slot]).wait()

    route = route_ref[...]
    gates = [route[:, TOP_K + k:TOP_K + k + 1] for k in range(TOP_K)]
    lows, highs = [], []
    for j in range(YROW_TILES):
        y_lo = x2_ref[:, j * LANES:(j + 1) * LANES]
        y_hi = x2_ref[:, HALF + j * LANES:HALF + (j + 1) * LANES]
        for k in range(TOP_K):
            lo, hi = _unpack_halves(buf_ref[slot, k, pl.ds(j, tokens, stride=YROW_TILES), :])
            y_lo = y_lo + gates[k] * lo
            y_hi = y_hi + gates[k] * hi
        lows.append(y_lo)
        highs.append(y_hi)
    o_ref[...] = _rms(jnp.concatenate(lows + highs, axis=1), g_ref[...])


def _combine(x2, route, final_norm, ys, pos_flat, tokens):
    t = x2.shape[0]
    nsteps = t // tokens
    kern = functools.partial(_combine_kernel, tokens=tokens)
    return pl.pallas_call(
        kern,
        out_shape=jax.ShapeDtypeStruct((t, D_MODEL), F32),
        grid=(nsteps,),
        in_specs=[pl.BlockSpec((tokens * TOP_K,), lambda i: (i,), memory_space=pltpu.SMEM),
                  pl.BlockSpec((tokens * TOP_K,), lambda i: (jnp.minimum(i + 1, nsteps - 1),),
                               memory_space=pltpu.SMEM),
                  pl.BlockSpec((tokens, D_MODEL), lambda i: (i, 0)),
                  pl.BlockSpec((tokens, LANES), lambda i: (i, 0)),
                  pl.BlockSpec((1, D_MODEL), lambda i: (0, 0)),
                  pl.BlockSpec(memory_space=pl.ANY)],
        out_specs=pl.BlockSpec((tokens, D_MODEL), lambda i: (i, 0)),
        scratch_shapes=[pltpu.VMEM((2, TOP_K, tokens * YROW_TILES, LANES), jnp.uint32),
                        pltpu.SemaphoreType.DMA((2,))],
        compiler_params=_cparams(("arbitrary",)),
        name="moe_combine",
    )(pos_flat, pos_flat, x2, route, final_norm, ys)


def _routing_tables(top_i, nitems):
    t = top_i.shape[0]
    sel = jnp.sum((top_i[:, :, None] == jnp.arange(N_EXPERTS, dtype=jnp.int32)[None, None, :]).astype(jnp.int32),
                  axis=1)
    cnt = jnp.sum(sel, axis=0)
    rank = jnp.cumsum(sel, axis=0) - sel
    cnt_pad = ((cnt + MOE_STAGE - 1) // MOE_STAGE) * MOE_STAGE
    off = jnp.cumsum(cnt_pad) - cnt_pad
    pos = jnp.take_along_axis(off[None, :] + rank, top_i, axis=1)
    tail_start = jnp.where(cnt > 0, off + cnt_pad - MOE_STAGE, -1)
    items_per = (cnt + MOE_ROWS - 1) // MOE_ROWS
    item_end = jnp.cumsum(items_per)
    total = item_end[-1]
    widx = jnp.arange(nitems, dtype=jnp.int32)
    e_of = jnp.minimum(jnp.searchsorted(item_end, widx, side="right"), N_EXPERTS - 1).astype(jnp.int32)
    j_of = widx - (item_end - items_per)[e_of]
    valid = widx < total
    e_last = e_of[jnp.maximum(total - 1, 0)]
    item_expert = jnp.where(valid, e_of, e_last).astype(jnp.int32)
    item_start = jnp.where(valid, off[e_of] + j_of * MOE_ROWS, 0).astype(jnp.int32)
    rows_left = cnt_pad[e_of] - j_of * MOE_ROWS
    item_nsub = jnp.where(valid, jnp.minimum(rows_left, MOE_ROWS) // MOE_STAGE, 0).astype(jnp.int32)
    return pos.reshape(t * TOP_K).astype(jnp.int32), tail_start.astype(jnp.int32), item_expert, item_start, item_nsub


def _trunk(x, conv_hist, s0, swa_fn, mk, mv, gdn_chunk, p):
    b, t, _ = x.shape
    m = b * t
    xf = x.reshape(m, D_MODEL)
    proj = _norm_matmul(xf, p["norm_mix"], p["w_in"], min(m, 1024), PROJ_TN)
    proj3 = proj.reshape(b, t, PROJ_COLS)
    hist8 = jnp.pad(conv_hist, ((0, 0), (SUBLANES - (CONV_W - 1), 0), (0, 0)))
    o_a, conv8, s_new = _gdn(proj3, hist8, s0, p["conv_w"], p["gate_par"], p["gdn_norm"], gdn_chunk,
                             GDN_SEQS_PER_STEP)
    o_b = swa_fn(proj3)
    x1, qm = _outproj(xf, o_a.reshape(m, V_A), o_b.reshape(m, Q_B), p["w_out"], p["norm_cross"], p["w_mq"], OUTPROJ_ROWS)
    tm = min(t, CROSS_ROWS)
    x2, hp, route = _cross_router(x1.reshape(b, t, D_MODEL), qm.reshape(b, t, MEM_W), mk, mv,
                                  p["w_mo"], p["norm_ffn"], p["w_router"], p["b_router"], CROSS_ROWS // tm, tm)
    return proj3, conv8[:, SUBLANES - (CONV_W - 1):], s_new, x2.reshape(m, D_MODEL), \
        hp.reshape(m * XROW_TILES, LANES), route.reshape(m, LANES)


def kernel(x_prompt, x_sample, cache_conv, state_gdn, cache_swa_k, cache_swa_v, cache_mem_k, cache_mem_v, mem_prompt, norm_mix, w_in, conv_w, gdn_a_log, gdn_dt_bias, gdn_norm, swa_sinks, rel_bias_table, w_out, norm_cross, norm_mem, w_mq, w_mk, w_mv, w_mo, norm_ffn, w_router, b_router, w_gate, b_gate, w_up, b_up, w_down, b_down, final_norm):
    depth = norm_mix.shape[0]
    assert depth == 1, "kernel is written for the single-layer trunk"
    bp, sp, _ = x_prompt.shape
    bs, ss, _ = x_sample.shape
    l = 0
    w = w_in[l]
    n_ab = 2 * GDN_HEADS
    w_perm = jnp.concatenate(
        [w[:, :CONV_CH + V_A].astype(BF16), w[:, CONV_CH + V_A + n_ab:].astype(BF16),
         w[:, CONV_CH + V_A:CONV_CH + V_A + n_ab].astype(BF16),
         jnp.zeros((D_MODEL, PROJ_COLS - w.shape[1]), BF16)], axis=1)
    gate_par = jnp.zeros((SUBLANES, LANES), F32)
    gate_par = gate_par.at[0, :GDN_HEADS].set(gdn_a_log[l]).at[1, :GDN_HEADS].set(gdn_dt_bias[l])
    p = dict(
        norm_mix=norm_mix[l].reshape(1, D_MODEL), w_in=w_perm, conv_w=conv_w[l], gate_par=gate_par,
        gdn_norm=gdn_norm[l].reshape(1, GDN_D), w_out=w_out[l].astype(BF16),
        norm_cross=norm_cross[l].reshape(1, D_MODEL), w_mq=w_mq[l].astype(BF16), w_mo=w_mo[l].astype(BF16),
        norm_ffn=norm_ffn[l].reshape(1, D_MODEL),
        w_router=_split_bf16(jnp.pad(w_router[l], ((0, 0), (0, LANES - N_EXPERTS)))),
        b_router=jnp.pad(b_router[l], (0, LANES - N_EXPERTS)).reshape(1, LANES))

    w_mkv = jnp.concatenate([w_mk[l], w_mv[l]], axis=1).astype(BF16)
    mkv = _norm_matmul(mem_prompt.reshape(bp * MEM_LEN, D_MODEL), norm_mem[l].reshape(1, D_MODEL), w_mkv,
                       min(bp * MEM_LEN, 1024), MEMKV_TN)
    mk_p = mkv[:, :MEM_W].reshape(bp, MEM_LEN * MEM_HEADS, MEM_DH)
    mv_p = mkv[:, MEM_W:].reshape(bp, MEM_LEN * MEM_HEADS, MEM_DH)

    zero_hist = jnp.zeros((bp, CONV_W - 1, CONV_CH), F32)
    zero_state = jnp.zeros((bp, GDN_HEADS, GDN_D, GDN_D), F32)
    swa_p = functools.partial(_swa_prompt, table=rel_bias_table, sinks=swa_sinks[l])
    proj_p, conv_p, st_p, x2_p, hp_p, route_p = _trunk(x_prompt, zero_hist, zero_state, swa_p, mk_p, mv_p, CHUNK, p)

    k_past = cache_swa_k[l].reshape(bs, WIN_PAST, KV_B)
    v_past = cache_swa_v[l].reshape(bs, WIN_PAST, KV_B)
    swa_s = functools.partial(_swa_sample, k_past=k_past, v_past=v_past, table=rel_bias_table, sinks=swa_sinks[l])
    proj_s, conv_s, st_s, x2_s, hp_s, route_s = _trunk(
        x_sample, cache_conv[l], state_gdn[l], swa_s,
        cache_mem_k[l].reshape(bs, MEM_LEN * MEM_HEADS, MEM_DH),
        cache_mem_v[l].reshape(bs, MEM_LEN * MEM_HEADS, MEM_DH), ss, p)

    mp, ms = bp * sp, bs * ss
    ntok = mp + ms
    top_i = jnp.concatenate([route_p[:, :TOP_K], route_s[:, :TOP_K]], axis=0).astype(jnp.int32)
    rows = ntok * TOP_K + N_EXPERTS * MOE_STAGE
    nitems = (ntok * TOP_K) // MOE_ROWS + N_EXPERTS
    pos_flat, tail_start, item_expert, item_start, item_nsub = _routing_tables(top_i, nitems)
    pos_p, pos_s = pos_flat[:mp * TOP_K], pos_flat[mp * TOP_K:]
    xs = _dispatch(hp_p, pos_p, tail_start, rows, 512)
    xs = _dispatch(hp_s, pos_s, tail_start, rows, 512, xs_prev=xs)
    ys = _moe(xs, item_expert, item_start, item_nsub,
              w_gate[l], b_gate[l].reshape(N_EXPERTS, 1, D_FF), w_up[l], b_up[l].reshape(N_EXPERTS, 1, D_FF),
              w_down[l], b_down[l].reshape(N_EXPERTS, 1, D_MODEL))
    fnorm = final_norm.reshape(1, D_MODEL)
    y_p = _combine(x2_p, route_p, fnorm, ys, pos_p, 256).reshape(bp, sp, D_MODEL)
    y_s = _combine(x2_s, route_s, fnorm, ys, pos_s, 256).reshape(bs, ss, D_MODEL)

    def kv_window(proj3, col, past=None):
        new = proj3[:, :, col * LANES:(col + 1) * LANES]
        full = new if past is None else jnp.concatenate([past, new], axis=1)
        win = full[:, -WIN_PAST:]
        return win.reshape(win.shape[0], WIN_PAST, SWA_KV_HEADS, SWA_DH)[None]

    return (y_p, y_s,
            conv_p[None], st_p[None], kv_window(proj_p, COL_KB), kv_window(proj_p, COL_VB),
            mk_p.reshape(bp, MEM_LEN, MEM_HEADS, MEM_DH)[None], mv_p.reshape(bp, MEM_LEN, MEM_HEADS, MEM_DH)[None],
            conv_s[None], st_s[None], kv_window(proj_s, COL_KB, k_past), kv_window(proj_s, COL_VB, v_past))
```

```python
import functools
import math

import numpy as np
import jax
import jax.numpy as jnp
from jax import lax
from jax.experimental import pallas as pl
from jax.experimental.pallas import tpu as pltpu

F32 = jnp.float32
BF16 = jnp.bfloat16
HIGHEST = lax.Precision.HIGHEST

D_MODEL = 2048
CHUNK = 64
GDN_HEADS = 8
GDN_D = 128
CONV_W = 4
SWA_HEADS = 16
SWA_KV_HEADS = 2
SWA_GROUP = SWA_HEADS // SWA_KV_HEADS
SWA_DH = 64
WIN_PAST = 128
REL_BUCKETS = 32
REL_MAX_DIST = 128
MEM_LEN = 256
MEM_HEADS = 4
MEM_DH = 128
N_EXPERTS = 32
TOP_K = 4
D_FF = D_MODEL
SWIGLU_LIMIT = 7.0
SWIGLU_ALPHA = 1.702
EPS = 1e-6

QK_A = GDN_HEADS * GDN_D
V_A = GDN_HEADS * GDN_D
CONV_CH = 2 * QK_A + V_A
Q_B = SWA_HEADS * SWA_DH
KV_B = SWA_KV_HEADS * SWA_DH
MEM_W = MEM_HEADS * MEM_DH

LANES = 128
SUBLANES = 8
VMEM_LIMIT_BYTES = 58 * 1024 * 1024
MOE_VMEM_LIMIT_BYTES = 60 * 1024 * 1024

PROJ_COLS = 5632
PROJ_TN = 1408
MEMKV_TN = 512
COL_Z = CONV_CH // V_A
COL_QB = (CONV_CH + V_A) // Q_B
COL_KB = (CONV_CH + V_A + Q_B) // LANES
COL_VB = COL_KB + 1
COL_AB = COL_KB + 2

MOE_SUB = 512
MOE_ROWS = 1536
MOE_STAGE = 128
MOE_OUT_SLOTS = 4
MOE_TF = 512
MOE_NF = D_FF // MOE_TF
HALF = D_MODEL // 2
NORM_ROWS = 256
SWA_CHUNKS_PER_STEP = 8
SWA_SEQS_PER_STEP = 4
CROSS_ROWS = 256
DMA_ISSUE_UNROLL = 8
OUTPROJ_ROWS = 512
GDN_SEQS_PER_STEP = 4
XROW_TILES = HALF // LANES
YROW_TILES = HALF // LANES


def _cparams(sem):
    return pltpu.CompilerParams(dimension_semantics=sem, vmem_limit_bytes=VMEM_LIMIT_BYTES)


def _rms(x, gain):
    return x * lax.rsqrt(jnp.mean(x * x, axis=-1, keepdims=True) + EPS) * gain


def _pack_halves(lo, hi):
    lo_bits = pltpu.bitcast(lo.astype(BF16).astype(F32), jnp.uint32)
    hi_bits = pltpu.bitcast(hi.astype(BF16).astype(F32), jnp.uint32)
    return (lo_bits >> 16) | (hi_bits & jnp.uint32(0xFFFF0000))


def _unpack_halves(u):
    return pltpu.bitcast(u << 16, F32), pltpu.bitcast(u & jnp.uint32(0xFFFF0000), F32)


def _norm_matmul_kernel(x_ref, g_ref, w_ref, o_ref, h_ref):
    @pl.when(pl.program_id(1) == 0)
    def _():
        def body(r, carry):
            rows = pl.ds(pl.multiple_of(r * NORM_ROWS, NORM_ROWS), NORM_ROWS)
            h_ref[rows, :] = _rms(x_ref[rows, :], g_ref[...]).astype(BF16)
            return carry

        lax.fori_loop(0, x_ref.shape[0] // NORM_ROWS, body, 0)

    o_ref[...] = jnp.dot(h_ref[...], w_ref[...], preferred_element_type=F32)


def _norm_matmul(x, gain, w, tm, tn):
    m, k = x.shape
    n = w.shape[1]
    return pl.pallas_call(
        _norm_matmul_kernel,
        out_shape=jax.ShapeDtypeStruct((m, n), F32),
        grid=(m // tm, n // tn),
        in_specs=[pl.BlockSpec((tm, k), lambda i, j: (i, 0)),
                  pl.BlockSpec((1, k), lambda i, j: (0, 0)),
                  pl.BlockSpec((k, tn), lambda i, j: (0, j))],
        out_specs=pl.BlockSpec((tm, tn), lambda i, j: (i, j)),
        scratch_shapes=[pltpu.VMEM((tm, k), BF16)],
        compiler_params=_cparams(("arbitrary", "arbitrary")),
        name="norm_matmul",
    )(x, gain, w)


def _tri_inverse(lows, c):
    row = lax.broadcasted_iota(jnp.int32, (c, c), 0)
    col = lax.broadcasted_iota(jnp.int32, (c, c), 1)
    eye = jnp.where(row == col, 1.0, 0.0).astype(F32)
    ps = [eye - low for low in lows]
    ms = list(lows)
    span = 1
    while 2 * span < c:
        mbs = [m.astype(BF16) for m in ms]
        ms = [jnp.dot(mb, mb, preferred_element_type=F32) for mb in mbs]
        ps = [p + jnp.dot(p.astype(BF16), m.astype(BF16), preferred_element_type=F32) for p, m in zip(ps, ms)]
        span *= 2
    return ps


def _gdn_kernel(qkv_ref, z_ref, ab_ref, hist_ref, s0_ref, convw_ref, gp_ref, norm_ref,
                o_ref, convnew_ref, s_ref, ext_ref, *, c):
    step = pl.program_id(1)
    nb = qkv_ref.shape[0]

    @pl.when(step == 0)
    def _():
        ext_ref[:, 0:SUBLANES, :] = hist_ref[...]
        s_ref[...] = s0_ref[...]

    row = lax.broadcasted_iota(jnp.int32, (c, c), 0)
    col = lax.broadcasted_iota(jnp.int32, (c, c), 1)
    incl = row >= col
    strict = row > col
    tril = jnp.where(incl, 1.0, 0.0).astype(F32)
    gp = gp_ref[...]
    scale = GDN_D ** -0.5
    heads = [(bi, h) for bi in range(nb) for h in range(GDN_HEADS)]
    s_old = [s_ref[bi, h] for bi, h in heads]

    acts, gc_alls, gc_ts, beta_alls = [], [], [], []
    for bi in range(nb):
        ext_ref[bi, SUBLANES:SUBLANES + c, :] = qkv_ref[bi]
        base = SUBLANES - (CONV_W - 1)
        conv = ext_ref[bi, base:base + c, :] * convw_ref[0:1, :]
        for j in range(1, CONV_W):
            conv = conv + ext_ref[bi, base + j:base + j + c, :] * convw_ref[j:j + 1, :]
        tail = ext_ref[bi, c:c + SUBLANES, :]
        convnew_ref[bi] = tail
        ext_ref[bi, 0:SUBLANES, :] = tail
        acts.append(conv * jax.nn.sigmoid(conv))
        ab = ab_ref[bi]
        g = -jnp.exp(gp[0:1, :]) * jax.nn.softplus(ab + gp[1:2, :])
        beta_alls.append(jax.nn.sigmoid(ab))
        gc_all = jnp.dot(tril, g, precision=HIGHEST, preferred_element_type=F32)
        gc_pad = jnp.concatenate([gc_all, jnp.zeros((LANES - c, LANES), F32)], axis=0) if c < LANES else gc_all
        gc_alls.append(gc_all)
        gc_ts.append(gc_pad.T)

    q, k, v = [], [], []
    for bi, h in heads:
        lo = h * GDN_D
        qh = acts[bi][:, lo:lo + GDN_D]
        kh = acts[bi][:, QK_A + lo:QK_A + lo + GDN_D]
        q.append(qh * lax.rsqrt(jnp.sum(qh * qh, axis=-1, keepdims=True) + EPS) * scale)
        k.append(kh * lax.rsqrt(jnp.sum(kh * kh, axis=-1, keepdims=True) + EPS))
        v.append(acts[bi][:, 2 * QK_A + lo:2 * QK_A + lo + GDN_D])
    beta = [beta_alls[bi][:, GDN_HEADS + h:GDN_HEADS + h + 1] for bi, h in heads]
    gc = [gc_alls[bi][:, h:h + 1] for bi, h in heads]
    gc_last = [gc_alls[bi][c - 1:c, h:h + 1] for bi, h in heads]
    heads = range(len(heads))
    decay = [jnp.exp(jnp.where(incl, gc[h] - gc_ts[h // GDN_HEADS][h % GDN_HEADS:h % GDN_HEADS + 1, 0:c],
                               -jnp.inf)) for h in heads]
    egc = [jnp.exp(gc[h]) for h in heads]
    kb = [k[h] * beta[h] for h in heads]
    qk = [lax.dot_general(jnp.concatenate([q[h], kb[h]], axis=0).astype(BF16), k[h].astype(BF16),
                          (((1,), (1,)), ((), ())), preferred_element_type=F32) for h in heads]
    a_qk = [jnp.where(incl, qk[h][0:c] * decay[h], 0.0) for h in heads]
    tinv = _tri_inverse([jnp.where(strict, qk[h][c:2 * c] * decay[h], 0.0) for h in heads], c)
    uw = [jnp.dot(tinv[h].astype(BF16),
                  jnp.concatenate([v[h] * beta[h], kb[h] * egc[h]], axis=1).astype(BF16),
                  preferred_element_type=F32) for h in heads]
    ws_qs = [jnp.dot(jnp.concatenate([uw[h][:, GDN_D:2 * GDN_D], q[h] * egc[h]], axis=0).astype(BF16),
                     s_old[h].astype(BF16), preferred_element_type=F32) for h in heads]
    v_new = [(uw[h][:, 0:GDN_D] - ws_qs[h][0:c]).astype(BF16) for h in heads]
    o = [ws_qs[h][c:2 * c] + jnp.dot(a_qk[h].astype(BF16), v_new[h], preferred_element_type=F32)
         for h in heads]
    s_new = [s_old[h] * jnp.exp(gc_last[h]) + lax.dot_general(
        (k[h] * jnp.exp(gc_last[h] - gc[h])).astype(BF16), v_new[h], (((0,), (0,)), ((), ())),
        preferred_element_type=F32) for h in heads]
    for bi in range(nb):
        z_all = z_ref[bi]
        gate = z_all * jax.nn.sigmoid(z_all)
        o_ref[bi] = jnp.concatenate([_rms(o[bi * GDN_HEADS + h], norm_ref[...]) for h in range(GDN_HEADS)],
                                    axis=1) * gate
    for h in heads:
        s_ref[h // GDN_HEADS, h % GDN_HEADS] = s_new[h]


def _gdn(proj3, hist8, s0, conv_w, gate_par, gdn_norm, c, nb):
    b, t, _ = proj3.shape
    nchunks = t // c
    kern = functools.partial(_gdn_kernel, c=c)
    return pl.pallas_call(
        kern,
        out_shape=(jax.ShapeDtypeStruct((b, t, V_A), F32),
                   jax.ShapeDtypeStruct((b, SUBLANES, CONV_CH), F32),
                   jax.ShapeDtypeStruct((b, GDN_HEADS, GDN_D, GDN_D), F32)),
        grid=(b // nb, nchunks),
        in_specs=[pl.BlockSpec((nb, c, CONV_CH), lambda i, j: (i, j, 0)),
                  pl.BlockSpec((nb, c, V_A), lambda i, j: (i, j, COL_Z)),
                  pl.BlockSpec((nb, c, LANES), lambda i, j: (i, j, COL_AB)),
                  pl.BlockSpec((nb, SUBLANES, CONV_CH), lambda i, j: (i, 0, 0)),
                  pl.BlockSpec((nb, GDN_HEADS, GDN_D, GDN_D), lambda i, j: (i, 0, 0, 0)),
                  pl.BlockSpec((CONV_W, CONV_CH), lambda i, j: (0, 0)),
                  pl.BlockSpec((SUBLANES, LANES), lambda i, j: (0, 0)),
                  pl.BlockSpec((1, GDN_D), lambda i, j: (0, 0))],
        out_specs=(pl.BlockSpec((nb, c, V_A), lambda i, j: (i, j, 0)),
                   pl.BlockSpec((nb, SUBLANES, CONV_CH), lambda i, j: (i, 0, 0)),
                   pl.BlockSpec((nb, GDN_HEADS, GDN_D, GDN_D), lambda i, j: (i, 0, 0, 0))),
        scratch_shapes=[pltpu.VMEM((nb, SUBLANES + c, CONV_CH), F32)],
        compiler_params=_cparams(("arbitrary", "arbitrary")),
        name="gdn_mixer",
    )(proj3, proj3, proj3, hist8, s0, conv_w, gate_par, gdn_norm)


def _swa_kernel(*refs, nq, nsub, piece_rows, masked):
    npieces = len(piece_rows)
    q_ref = refs[0]
    k_refs = refs[1:1 + npieces]
    v_refs = refs[1 + npieces:1 + 2 * npieces]
    bias_ref, sink_ref, o_ref = refs[1 + 2 * npieces:]
    nk = WIN_PAST + nq
    nseq = q_ref.shape[0]
    kcat = [jnp.concatenate([r[b] for r in k_refs], axis=0).astype(BF16) for b in range(nseq)]
    vcat = [jnp.concatenate([r[b] for r in v_refs], axis=0).astype(BF16) for b in range(nseq)]
    q = [q_ref[b] for b in range(nseq)]
    kidx = lax.broadcasted_iota(jnp.int32, (SWA_GROUP * nq, nk), 1)
    blocks = [(b, i, kv) for b in range(nseq) for i in range(nsub) for kv in range(SWA_KV_HEADS)]
    scores = []
    for b, i, kv in blocks:
        qs = jnp.concatenate(
            [q[b][i * nq:(i + 1) * nq, (kv * SWA_GROUP + g) * SWA_DH:(kv * SWA_GROUP + g + 1) * SWA_DH]
             for g in range(SWA_GROUP)], axis=0).astype(BF16)
        kh = kcat[b][i * nq:i * nq + nk, kv * SWA_DH:(kv + 1) * SWA_DH]
        s = lax.dot_general(qs, kh, (((1,), (1,)), ((), ())), preferred_element_type=F32)
        s = s * SWA_DH ** -0.5 + bias_ref[kv]
        if masked:
            first = (pl.program_id(1) * nsub + i) * nq - WIN_PAST
            s = jnp.where(kidx + first >= 0, s, -jnp.inf)
        scores.append(s)
    sinks = [sink_ref[kv] for _, _, kv in blocks]
    maxes = [jnp.maximum(jnp.max(s, axis=-1, keepdims=True), sk) for s, sk in zip(scores, sinks)]
    probs = [jnp.exp(s - m) for s, m in zip(scores, maxes)]
    dens = [jnp.sum(p, axis=-1, keepdims=True) + jnp.exp(sk - m) for p, sk, m in zip(probs, sinks, maxes)]
    outs = [jnp.dot((p / den).astype(BF16), vcat[b][i * nq:i * nq + nk, kv * SWA_DH:(kv + 1) * SWA_DH],
                    preferred_element_type=F32) for (b, i, kv), p, den in zip(blocks, probs, dens)]
    for b in range(nseq):
        out_rows = []
        for i in range(nsub):
            heads = []
            for kv in range(SWA_KV_HEADS):
                o = outs[(b * nsub + i) * SWA_KV_HEADS + kv]
                heads += [o[g * nq:(g + 1) * nq, :] for g in range(SWA_GROUP)]
            out_rows.append(jnp.concatenate(heads, axis=1))
        o_ref[b] = out_rows[0] if nsub == 1 else jnp.concatenate(out_rows, axis=0)


def _rel_bucket(nq, nk):
    rel = jnp.arange(nk)[None, :] - WIN_PAST - jnp.arange(nq)[:, None]
    nb = REL_BUCKETS // 2
    max_exact = nb // 2
    n = jnp.abs(rel)
    large = max_exact + (jnp.log(jnp.maximum(n, 1).astype(F32) / max_exact)
                         / math.log(REL_MAX_DIST / max_exact) * (nb - max_exact)).astype(jnp.int32)
    large = jnp.minimum(large, nb - 1)
    return jnp.where(rel > 0, nb, 0) + jnp.where(n < max_exact, n, large)


def _swa_tables(table, sinks, nq, nk):
    onehot = (_rel_bucket(nq, nk)[:, :, None] == jnp.arange(REL_BUCKETS)[None, None, :]).astype(F32)
    bias = jnp.einsum("qkb,bh->qkh", onehot, table.astype(F32), precision=HIGHEST)
    bias = jnp.transpose(bias, (2, 0, 1)).astype(F32)
    bias = bias.reshape(SWA_KV_HEADS, SWA_GROUP * nq, nk)
    sk = jnp.broadcast_to(sinks.astype(F32).reshape(SWA_KV_HEADS, SWA_GROUP, 1, 1),
                          (SWA_KV_HEADS, SWA_GROUP, nq, 1)).reshape(SWA_KV_HEADS, SWA_GROUP * nq, 1)
    return bias, sk


def _swa_prompt(proj3, table, sinks):
    b, t, _ = proj3.shape
    nq = CHUNK
    nsub = SWA_CHUNKS_PER_STEP
    rows = nsub * nq
    assert rows % WIN_PAST == 0 and t % rows == 0
    past_per_step = rows // WIN_PAST
    bias, sk = _swa_tables(table, sinks, nq, WIN_PAST + nq)
    kern = functools.partial(_swa_kernel, nq=nq, nsub=nsub, piece_rows=(WIN_PAST, rows), masked=True)

    def past_spec(colblk):
        return pl.BlockSpec((1, WIN_PAST, LANES),
                            lambda i, j: (i, jnp.maximum(j * past_per_step - 1, 0), colblk))

    def cur_spec(colblk):
        return pl.BlockSpec((1, rows, LANES), lambda i, j: (i, j, colblk))

    in_specs = [pl.BlockSpec((1, rows, Q_B), lambda i, j: (i, j, COL_QB)),
                past_spec(COL_KB), cur_spec(COL_KB), past_spec(COL_VB), cur_spec(COL_VB),
                pl.BlockSpec(bias.shape, lambda i, j: (0, 0, 0)),
                pl.BlockSpec(sk.shape, lambda i, j: (0, 0, 0))]
    return pl.pallas_call(
        kern,
        out_shape=jax.ShapeDtypeStruct((b, t, Q_B), F32),
        grid=(b, t // rows),
        in_specs=in_specs,
        out_specs=pl.BlockSpec((1, rows, Q_B), lambda i, j: (i, j, 0)),
        compiler_params=_cparams(("arbitrary", "arbitrary")),
        name="swa_prompt",
    )(proj3, proj3, proj3, proj3, proj3, bias, sk)


def _swa_sample(proj3, k_past, v_past, table, sinks):
    b, t, _ = proj3.shape
    bias, sk = _swa_tables(table, sinks, t, WIN_PAST + t)
    kern = functools.partial(_swa_kernel, nq=t, nsub=1, piece_rows=(WIN_PAST, t), masked=False)
    ns = SWA_SEQS_PER_STEP
    in_specs = [pl.BlockSpec((ns, t, Q_B), lambda i, j: (i, 0, COL_QB)),
                pl.BlockSpec((ns, WIN_PAST, LANES), lambda i, j: (i, 0, 0)),
                pl.BlockSpec((ns, t, LANES), lambda i, j: (i, 0, COL_KB)),
                pl.BlockSpec((ns, WIN_PAST, LANES), lambda i, j: (i, 0, 0)),
                pl.BlockSpec((ns, t, LANES), lambda i, j: (i, 0, COL_VB)),
                pl.BlockSpec(bias.shape, lambda i, j: (0, 0, 0)),
                pl.BlockSpec(sk.shape, lambda i, j: (0, 0, 0))]
    return pl.pallas_call(
        kern,
        out_shape=jax.ShapeDtypeStruct((b, t, Q_B), F32),
        grid=(b // ns, 1),
        in_specs=in_specs,
        out_specs=pl.BlockSpec((ns, t, Q_B), lambda i, j: (i, 0, 0)),
        compiler_params=_cparams(("arbitrary", "arbitrary")),
        name="swa_sample",
    )(proj3, k_past, proj3, v_past, proj3, bias, sk)


def _outproj_kernel(x_ref, oa_ref, ob_ref, wa_ref, wb_ref, g_ref, wq_ref, x1_ref, q_ref):
    x1 = (x_ref[...]
          + jnp.dot(oa_ref[...].astype(BF16), wa_ref[...], preferred_element_type=F32)
          + jnp.dot(ob_ref[...].astype(BF16), wb_ref[...], preferred_element_type=F32))
    x1_ref[...] = x1
    hc = _rms(x1, g_ref[...]).astype(BF16)
    q_ref[...] = jnp.dot(hc, wq_ref[...], preferred_element_type=F32)


def _outproj(x, o_a, o_b, w_out, norm_cross, w_mq, tm):
    m = x.shape[0]
    return pl.pallas_call(
        _outproj_kernel,
        out_shape=(jax.ShapeDtypeStruct((m, D_MODEL), F32), jax.ShapeDtypeStruct((m, MEM_W), F32)),
        grid=(m // tm,),
        in_specs=[pl.BlockSpec((tm, D_MODEL), lambda i: (i, 0)),
                  pl.BlockSpec((tm, V_A), lambda i: (i, 0)),
                  pl.BlockSpec((tm, Q_B), lambda i: (i, 0)),
                  pl.BlockSpec((V_A, D_MODEL), lambda i: (0, 0)),
                  pl.BlockSpec((Q_B, D_MODEL), lambda i: (1, 0)),
                  pl.BlockSpec((1, D_MODEL), lambda i: (0, 0)),
                  pl.BlockSpec((D_MODEL, MEM_W), lambda i: (0, 0))],
        out_specs=(pl.BlockSpec((tm, D_MODEL), lambda i: (i, 0)),
                   pl.BlockSpec((tm, MEM_W), lambda i: (i, 0))),
        compiler_params=_cparams(("arbitrary",)),
        name="outproj",
    )(x, o_a, o_b, w_out, w_out, norm_cross, w_mq)


def _cross_router_kernel(x1_ref, q_ref, mk_ref, mv_ref, wo_ref, g_ref, wr_ref, br_ref,
                         x2_ref, hp_ref, route_ref):
    nb, tm = x1_ref.shape[0], x1_ref.shape[1]
    rows = []
    for bi in range(nb):
        q = q_ref[bi]
        outs = []
        for h in range(MEM_HEADS):
            sl = slice(h * MEM_DH, (h + 1) * MEM_DH)
            mk = mk_ref[bi, pl.ds(h, MEM_LEN, stride=MEM_HEADS), :].astype(BF16)
            mv = mv_ref[bi, pl.ds(h, MEM_LEN, stride=MEM_HEADS), :].astype(BF16)
            s = lax.dot_general(q[:, sl].astype(BF16), mk, (((1,), (1,)), ((), ())),
                                preferred_element_type=F32) * MEM_DH ** -0.5
            m = jnp.max(s, axis=-1, keepdims=True)
            p = jnp.exp(s - m)
            p = (p / jnp.sum(p, axis=-1, keepdims=True)).astype(BF16)
            outs.append(jnp.dot(p, mv, preferred_element_type=F32))
        rows.append(jnp.concatenate(outs, axis=1))
    att = (rows[0] if nb == 1 else jnp.concatenate(rows, axis=0)).astype(BF16)
    x2 = x1_ref[...].reshape(nb * tm, D_MODEL) + jnp.dot(att, wo_ref[...], preferred_element_type=F32)
    x2_ref[...] = x2.reshape(nb, tm, D_MODEL)
    hf = _rms(x2, g_ref[...])
    packed = _pack_halves(hf[:, 0:HALF], hf[:, HALF:D_MODEL])
    for bi in range(nb):
        for j in range(XROW_TILES):
            hp_ref[bi, pl.ds(j, tm, stride=XROW_TILES), :] = packed[bi * tm:(bi + 1) * tm, j * LANES:(j + 1) * LANES]
    hf_hi = hf.astype(BF16)
    hf_lo = (hf - hf_hi.astype(F32)).astype(BF16)
    wr = wr_ref[...]
    r_hi = jnp.dot(hf_hi, wr, preferred_element_type=F32)
    r_lo = jnp.dot(hf_lo, wr[:, 0:LANES], preferred_element_type=F32)
    logits = r_hi[:, 0:LANES] + r_hi[:, LANES:2 * LANES] + r_lo + br_ref[...]
    lane = lax.broadcasted_iota(jnp.int32, logits.shape, 1)
    lanef = lane.astype(F32)
    l = jnp.where(lane < N_EXPERTS, logits, -jnp.inf)
    vals, idxs = [], []
    for _ in range(TOP_K):
        m = jnp.max(l, axis=-1, keepdims=True)
        idx = jnp.min(jnp.where(l == m, lanef, float(LANES)), axis=-1, keepdims=True)
        vals.append(m)
        idxs.append(idx)
        l = jnp.where(lanef == idx, -jnp.inf, l)
    es = [jnp.exp(v - vals[0]) for v in vals]
    den = es[0] + es[1] + es[2] + es[3]
    route = jnp.zeros(logits.shape, F32)
    for k in range(TOP_K):
        route = jnp.where(lane == k, idxs[k], route)
        route = jnp.where(lane == TOP_K + k, es[k] / den, route)
    route_ref[...] = route.reshape(nb, tm, LANES)


def _split_bf16(w):
    w_hi = w.astype(BF16)
    w_lo = (w - w_hi.astype(F32)).astype(BF16)
    return jnp.concatenate([w_hi, w_lo], axis=1)


def _cross_router(x1, qm, mk, mv, w_mo, norm_ffn, w_router, b_router, nb, tm):
    b, t, _ = x1.shape
    nt = t // tm
    return pl.pallas_call(
        _cross_router_kernel,
        out_shape=(jax.ShapeDtypeStruct((b, t, D_MODEL), F32),
                   jax.ShapeDtypeStruct((b, t * XROW_TILES, LANES), jnp.uint32),
                   jax.ShapeDtypeStruct((b, t, LANES), F32)),
        grid=(b // nb, nt),
        in_specs=[pl.BlockSpec((nb, tm, D_MODEL), lambda i, j: (i, j, 0)),
                  pl.BlockSpec((nb, tm, MEM_W), lambda i, j: (i, j, 0)),
                  pl.BlockSpec((nb, MEM_LEN * MEM_HEADS, MEM_DH), lambda i, j: (i, 0, 0)),
                  pl.BlockSpec((nb, MEM_LEN * MEM_HEADS, MEM_DH), lambda i, j: (i, 0, 0)),
                  pl.BlockSpec((MEM_W, D_MODEL), lambda i, j: (0, 0)),
                  pl.BlockSpec((1, D_MODEL), lambda i, j: (0, 0)),
                  pl.BlockSpec((D_MODEL, 2 * LANES), lambda i, j: (0, 0)),
                  pl.BlockSpec((1, LANES), lambda i, j: (0, 0))],
        out_specs=(pl.BlockSpec((nb, tm, D_MODEL), lambda i, j: (i, j, 0)),
                   pl.BlockSpec((nb, tm * XROW_TILES, LANES), lambda i, j: (i, j, 0)),
                   pl.BlockSpec((nb, tm, LANES), lambda i, j: (i, j, 0))),
        compiler_params=_cparams(("arbitrary", "arbitrary")),
        name="cross_router",
    )(x1, qm, mk, mv, w_mo, norm_ffn, w_router, b_router)


def _row_tile(idx, tiles):
    return pl.ds(pl.multiple_of(idx * tiles, tiles), tiles)


def _dispatch_kernel(*refs, tokens, first_group):
    if first_group:
        pos_ref, tail_ref, hp_ref, xs_hbm, zero_ref, sem = refs
    else:
        pos_ref, tail_ref, hp_ref, _, xs_hbm, zero_ref, sem = refs

    def zero_tails():
        zero_ref[...] = jnp.zeros(zero_ref.shape, zero_ref.dtype)

        def tail_copy(e):
            dst = xs_hbm.at[pl.ds(pl.multiple_of(tail_ref[e] * XROW_TILES, XROW_TILES), MOE_STAGE * XROW_TILES)]
            return pltpu.make_async_copy(zero_ref, dst, sem.at[0])

        for e in range(N_EXPERTS):
            pl.when(tail_ref[e] >= 0)(lambda e=e: tail_copy(e).start())
        for e in range(N_EXPERTS):
            pl.when(tail_ref[e] >= 0)(lambda e=e: tail_copy(e).wait())

    if first_group:
        pl.when(pl.program_id(0) == 0)(zero_tails)

    def issue(t, carry):
        src = hp_ref.at[_row_tile(t, XROW_TILES)]
        for k in range(TOP_K):
            dst = xs_hbm.at[_row_tile(pos_ref[t * TOP_K + k], XROW_TILES)]
            pltpu.make_async_copy(src, dst, sem.at[1]).start(priority=k % 2)
        return carry

    lax.fori_loop(0, tokens, issue, 0, unroll=DMA_ISSUE_UNROLL)

    for k in range(TOP_K):
        pltpu.make_async_copy(hp_ref, xs_hbm.at[pl.ds(0, tokens * XROW_TILES)], sem.at[1]).wait()


def _dispatch(hp, pos_flat, tail_start, rows, tokens, xs_prev=None):
    t = hp.shape[0] // XROW_TILES
    first_group = xs_prev is None
    kern = functools.partial(_dispatch_kernel, tokens=tokens, first_group=first_group)
    in_specs = [pl.BlockSpec((tokens * TOP_K,), lambda i: (i,), memory_space=pltpu.SMEM),
                pl.BlockSpec((N_EXPERTS,), lambda i: (0,), memory_space=pltpu.SMEM),
                pl.BlockSpec((tokens * XROW_TILES, LANES), lambda i: (i, 0))]
    args = [pos_flat, tail_start, hp]
    if not first_group:
        in_specs.append(pl.BlockSpec(memory_space=pl.ANY))
        args.append(xs_prev)
    return pl.pallas_call(
        kern,
        out_shape=jax.ShapeDtypeStruct((rows * XROW_TILES, LANES), jnp.uint32),
        grid=(t // tokens,),
        in_specs=in_specs,
        out_specs=pl.BlockSpec(memory_space=pl.ANY),
        scratch_shapes=[pltpu.VMEM((MOE_STAGE * XROW_TILES, LANES), jnp.uint32),
                        pltpu.SemaphoreType.DMA((2,))],
        input_output_aliases={} if first_group else {3: 0},
        compiler_params=pltpu.CompilerParams(dimension_semantics=("arbitrary",),
                                             vmem_limit_bytes=VMEM_LIMIT_BYTES, has_side_effects=True),
        name="moe_dispatch",
    )(*args)


def _moe_kernel(ie_ref, is_ref, ns_ref, xs_hbm, wg_ref, bg_ref, wu_ref, bu_ref, wd_ref, bd_ref,
                ys_hbm, xu_ref, acc_ref, ystage_ref, in_sem, out_sem):
    w = pl.program_id(0)
    f = pl.program_id(1)
    nstage = ns_ref[w]
    nsub = nstage // (MOE_SUB // MOE_STAGE)
    start = is_ref[w]

    def stage_rows(s):
        return pl.ds(pl.multiple_of(s * MOE_STAGE, MOE_STAGE), MOE_STAGE)

    def hbm_rows(first, s, tiles):
        return pl.ds(pl.multiple_of((first + s * MOE_STAGE) * tiles, MOE_STAGE * tiles), MOE_STAGE * tiles)

    def load(first, s):
        dst = xu_ref.at[pl.ds(pl.multiple_of(s * MOE_STAGE * XROW_TILES, MOE_STAGE * XROW_TILES),
                              MOE_STAGE * XROW_TILES)]
        return pltpu.make_async_copy(xs_hbm.at[hbm_rows(first, s, XROW_TILES)], dst, in_sem.at[0])

    def fetch_item(first, stages):
        def issue(s, carry):
            load(first, s).start()
            return carry

        lax.fori_loop(0, stages, issue, 0)

    pl.when(jnp.logical_and(jnp.logical_and(w == 0, f == 0), nstage > 0))(lambda: fetch_item(start, nstage))

    @pl.when(jnp.logical_and(f == 0, nstage > 0))
    def _():
        def landed(s, carry):
            load(start, s).wait()
            return carry

        lax.fori_loop(0, nstage, landed, 0)

        def init(s, carry):
            acc_ref[stage_rows(s), :] = jnp.broadcast_to(bd_ref[...], (MOE_STAGE, D_MODEL))
            return carry

        lax.fori_loop(0, nstage, init, 0)

    def x_rows(row0, nrows):
        base = pl.multiple_of(row0 * XROW_TILES, MOE_STAGE * XROW_TILES)
        lows, highs = [], []
        for j in range(XROW_TILES):
            lo, hi = _unpack_halves(xu_ref[pl.ds(base + j, nrows, stride=XROW_TILES), :])
            lows.append(lo.astype(BF16))
            highs.append(hi.astype(BF16))
        return jnp.concatenate(lows + highs, axis=1)

    @pl.when(nstage > 0)
    def _():
        bg = bg_ref[...]
        bu = bu_ref[...]

        def expert_rows(row0, nrows):
            x = x_rows(row0, nrows)
            gt = jnp.dot(x, wg_ref[...].astype(BF16), preferred_element_type=F32) + bg
            up = jnp.dot(x, wu_ref[...].astype(BF16), preferred_element_type=F32) + bu
            gt = jnp.minimum(gt, SWIGLU_LIMIT)
            up = jnp.clip(up, -SWIGLU_LIMIT, SWIGLU_LIMIT)
            a = (gt * jax.nn.sigmoid(SWIGLU_ALPHA * gt) * (up + 1.0)).astype(BF16)
            rows = pl.ds(pl.multiple_of(row0, MOE_STAGE), nrows)
            acc_ref[rows, :] += jnp.dot(a, wd_ref[...].astype(BF16), preferred_element_type=F32)

        def body(s, carry):
            expert_rows(s * MOE_SUB, MOE_SUB)
            return carry

        lax.fori_loop(0, nsub, body, 0)
        done = nsub * MOE_SUB
        left = nstage - nsub * (MOE_SUB // MOE_STAGE)
        size = MOE_SUB // 2
        while size >= MOE_STAGE:
            here = left * MOE_STAGE >= size
            pl.when(here)(lambda done=done, size=size: expert_rows(done, size))
            done = done + jnp.where(here, size, 0)
            left = left - jnp.where(here, size // MOE_STAGE, 0)
            size //= 2

    @pl.when(jnp.logical_and(f == MOE_NF - 1, nstage > 0))
    def _():
        nxt = jnp.minimum(w + 1, pl.num_programs(0) - 1)
        nxt_stages = jnp.where(w + 1 < pl.num_programs(0), ns_ref[nxt], 0)
        fetch_item(is_ref[nxt], nxt_stages)

        def store(s):
            slot = s % MOE_OUT_SLOTS
            return pltpu.make_async_copy(ystage_ref.at[slot], ys_hbm.at[hbm_rows(start, s, YROW_TILES)],
                                         out_sem.at[slot])

        def write(s, carry):
            pl.when(s >= MOE_OUT_SLOTS)(lambda: store(s - MOE_OUT_SLOTS).wait())
            for j in range(YROW_TILES):
                lo = acc_ref[stage_rows(s), j * LANES:(j + 1) * LANES]
                hi = acc_ref[stage_rows(s), HALF + j * LANES:HALF + (j + 1) * LANES]
                ystage_ref[s % MOE_OUT_SLOTS, pl.ds(j, MOE_STAGE, stride=YROW_TILES), :] = _pack_halves(lo, hi)
            store(s).start()
            return carry

        lax.fori_loop(0, nstage, write, 0)
        for back in range(1, MOE_OUT_SLOTS + 1):
            pl.when(nstage >= back)(lambda back=back: store(nstage - back).wait())


def _moe(xs, item_expert, item_start, item_nsub, w_gate, b_gate, w_up, b_up, w_down, b_down):
    rows = xs.shape[0] // XROW_TILES
    nitems = item_expert.shape[0]

    def fcol(w, f, ie, st, ns):
        return jnp.where(ns[w] > 0, f, MOE_NF - 1)

    grid_spec = pltpu.PrefetchScalarGridSpec(
        num_scalar_prefetch=3,
        grid=(nitems, MOE_NF),
        in_specs=[pl.BlockSpec(memory_space=pl.ANY),
                  pl.BlockSpec((None, D_MODEL, MOE_TF), lambda w, f, ie, st, ns: (ie[w], 0, fcol(w, f, ie, st, ns))),
                  pl.BlockSpec((None, 1, MOE_TF), lambda w, f, ie, st, ns: (ie[w], 0, fcol(w, f, ie, st, ns))),
                  pl.BlockSpec((None, D_MODEL, MOE_TF), lambda w, f, ie, st, ns: (ie[w], 0, fcol(w, f, ie, st, ns))),
                  pl.BlockSpec((None, 1, MOE_TF), lambda w, f, ie, st, ns: (ie[w], 0, fcol(w, f, ie, st, ns))),
                  pl.BlockSpec((None, MOE_TF, D_MODEL), lambda w, f, ie, st, ns: (ie[w], fcol(w, f, ie, st, ns), 0)),
                  pl.BlockSpec((None, 1, D_MODEL), lambda w, f, ie, st, ns: (ie[w], 0, 0))],
        out_specs=pl.BlockSpec(memory_space=pl.ANY),
        scratch_shapes=[pltpu.VMEM((MOE_ROWS * XROW_TILES, LANES), jnp.uint32),
                        pltpu.VMEM((MOE_ROWS, D_MODEL), F32),
                        pltpu.VMEM((MOE_OUT_SLOTS, MOE_STAGE * YROW_TILES, LANES), jnp.uint32),
                        pltpu.SemaphoreType.DMA((1,)),
                        pltpu.SemaphoreType.DMA((MOE_OUT_SLOTS,))])
    return pl.pallas_call(
        _moe_kernel,
        out_shape=jax.ShapeDtypeStruct((rows * YROW_TILES, LANES), jnp.uint32),
        grid_spec=grid_spec,
        compiler_params=pltpu.CompilerParams(dimension_semantics=("arbitrary", "arbitrary"),
                                             vmem_limit_bytes=MOE_VMEM_LIMIT_BYTES, has_side_effects=True),
        name="moe_experts",
    )(item_expert, item_start, item_nsub, xs, w_gate, b_gate, w_up, b_up, w_down, b_down)


def _combine_kernel(pos_ref, nxt_ref, x2_ref, route_ref, g_ref, ys_hbm, o_ref, buf_ref, sem, *, tokens):
    i = pl.program_id(0)
    slot = i % 2

    def gather(p_ref, dst_slot):
        def issue(t, carry):
            for k in range(TOP_K):
                pltpu.make_async_copy(ys_hbm.at[_row_tile(p_ref[t * TOP_K + k], YROW_TILES)],
                                      buf_ref.at[dst_slot, k, _row_tile(t, YROW_TILES)],
                                      sem.at[dst_slot]).start(priority=k % 2)
            return carry

        lax.fori_loop(0, tokens, issue, 0, unroll=DMA_ISSUE_UNROLL)

    pl.when(i == 0)(lambda: gather(pos_ref, 0))
    pl.when(i + 1 < pl.num_programs(0))(lambda: gather(nxt_ref, 1 - slot))

    for k in range(TOP_K):
        pltpu.make_async_copy(ys_hbm.at[pl.ds(0, tokens * YROW_TILES)], buf_ref.at[slot, k], sem.at[slot]).wait()

    route = route_ref[...]
    gates = [route[:, TOP_K + k:TOP_K + k + 1] for k in range(TOP_K)]
    lows, highs = [], []
    for j in range(YROW_TILES):
        y_lo = x2_ref[:, j * LANES:(j + 1) * LANES]
        y_hi = x2_ref[:, HALF + j * LANES:HALF + (j + 1) * LANES]
        for k in range(TOP_K):
            lo, hi = _unpack_halves(buf_ref[slot, k, pl.ds(j, tokens, stride=YROW_TILES), :])
            y_lo = y_lo + gates[k] * lo
            y_hi = y_hi + gates[k] * hi
        lows.append(y_lo)
        highs.append(y_hi)
    o_ref[...] = _rms(jnp.concatenate(lows + highs, axis=1), g_ref[...])


def _combine(x2, route, final_norm, ys, pos_flat, tokens):
    t = x2.shape[0]
    nsteps = t // tokens
    kern = functools.partial(_combine_kernel, tokens=tokens)
    return pl.pallas_call(
        kern,
        out_shape=jax.ShapeDtypeStruct((t, D_MODEL), F32),
        grid=(nsteps,),
        in_specs=[pl.BlockSpec((tokens * TOP_K,), lambda i: (i,), memory_space=pltpu.SMEM),
                  pl.BlockSpec((tokens * TOP_K,), lambda i: (jnp.minimum(i + 1, nsteps - 1),),
                               memory_space=pltpu.SMEM),
                  pl.BlockSpec((tokens, D_MODEL), lambda i: (i, 0)),
                  pl.BlockSpec((tokens, LANES), lambda i: (i, 0)),
                  pl.BlockSpec((1, D_MODEL), lambda i: (0, 0)),
                  pl.BlockSpec(memory_space=pl.ANY)],
        out_specs=pl.BlockSpec((tokens, D_MODEL), lambda i: (i, 0)),
        scratch_shapes=[pltpu.VMEM((2, TOP_K, tokens * YROW_TILES, LANES), jnp.uint32),
                        pltpu.SemaphoreType.DMA((2,))],
        compiler_params=_cparams(("arbitrary",)),
        name="moe_combine",
    )(pos_flat, pos_flat, x2, route, final_norm, ys)


def _routing_tables(top_i, nitems):
    t = top_i.shape[0]
    sel = jnp.sum((top_i[:, :, None] == jnp.arange(N_EXPERTS, dtype=jnp.int32)[None, None, :]).astype(jnp.int32),
                  axis=1)
    cnt = jnp.sum(sel, axis=0)
    rank = jnp.cumsum(sel, axis=0) - sel
    cnt_pad = ((cnt + MOE_STAGE - 1) // MOE_STAGE) * MOE_STAGE
    off = jnp.cumsum(cnt_pad) - cnt_pad
    pos = jnp.take_along_axis(off[None, :] + rank, top_i, axis=1)
    tail_start = jnp.where(cnt > 0, off + cnt_pad - MOE_STAGE, -1)
    items_per = (cnt + MOE_ROWS - 1) // MOE_ROWS
    item_end = jnp.cumsum(items_per)
    total = item_end[-1]
    widx = jnp.arange(nitems, dtype=jnp.int32)
    e_of = jnp.minimum(jnp.searchsorted(item_end, widx, side="right"), N_EXPERTS - 1).astype(jnp.int32)
    j_of = widx - (item_end - items_per)[e_of]
    valid = widx < total
    e_last = e_of[jnp.maximum(total - 1, 0)]
    item_expert = jnp.where(valid, e_of, e_last).astype(jnp.int32)
    item_start = jnp.where(valid, off[e_of] + j_of * MOE_ROWS, 0).astype(jnp.int32)
    rows_left = cnt_pad[e_of] - j_of * MOE_ROWS
    item_nsub = jnp.where(valid, jnp.minimum(rows_left, MOE_ROWS) // MOE_STAGE, 0).astype(jnp.int32)
    return pos.reshape(t * TOP_K).astype(jnp.int32), tail_start.astype(jnp.int32), item_expert, item_start, item_nsub


def _trunk(x, conv_hist, s0, swa_fn, mk, mv, gdn_chunk, p):
    b, t, _ = x.shape
    m = b * t
    xf = x.reshape(m, D_MODEL)
    proj = _norm_matmul(xf, p["norm_mix"], p["w_in"], min(m, 1024), PROJ_TN)
    proj3 = proj.reshape(b, t, PROJ_COLS)
    hist8 = jnp.pad(conv_hist, ((0, 0), (SUBLANES - (CONV_W - 1), 0), (0, 0)))
    o_a, conv8, s_new = _gdn(proj3, hist8, s0, p["conv_w"], p["gate_par"], p["gdn_norm"], gdn_chunk,
                             GDN_SEQS_PER_STEP)
    o_b = swa_fn(proj3)
    x1, qm = _outproj(xf, o_a.reshape(m, V_A), o_b.reshape(m, Q_B), p["w_out"], p["norm_cross"], p["w_mq"], OUTPROJ_ROWS)
    tm = min(t, CROSS_ROWS)
    x2, hp, route = _cross_router(x1.reshape(b, t, D_MODEL), qm.reshape(b, t, MEM_W), mk, mv,
                                  p["w_mo"], p["norm_ffn"], p["w_router"], p["b_router"], CROSS_ROWS // tm, tm)
    return proj3, conv8[:, SUBLANES - (CONV_W - 1):], s_new, x2.reshape(m, D_MODEL), \
        hp.reshape(m * XROW_TILES, LANES), route.reshape(m, LANES)


def kernel(x_prompt, x_sample, cache_conv, state_gdn, cache_swa_k, cache_swa_v, cache_mem_k, cache_mem_v, mem_prompt, norm_mix, w_in, conv_w, gdn_a_log, gdn_dt_bias, gdn_norm, swa_sinks, rel_bias_table, w_out, norm_cross, norm_mem, w_mq, w_mk, w_mv, w_mo, norm_ffn, w_router, b_router, w_gate, b_gate, w_up, b_up, w_down, b_down, final_norm):
    depth = norm_mix.shape[0]
    assert depth == 1, "kernel is written for the single-layer trunk"
    bp, sp, _ = x_prompt.shape
    bs, ss, _ = x_sample.shape
    l = 0
    w = w_in[l]
    n_ab = 2 * GDN_HEADS
    w_perm = jnp.concatenate(
        [w[:, :CONV_CH + V_A].astype(BF16), w[:, CONV_CH + V_A + n_ab:].astype(BF16),
         w[:, CONV_CH + V_A:CONV_CH + V_A + n_ab].astype(BF16),
         jnp.zeros((D_MODEL, PROJ_COLS - w.shape[1]), BF16)], axis=1)
    gate_par = jnp.zeros((SUBLANES, LANES), F32)
    gate_par = gate_par.at[0, :GDN_HEADS].set(gdn_a_log[l]).at[1, :GDN_HEADS].set(gdn_dt_bias[l])
    p = dict(
        norm_mix=norm_mix[l].reshape(1, D_MODEL), w_in=w_perm, conv_w=conv_w[l], gate_par=gate_par,
        gdn_norm=gdn_norm[l].reshape(1, GDN_D), w_out=w_out[l].astype(BF16),
        norm_cross=norm_cross[l].reshape(1, D_MODEL), w_mq=w_mq[l].astype(BF16), w_mo=w_mo[l].astype(BF16),
        norm_ffn=norm_ffn[l].reshape(1, D_MODEL),
        w_router=_split_bf16(jnp.pad(w_router[l], ((0, 0), (0, LANES - N_EXPERTS)))),
        b_router=jnp.pad(b_router[l], (0, LANES - N_EXPERTS)).reshape(1, LANES))

    w_mkv = jnp.concatenate([w_mk[l], w_mv[l]], axis=1).astype(BF16)
    mkv = _norm_matmul(mem_prompt.reshape(bp * MEM_LEN, D_MODEL), norm_mem[l].reshape(1, D_MODEL), w_mkv,
                       min(bp * MEM_LEN, 1024), MEMKV_TN)
    mk_p = mkv[:, :MEM_W].reshape(bp, MEM_LEN * MEM_HEADS, MEM_DH)
    mv_p = mkv[:, MEM_W:].reshape(bp, MEM_LEN * MEM_HEADS, MEM_DH)

    zero_hist = jnp.zeros((bp, CONV_W - 1, CONV_CH), F32)
    zero_state = jnp.zeros((bp, GDN_HEADS, GDN_D, GDN_D), F32)
    swa_p = functools.partial(_swa_prompt, table=rel_bias_table, sinks=swa_sinks[l])
    proj_p, conv_p, st_p, x2_p, hp_p, route_p = _trunk(x_prompt, zero_hist, zero_state, swa_p, mk_p, mv_p, CHUNK, p)

    k_past = cache_swa_k[l].reshape(bs, WIN_PAST, KV_B)
    v_past = cache_swa_v[l].reshape(bs, WIN_PAST, KV_B)
    swa_s = functools.partial(_swa_sample, k_past=k_past, v_past=v_past, table=rel_bias_table, sinks=swa_sinks[l])
    proj_s, conv_s, st_s, x2_s, hp_s, route_s = _trunk(
        x_sample, cache_conv[l], state_gdn[l], swa_s,
        cache_mem_k[l].reshape(bs, MEM_LEN * MEM_HEADS, MEM_DH),
        cache_mem_v[l].reshape(bs, MEM_LEN * MEM_HEADS, MEM_DH), ss, p)

    mp, ms = bp * sp, bs * ss
    ntok = mp + ms
    top_i = jnp.concatenate([route_p[:, :TOP_K], route_s[:, :TOP_K]], axis=0).astype(jnp.int32)
    rows = ntok * TOP_K + N_EXPERTS * MOE_STAGE
    nitems = (ntok * TOP_K) // MOE_ROWS + N_EXPERTS
    pos_flat, tail_start, item_expert, item_start, item_nsub = _routing_tables(top_i, nitems)
    pos_p, pos_s = pos_flat[:mp * TOP_K], pos_flat[mp * TOP_K:]
    xs = _dispatch(hp_p, pos_p, tail_start, rows, 512)
    xs = _dispatch(hp_s, pos_s, tail_start, rows, 512, xs_prev=xs)
    ys = _moe(xs, item_expert, item_start, item_nsub,
              w_gate[l], b_gate[l].reshape(N_EXPERTS, 1, D_FF), w_up[l], b_up[l].reshape(N_EXPERTS, 1, D_FF),
              w_down[l], b_down[l].reshape(N_EXPERTS, 1, D_MODEL))
    fnorm = final_norm.reshape(1, D_MODEL)
    y_p = _combine(x2_p, route_p, fnorm, ys, pos_p, 256).reshape(bp, sp, D_MODEL)
    y_s = _combine(x2_s, route_s, fnorm, ys, pos_s, 256).reshape(bs, ss, D_MODEL)

    def kv_window(proj3, col, past=None):
        new = proj3[:, :, col * LANES:(col + 1) * LANES]
        full = new if past is None else jnp.concatenate([past, new], axis=1)
        win = full[:, -WIN_PAST:]
        return win.reshape(win.shape[0], WIN_PAST, SWA_KV_HEADS, SWA_DH)[None]

    return (y_p, y_s,
            conv_p[None], st_p[None], kv_window(proj_p, COL_KB), kv_window(proj_p, COL_VB),
            mk_p.reshape(bp, MEM_LEN, MEM_HEADS, MEM_DH)[None], mv_p.reshape(bp, MEM_LEN, MEM_HEADS, MEM_DH)[None],
            conv_s[None], st_s[None], kv_window(proj_s, COL_KB, k_past), kv_window(proj_s, COL_VB, v_past))
```
